```python
import jax, jax.numpy as jnp
from jax import lax
import numpy as np

D_MODEL = 2048
BATCH = 8
SEQ = 2048
DEPTH = 2

N_A_LAYERS = DEPTH // 2
N_B_LAYERS = DEPTH - N_A_LAYERS
N_META = 16
D_FF = 5504
ROPE_THETA = 500000.0
ROPE_FRACTION = 4
EPS = 1e-6

A_HEADS = 16
A_KV_HEADS = 4
A_HEAD_DIM = D_MODEL // A_HEADS
IDX_HEADS = 16
IDX_DIM = 64
TOPK_MAX = 256
A_QBLOCK = 64
A_SPLIT_SIZES = (A_HEADS * A_HEAD_DIM, A_KV_HEADS * A_HEAD_DIM, A_KV_HEADS * A_HEAD_DIM, IDX_HEADS * IDX_DIM, IDX_DIM)
A_IN_DIM = sum(A_SPLIT_SIZES) + IDX_HEADS

B_HEADS = 32
B_KV_HEADS = 4
B_HEAD_DIM = D_MODEL // B_HEADS
WINDOW = 128
BLOCK = 128

kernel_name = "yoco_dsa_swa_sink_macaron_hybrid"


def rms_norm(x, g):
    xf = x.astype(jnp.float32)
    y = xf * lax.rsqrt(jnp.mean(xf * xf, axis=-1, keepdims=True) + EPS)
    return (y * g.astype(jnp.float32)).astype(x.dtype)


def partial_rope(x, pos):
    dh = x.shape[-1]
    rot = dh // ROPE_FRACTION
    half = rot // 2
    inv = ROPE_THETA ** (-jnp.arange(half, dtype=jnp.float32) / half)
    ang = pos.astype(jnp.float32)[:, None] * inv[None, :]
    cos = jnp.cos(ang)[:, None, :]
    sin = jnp.sin(ang)[:, None, :]
    xf = x.astype(jnp.float32)
    x1 = xf[..., :half]
    x2 = xf[..., half:rot]
    out = jnp.concatenate([x1 * cos - x2 * sin, x2 * cos + x1 * sin, xf[..., rot:]], axis=-1)
    return out.astype(x.dtype)


def swiglu_half(h, g, w_gate, w_up, w_down):
    n = rms_norm(h, g)
    return h + 0.5 * ((jax.nn.silu(n @ w_gate) * (n @ w_up)) @ w_down)


def indexer_sparse_attention(hn, pos, w_in, q_norm, k_norm, idx_k_norm, w_out, topk):
    bsz, t_len, _ = hn.shape
    proj = hn @ w_in
    q, k, v, qi, ki, wi = jnp.split(proj, np.cumsum(A_SPLIT_SIZES).tolist(), axis=-1)
    q = partial_rope(rms_norm(q.reshape(bsz, t_len, A_HEADS, A_HEAD_DIM), q_norm), pos)
    k = partial_rope(rms_norm(k.reshape(bsz, t_len, A_KV_HEADS, A_HEAD_DIM), k_norm), pos)
    v = v.reshape(bsz, t_len, A_KV_HEADS, A_HEAD_DIM)
    qi = partial_rope(qi.reshape(bsz, t_len, IDX_HEADS, IDX_DIM), pos)
    ki = partial_rope(rms_norm(ki, idx_k_norm)[:, :, None, :], pos)[:, :, 0, :].astype(jnp.float32)
    wi = wi.astype(jnp.float32) * (IDX_HEADS * IDX_DIM) ** -0.5
    group = A_HEADS // A_KV_HEADS
    scale = A_HEAD_DIM ** -0.5
    nb = t_len // A_QBLOCK

    def to_blocks(a):
        return jnp.moveaxis(a.reshape(bsz, nb, A_QBLOCK, *a.shape[2:]), 1, 0)

    def block(args):
        qb, qib, wib, qp = args
        s_idx = jnp.einsum('bqhd,bsd->bqhs', qib.astype(jnp.float32), ki)
        score = jnp.einsum('bqhs,bqh->bqs', jax.nn.relu(s_idx), wib)
        causal = pos[None, :] <= qp[:, None]
        score = jnp.where(causal[None], score, -jnp.inf)
        _, sel = lax.top_k(score, topk)
        kg = jax.vmap(lambda kb, ib: kb[ib])(k, sel)
        vg = jax.vmap(lambda vb, ib: vb[ib])(v, sel)
        qg = qb.reshape(bsz, A_QBLOCK, A_KV_HEADS, group, A_HEAD_DIM)
        logits = jnp.einsum('bqngd,bqknd->bqngk', qg, kg).astype(jnp.float32) * scale
        valid = pos[sel] <= qp[None, :, None]
        logits = jnp.where(valid[:, :, None, None, :], logits, -jnp.inf)
        p = jax.nn.softmax(logits, axis=-1).astype(vg.dtype)
        o = jnp.einsum('bqngk,bqknd->bqngd', p, vg)
        return o.reshape(bsz, A_QBLOCK, A_HEADS * A_HEAD_DIM)

    out = lax.map(block, (to_blocks(q), to_blocks(qi), to_blocks(wi), pos.reshape(nb, A_QBLOCK)))
    out = jnp.moveaxis(out, 0, 1).reshape(bsz, t_len, A_HEADS * A_HEAD_DIM)
    return out @ w_out


def shared_kv(hn, pos, w_kv, k_norm):
    bsz, t_len, _ = hn.shape
    k, v = jnp.split(hn @ w_kv, 2, axis=-1)
    k = partial_rope(rms_norm(k.reshape(bsz, t_len, B_KV_HEADS, B_HEAD_DIM), k_norm), pos)
    v = v.reshape(bsz, t_len, B_KV_HEADS, B_HEAD_DIM)
    return k, v


def sliding_window_sink_attention(hn, pos, k, v, w_q, q_norm, sinks, w_out):
    bsz, t_len, _ = hn.shape
    group = B_HEADS // B_KV_HEADS
    scale = B_HEAD_DIM ** -0.5
    nb = t_len // BLOCK
    q = partial_rope(rms_norm((hn @ w_q).reshape(bsz, t_len, B_HEADS, B_HEAD_DIM), q_norm), pos)
    qb = jnp.moveaxis(q.reshape(bsz, nb, BLOCK, B_KV_HEADS, group, B_HEAD_DIM), 1, 0)
    kb = k.reshape(bsz, nb, BLOCK, B_KV_HEADS, B_HEAD_DIM)
    vb = v.reshape(bsz, nb, BLOCK, B_KV_HEADS, B_HEAD_DIM)
    pad = ((0, 0), (1, 0), (0, 0), (0, 0), (0, 0))
    kw = jnp.moveaxis(jnp.concatenate([jnp.pad(kb[:, :-1], pad), kb], axis=2), 1, 0)
    vw = jnp.moveaxis(jnp.concatenate([jnp.pad(vb[:, :-1], pad), vb], axis=2), 1, 0)
    c = jnp.arange(nb)[:, None, None]
    r = jnp.arange(BLOCK)[None, :, None]
    j = jnp.arange(2 * BLOCK)[None, None, :]
    rel = BLOCK + r - j
    mask = (rel >= 0) & (rel < WINDOW) & ((c > 0) | (j >= BLOCK))
    sink = sinks.astype(jnp.float32).reshape(1, B_KV_HEADS, group, 1, 1)

    def block(args):
        qc, kc, vc, mc = args
        logits = jnp.einsum('bqngd,bknd->bngqk', qc, kc).astype(jnp.float32) * scale
        logits = jnp.where(mc[None, None, None], logits, -jnp.inf)
        m = jnp.maximum(jnp.max(logits, axis=-1, keepdims=True), sink)
        e = jnp.exp(logits - m)
        p = e / (jnp.sum(e, axis=-1, keepdims=True) + jnp.exp(sink - m))
        return jnp.einsum('bngqk,bknd->bqngd', p.astype(vc.dtype), vc)

    out = lax.map(block, (qb, kw, vw, mask))
    out = jnp.moveaxis(out, 0, 1).reshape(bsz, t_len, B_HEADS * B_HEAD_DIM)
    return out @ w_out


def setup_inputs(seed: int = 0) -> dict:
    key = jax.random.key(seed)
    ks = iter(jax.random.split(key, 40))

    def nrm(shape, scale):
        return jax.random.normal(next(ks), shape, jnp.float32) * scale

    def gain(shape):
        return 1.0 + 0.02 * jax.random.normal(next(ks), shape, jnp.float32)

    d = D_MODEL
    return {
        "x": nrm((BATCH, SEQ, d), 1.0),
        "meta_tokens": nrm((N_META, d), 1.0),
        "ffn1_norm": gain((DEPTH, d)),
        "ffn1_w_gate": nrm((DEPTH, d, D_FF), d ** -0.5),
        "ffn1_w_up": nrm((DEPTH, d, D_FF), d ** -0.5),
        "ffn1_w_down": nrm((DEPTH, D_FF, d), D_FF ** -0.5),
        "ffn2_norm": gain((DEPTH, d)),
        "ffn2_w_gate": nrm((DEPTH, d, D_FF), d ** -0.5),
        "ffn2_w_up": nrm((DEPTH, d, D_FF), d ** -0.5),
        "ffn2_w_down": nrm((DEPTH, D_FF, d), D_FF ** -0.5),
        "a_norm": gain((N_A_LAYERS, d)),
        "a_w_in": nrm((N_A_LAYERS, d, A_IN_DIM), d ** -0.5),
        "a_q_norm": gain((N_A_LAYERS, A_HEAD_DIM)),
        "a_k_norm": gain((N_A_LAYERS, A_HEAD_DIM)),
        "a_idx_k_norm": gain((N_A_LAYERS, IDX_DIM)),
        "a_w_out": nrm((N_A_LAYERS, A_HEADS * A_HEAD_DIM, d), (A_HEADS * A_HEAD_DIM) ** -0.5),
        "kv_norm": gain((d,)),
        "kv_w": nrm((d, 2 * B_KV_HEADS * B_HEAD_DIM), d ** -0.5),
        "kv_k_norm": gain((B_HEAD_DIM,)),
        "b_norm": gain((N_B_LAYERS, d)),
        "b_w_q": nrm((N_B_LAYERS, d, B_HEADS * B_HEAD_DIM), d ** -0.5),
        "b_q_norm": gain((N_B_LAYERS, B_HEAD_DIM)),
        "b_sinks": nrm((N_B_LAYERS, B_HEADS), 1.0),
        "b_w_out": nrm((N_B_LAYERS, B_HEADS * B_HEAD_DIM, d), (B_HEADS * B_HEAD_DIM) ** -0.5),
    }


def reference(x, meta_tokens, ffn1_norm, ffn1_w_gate, ffn1_w_up, ffn1_w_down, ffn2_norm, ffn2_w_gate, ffn2_w_up, ffn2_w_down, a_norm, a_w_in, a_q_norm, a_k_norm, a_idx_k_norm, a_w_out, kv_norm, kv_w, kv_k_norm, b_norm, b_w_q, b_q_norm, b_sinks, b_w_out):
    bsz, s_len, _ = x.shape
    topk = min(TOPK_MAX, s_len // 4)
    t_real = s_len + N_META
    t_len = -(-t_real // BLOCK) * BLOCK
    meta = jnp.broadcast_to(meta_tokens[None].astype(x.dtype), (bsz, N_META, D_MODEL))
    h = jnp.concatenate([meta, x, jnp.zeros((bsz, t_len - t_real, D_MODEL), x.dtype)], axis=1)
    pos = jnp.arange(t_len, dtype=jnp.int32)
    k_sh = v_sh = None
    for layer in range(DEPTH):
        if layer == N_A_LAYERS:
            k_sh, v_sh = shared_kv(rms_norm(h, kv_norm), pos, kv_w, kv_k_norm)
        h = swiglu_half(h, ffn1_norm[layer], ffn1_w_gate[layer], ffn1_w_up[layer], ffn1_w_down[layer])
        if layer < N_A_LAYERS:
            i = layer
            h = h + indexer_sparse_attention(rms_norm(h, a_norm[i]), pos, a_w_in[i], a_q_norm[i], a_k_norm[i], a_idx_k_norm[i], a_w_out[i], topk)
        else:
            i = layer - N_A_LAYERS
            h = h + sliding_window_sink_attention(rms_norm(h, b_norm[i]), pos, k_sh, v_sh, b_w_q[i], b_q_norm[i], b_sinks[i], b_w_out[i])
        h = swiglu_half(h, ffn2_norm[layer], ffn2_w_gate[layer], ffn2_w_up[layer], ffn2_w_down[layer])
    return h[:, N_META:N_META + s_len]
```

```python
import functools
import math

import jax
import jax.numpy as jnp
from jax import lax
from jax.experimental import pallas as pl
from jax.experimental.pallas import tpu as pltpu

F32 = jnp.float32
BF16 = jnp.bfloat16

LANES = 128
N_META = 16
EPS = 1e-6
ROPE_THETA = 500000.0
ROPE_FRACTION = 4
A_HEADS, A_KV_HEADS, A_HEAD_DIM = 16, 4, 128
IDX_HEADS, IDX_DIM = 16, 64
TOPK_MAX = 256
B_HEADS, B_KV_HEADS, B_HEAD_DIM = 32, 4, 64
WINDOW = 128
VMEM_LIMIT = 56 * 1024 * 1024
NEG_INIT = -1e30
N_BISECT = 26


def _cparams(sem):
    return pltpu.CompilerParams(dimension_semantics=sem, vmem_limit_bytes=VMEM_LIMIT)


def _resident(shape):
    nd = len(shape)
    return pl.BlockSpec(shape, lambda *_: (0,) * nd, pipeline_mode=pl.Buffered(1))


def _rms(x, g):
    ms = jnp.mean(x * x, axis=-1, keepdims=True)
    return x * lax.rsqrt(ms + EPS) * g


def _rope(y, c, a, b, half):
    return y * c + pltpu.roll(y, LANES - half, 1) * a + pltpu.roll(y, half, 1) * b


def _nt(a, b):
    return lax.dot_general(a, b, (((1,), (1,)), ((), ())), preferred_element_type=F32)


def _ffn_kernel(h_ref, g_ref, wg_ref, wu_ref, wd_ref, o_ref, n_ref):
    @pl.when(pl.program_id(1) == 0)
    def _():
        x = h_ref[...]
        n_ref[...] = _rms(x, g_ref[...]).astype(BF16)
        o_ref[...] = x

    n = n_ref[...]
    gate = jnp.dot(n, wg_ref[...], preferred_element_type=F32)
    up = jnp.dot(n, wu_ref[...], preferred_element_type=F32)
    act = (0.5 * gate) * jax.nn.sigmoid(gate) * up
    o_ref[...] += jnp.dot(act.astype(BF16), wd_ref[...], preferred_element_type=F32)


def _ffn(h, g, wg, wu, wd, tm=512, tf=512):
    m, d = h.shape
    fp = wg.shape[1]
    return pl.pallas_call(
        _ffn_kernel,
        grid=(m // tm, fp // tf),
        in_specs=[
            pl.BlockSpec((tm, d), lambda i, f: (i, 0)),
            pl.BlockSpec((1, d), lambda i, f: (0, 0)),
            pl.BlockSpec((d, tf), lambda i, f: (0, f)),
            pl.BlockSpec((d, tf), lambda i, f: (0, f)),
            pl.BlockSpec((tf, d), lambda i, f: (f, 0)),
        ],
        out_specs=pl.BlockSpec((tm, d), lambda i, f: (i, 0)),
        out_shape=jax.ShapeDtypeStruct((m, d), F32),
        scratch_shapes=[pltpu.VMEM((tm, d), BF16)],
        compiler_params=_cparams(("parallel", "arbitrary")),
        name="ffn",
    )(h, g, wg, wu, wd)


def _oproj_kernel(h_ref, o_ref, w_ref, out_ref):
    out_ref[...] = h_ref[...] + jnp.dot(o_ref[...], w_ref[...], preferred_element_type=F32)


def _oproj(h, o, w, tm=512):
    m, d = h.shape
    return pl.pallas_call(
        _oproj_kernel,
        grid=(m // tm,),
        in_specs=[
            pl.BlockSpec((tm, d), lambda i: (i, 0)),
            pl.BlockSpec((tm, o.shape[1]), lambda i: (i, 0)),
            _resident(w.shape),
        ],
        out_specs=pl.BlockSpec((tm, d), lambda i: (i, 0)),
        out_shape=jax.ShapeDtypeStruct((m, d), F32),
        compiler_params=_cparams(("parallel",)),
        name="oproj",
    )(h, o, w)


def _aproj_kernel(h_ref, g_ref, wq_ref, wk_ref, wvt_ref, wqi_ref, wkw_ref, qn_ref, kn_ref, ikn_ref,
                  c128_ref, a128_ref, b128_ref, c64_ref, a64_ref, b64_ref,
                  q_ref, k_ref, vt_ref, qi_ref, ki2_ref, wit_ref):
    tm = h_ref.shape[0]
    hn = _rms(h_ref[...], g_ref[...]).astype(BF16)
    c128, a128, b128 = c128_ref[...], a128_ref[...], b128_ref[...]
    c64, a64, b64 = c64_ref[...], a64_ref[...], b64_ref[...]
    half128 = A_HEAD_DIM // ROPE_FRACTION // 2
    half64 = IDX_DIM // ROPE_FRACTION // 2

    qf = jnp.dot(hn, wq_ref[...], preferred_element_type=F32)
    for hd in range(A_HEADS):
        sl = slice(hd * LANES, (hd + 1) * LANES)
        q_ref[:, sl] = _rope(_rms(qf[:, sl], qn_ref[...]), c128, a128, b128, half128).astype(BF16)

    kf = jnp.dot(hn, wk_ref[...], preferred_element_type=F32)
    for hd in range(A_KV_HEADS):
        sl = slice(hd * LANES, (hd + 1) * LANES)
        k_ref[:, sl] = _rope(_rms(kf[:, sl], kn_ref[...]), c128, a128, b128, half128).astype(BF16)

    vt = _nt(wvt_ref[...], hn)
    for c in range(tm // LANES):
        vt_ref[c] = vt[:, c * LANES:(c + 1) * LANES].astype(BF16)

    qif = jnp.dot(hn, wqi_ref[...], preferred_element_type=F32)
    for s in range(IDX_HEADS * IDX_DIM // LANES):
        sl = slice(s * LANES, (s + 1) * LANES)
        qi_ref[:, sl] = _rope(qif[:, sl], c64, a64, b64, half64).astype(BF16)

    kw = jnp.dot(hn, wkw_ref[...], preferred_element_type=F32)
    lane = lax.broadcasted_iota(jnp.int32, kw.shape, 1)
    ms = jnp.sum(jnp.where(lane < IDX_DIM, kw * kw, 0.0), axis=-1, keepdims=True) * (1.0 / IDX_DIM)
    ki = _rope(kw * lax.rsqrt(ms + EPS) * ikn_ref[...], c64, a64, b64, half64)
    ki2_ref[:, :LANES] = ki.astype(BF16)
    ki2_ref[:, LANES:] = pltpu.roll(ki, IDX_DIM, 1).astype(BF16)
    wit_ref[...] = kw.T[IDX_DIM:IDX_DIM + IDX_HEADS, :] * ((IDX_HEADS * IDX_DIM) ** -0.5)


def _aproj(h, g, wq, wk, wvt, wqi, wkw, qn, kn, ikn, t128, t64, tm=256):
    m, d = h.shape
    row = lambda w: pl.BlockSpec((tm, w), lambda i: (i, 0))
    tab = pl.BlockSpec((tm, LANES), lambda i: (i, 0))
    vec = lambda w: pl.BlockSpec((1, w), lambda i: (0, 0))
    kvd = wk.shape[1]
    return pl.pallas_call(
        _aproj_kernel,
        grid=(m // tm,),
        in_specs=[row(d), vec(d), _resident(wq.shape), _resident(wk.shape), _resident(wvt.shape),
                  _resident(wqi.shape), _resident(wkw.shape), vec(LANES), vec(LANES), vec(LANES),
                  tab, tab, tab, tab, tab, tab],
        out_specs=[row(wq.shape[1]), row(kvd),
                   pl.BlockSpec((tm // LANES, kvd, LANES), lambda i: (i, 0, 0)),
                   row(wqi.shape[1]), row(2 * LANES),
                   pl.BlockSpec((IDX_HEADS, tm), lambda i: (0, i))],
        out_shape=[jax.ShapeDtypeStruct((m, wq.shape[1]), BF16),
                   jax.ShapeDtypeStruct((m, kvd), BF16),
                   jax.ShapeDtypeStruct((m // LANES, kvd, LANES), BF16),
                   jax.ShapeDtypeStruct((m, wqi.shape[1]), BF16),
                   jax.ShapeDtypeStruct((m, 2 * LANES), BF16),
                   jax.ShapeDtypeStruct((IDX_HEADS, m), F32)],
        compiler_params=_cparams(("parallel",)),
        name="aproj",
    )(h, g, wq, wk, wvt, wqi, wkw, qn, kn, ikn, *t128, *t64)


def _dsa_kernel(topk, qi_ref, wit_ref, q_ref, ki2_ref, k_ref, vt_ref, o_ref, score_ref, acc_ref):
    qb = q_ref.shape[0]
    j = pl.program_id(1)
    nkc = j + 1
    kf = jnp.float32(topk)
    group = A_HEADS // A_KV_HEADS
    n_slab = IDX_HEADS * IDX_DIM // LANES

    qi = qi_ref[...]
    qs = jnp.concatenate([qi[:, s * LANES:(s + 1) * LANES] for s in range(n_slab)], axis=0)
    krow = lax.broadcasted_iota(jnp.int32, (LANES, qb), 0)
    qcol = lax.broadcasted_iota(jnp.int32, (LANES, qb), 1)

    def score_chunk(c, carry):
        mn, mx = carry
        r0 = pl.multiple_of(c * LANES, LANES)
        kc = ki2_ref[pl.ds(r0, LANES), :]
        kcat = jnp.concatenate([kc[:, :LANES], kc[:, LANES:]], axis=0)
        s2 = _nt(kcat, qs)
        acc = jnp.zeros((LANES, qb), F32)
        for s in range(n_slab):
            for par in range(2):
                blk = s2[par * LANES:(par + 1) * LANES, s * qb:(s + 1) * qb]
                acc = acc + jnp.maximum(blk, 0.0) * wit_ref[2 * s + par:2 * s + par + 1, :]
        causal = (krow + c * LANES) <= (qcol + j * qb)
        score_ref[pl.ds(r0, LANES), :] = jnp.where(causal, acc, -jnp.inf)
        mn = jnp.minimum(mn, jnp.min(jnp.where(causal, acc, jnp.inf), axis=0, keepdims=True))
        mx = jnp.maximum(mx, jnp.max(jnp.where(causal, acc, -jnp.inf), axis=0, keepdims=True))
        return mn, mx

    mn, mx = lax.fori_loop(0, nkc, score_chunk,
                           (jnp.full((1, qb), jnp.inf, F32), jnp.full((1, qb), -jnp.inf, F32)))

    def reduce_keys(fn, init, combine):
        def body(c, acc):
            r0 = pl.multiple_of(c * LANES, LANES)
            blk = score_ref[pl.ds(r0, LANES), :]
            kidx = (krow + c * LANES).astype(F32)
            return combine(acc, fn(blk, kidx).reshape(LANES // 8, 8, qb))
        acc8 = lax.fori_loop(0, nkc, body, jnp.full((8, qb), init, F32))
        return acc8

    def count(pred):
        acc8 = reduce_keys(lambda blk, kidx: jnp.where(pred(blk, kidx), 1.0, 0.0), 0.0,
                           lambda a, x: a + jnp.sum(x, axis=0))
        return jnp.sum(acc8, axis=0, keepdims=True)

    def min_where(pred):
        acc8 = reduce_keys(lambda blk, kidx: jnp.where(pred(blk, kidx), blk, jnp.inf), jnp.inf,
                           lambda a, x: jnp.minimum(a, jnp.min(x, axis=0)))
        return jnp.min(acc8, axis=0, keepdims=True)

    def bisect(_, lohi):
        lo, hi = lohi
        mid = 0.5 * lo + 0.5 * hi
        ok = count(lambda blk, kidx: blk >= mid) >= kf
        return jnp.where(ok, mid, lo), jnp.where(ok, hi, mid)

    lo, _ = lax.fori_loop(0, N_BISECT, bisect, (mn, mx))
    c_lo = count(lambda blk, kidx: blk >= lo)
    inexact = jnp.max(jnp.where(c_lo > kf, 1.0, 0.0)) > 0.0

    def exact_threshold(lo):
        def step(state):
            lo, _ = state
            m = min_where(lambda blk, kidx: blk >= lo)
            adv = count(lambda blk, kidx: blk > m) >= kf
            nxt = min_where(lambda blk, kidx: blk > m)
            return jnp.where(adv, nxt, lo), jnp.max(jnp.where(adv, 1.0, 0.0))

        lo, _ = lax.while_loop(lambda st: st[1] > 0.0, step, (lo, jnp.float32(1.0)))
        m = min_where(lambda blk, kidx: blk >= lo)
        r = kf - count(lambda blk, kidx: blk > m)
        tied = count(lambda blk, kidx: blk == m) > r

        def idx_bisect(_, ab):
            a, b = ab
            mid = jnp.floor(0.5 * (a + b))
            ok = count(lambda blk, kidx: (blk == m) & (kidx <= mid)) >= r
            return jnp.where(ok, a, mid), jnp.where(ok, mid, b)

        t_all = score_ref.shape[0]
        _, b = lax.fori_loop(0, int(math.ceil(math.log2(t_all))) + 1, idx_bisect,
                             (jnp.full((1, qb), -1.0, F32), jnp.full((1, qb), float(t_all - 1), F32)))
        return m, jnp.where(tied, b, float(t_all))

    thr, cidx = lax.cond(inexact, exact_threshold,
                         lambda lo: (lo, jnp.full((1, qb), float(score_ref.shape[0]), F32)), lo)

    def to_bias(c, _):
        r0 = pl.multiple_of(c * LANES, LANES)
        blk = score_ref[pl.ds(r0, LANES), :]
        kidx = (krow + c * LANES).astype(F32)
        sel = (blk > thr) | ((blk == thr) & (kidx <= cidx))
        score_ref[pl.ds(r0, LANES), :] = jnp.where(sel, 0.0, -jnp.inf)
        return 0

    lax.fori_loop(0, nkc, to_bias, 0)

    scale = A_HEAD_DIM ** -0.5
    qg = [jnp.concatenate([q_ref[:, (n * group + g) * LANES:(n * group + g + 1) * LANES] for g in range(group)], axis=0)
          for n in range(A_KV_HEADS)]
    acc_ref[...] = jnp.zeros_like(acc_ref)

    def attend(c, carry):
        ms, ls = carry
        r0 = pl.multiple_of(c * LANES, LANES)
        bias = score_ref[pl.ds(r0, LANES), :]
        bias = jnp.concatenate([bias] * group, axis=1)
        new_ms, new_ls = [], []
        for n in range(A_KV_HEADS):
            kc = k_ref[pl.ds(r0, LANES), n * LANES:(n + 1) * LANES]
            st = _nt(kc, qg[n]) * scale + bias
            m_new = jnp.maximum(ms[n], jnp.max(st, axis=0, keepdims=True))
            p = jnp.exp(st - m_new)
            alpha = jnp.exp(ms[n] - m_new)
            new_ls.append(alpha * ls[n] + jnp.sum(p, axis=0, keepdims=True))
            new_ms.append(m_new)
            vtc = vt_ref[c, n * LANES:(n + 1) * LANES, :]
            acc_ref[n] = alpha * acc_ref[n] + jnp.dot(vtc, p.astype(BF16), preferred_element_type=F32)
        return tuple(new_ms), tuple(new_ls)

    init_m = tuple(jnp.full((1, group * qb), NEG_INIT, F32) for _ in range(A_KV_HEADS))
    init_l = tuple(jnp.zeros((1, group * qb), F32) for _ in range(A_KV_HEADS))
    _, ls = lax.fori_loop(0, nkc, attend, (init_m, init_l))

    for n in range(A_KV_HEADS):
        ot = acc_ref[n] / ls[n]
        for g in range(group):
            hd = n * group + g
            o_ref[:, hd * LANES:(hd + 1) * LANES] = ot[:, g * qb:(g + 1) * qb].T.astype(BF16)


def _dsa(qi, wit, q, ki2, k, vt, bsz, t_len, topk):
    qb = LANES
    nq = t_len // qb
    m = q.shape[0]
    kvd = k.shape[1]
    group = A_HEADS // A_KV_HEADS
    qrow = lambda w: pl.BlockSpec((qb, w), lambda b, j: (b * nq + j, 0))
    return pl.pallas_call(
        functools.partial(_dsa_kernel, topk),
        grid=(bsz, nq),
        in_specs=[qrow(qi.shape[1]),
                  pl.BlockSpec((IDX_HEADS, qb), lambda b, j: (0, b * nq + j)),
                  qrow(q.shape[1]),
                  pl.BlockSpec((t_len, 2 * LANES), lambda b, j: (b, 0)),
                  pl.BlockSpec((t_len, kvd), lambda b, j: (b, 0)),
                  pl.BlockSpec((nq, kvd, LANES), lambda b, j: (b, 0, 0))],
        out_specs=qrow(q.shape[1]),
        out_shape=jax.ShapeDtypeStruct((m, q.shape[1]), BF16),
        scratch_shapes=[pltpu.VMEM((t_len, qb), F32),
                        pltpu.VMEM((A_KV_HEADS, A_HEAD_DIM, group * qb), F32)],
        compiler_params=_cparams(("parallel", "arbitrary")),
        name="dsa",
    )(qi, wit, q, ki2, k, vt)


def _rms64(x, g, pavg):
    x2 = x * x
    hi = x2.astype(BF16)
    lo = (x2 - hi.astype(F32)).astype(BF16)
    ms = jnp.dot(hi, pavg, preferred_element_type=F32) + jnp.dot(lo, pavg, preferred_element_type=F32)
    return x * lax.rsqrt(ms + EPS) * g


def _split_halves(x, par):
    lane = lax.broadcasted_iota(jnp.int32, x.shape, 1)
    keep = jnp.where(lane < B_HEAD_DIM if par == 0 else lane >= B_HEAD_DIM, x, 0.0)
    moved = pltpu.roll(keep, B_HEAD_DIM, 1)
    return (keep, moved) if par == 0 else (moved, keep)


def _kvproj_kernel(h_ref, g_ref, w_ref, kn_ref, pavg_ref, c64_ref, a64_ref, b64_ref, k2_ref, v2_ref):
    hn = _rms(h_ref[...], g_ref[...]).astype(BF16)
    kv = jnp.dot(hn, w_ref[...], preferred_element_type=F32)
    half = B_HEAD_DIM // ROPE_FRACTION // 2
    kvd = B_KV_HEADS * B_HEAD_DIM
    for s in range(kvd // LANES):
        ks = kv[:, s * LANES:(s + 1) * LANES]
        ks = _rope(_rms64(ks, kn_ref[...], pavg_ref[...]), c64_ref[...], a64_ref[...], b64_ref[...], half)
        vs = kv[:, kvd + s * LANES:kvd + (s + 1) * LANES]
        for par in range(2):
            n = 2 * s + par
            for src, dst in ((ks, k2_ref), (vs, v2_ref)):
                low, high = _split_halves(src, par)
                dst[:, (2 * n) * LANES:(2 * n + 1) * LANES] = low.astype(BF16)
                dst[:, (2 * n + 1) * LANES:(2 * n + 2) * LANES] = high.astype(BF16)


def _kvproj(h, g, w, kn, pavg, t64, tm=512):
    m, d = h.shape
    row = lambda w_: pl.BlockSpec((tm, w_), lambda i: (i, 0))
    vec = lambda w_: pl.BlockSpec((1, w_), lambda i: (0, 0))
    tab = pl.BlockSpec((tm, LANES), lambda i: (i, 0))
    wide = B_KV_HEADS * 2 * LANES
    return pl.pallas_call(
        _kvproj_kernel,
        grid=(m // tm,),
        in_specs=[row(d), vec(d), _resident(w.shape), vec(LANES), _resident(pavg.shape), tab, tab, tab],
        out_specs=[row(wide), row(wide)],
        out_shape=[jax.ShapeDtypeStruct((m, wide), BF16)] * 2,
        compiler_params=_cparams(("parallel",)),
        name="kvproj",
    )(h, g, w, kn, pavg, *t64)


def _bqproj_kernel(h_ref, g_ref, w_ref, qn_ref, pavg_ref, c64_ref, a64_ref, b64_ref, q_ref):
    hn = _rms(h_ref[...], g_ref[...]).astype(BF16)
    qf = jnp.dot(hn, w_ref[...], preferred_element_type=F32)
    half = B_HEAD_DIM // ROPE_FRACTION // 2
    scale = B_HEAD_DIM ** -0.5
    for s in range(B_HEADS * B_HEAD_DIM // LANES):
        sl = slice(s * LANES, (s + 1) * LANES)
        y = _rope(_rms64(qf[:, sl], qn_ref[...], pavg_ref[...]), c64_ref[...], a64_ref[...], b64_ref[...], half)
        q_ref[:, sl] = (y * scale).astype(BF16)


def _bqproj(h, g, w, qn, pavg, t64, tm=256):
    m, d = h.shape
    row = lambda w_: pl.BlockSpec((tm, w_), lambda i: (i, 0))
    vec = lambda w_: pl.BlockSpec((1, w_), lambda i: (0, 0))
    tab = pl.BlockSpec((tm, LANES), lambda i: (i, 0))
    return pl.pallas_call(
        _bqproj_kernel,
        grid=(m // tm,),
        in_specs=[row(d), vec(d), _resident(w.shape), vec(LANES), _resident(pavg.shape), tab, tab, tab],
        out_specs=row(w.shape[1]),
        out_shape=jax.ShapeDtypeStruct((m, w.shape[1]), BF16),
        compiler_params=_cparams(("parallel",)),
        name="bqproj",
    )(h, g, w, qn, pavg, *t64)


def _swa_kernel(sink_ref, q_ref, kp_ref, kc_ref, vp_ref, vc_ref, o_ref):
    qb = q_ref.shape[0]
    j = pl.program_id(1)
    per_kv = B_HEADS // B_KV_HEADS
    n_tile = per_kv // 2
    r = lax.broadcasted_iota(jnp.int32, (qb, 2 * qb), 0)
    col = lax.broadcasted_iota(jnp.int32, (qb, 2 * qb), 1)
    rel = qb + r - col
    first_col = jnp.where(j > 0, 0, qb)
    mask = (rel >= 0) & (rel < WINDOW) & (col >= first_col)
    for n in range(B_KV_HEADS):
        lo = slice(2 * n * LANES, (2 * n + 1) * LANES)
        hi = slice((2 * n + 1) * LANES, (2 * n + 2) * LANES)
        kcat = jnp.concatenate([kp_ref[:, lo], kc_ref[:, lo], kp_ref[:, hi], kc_ref[:, hi]], axis=0)
        vcat = jnp.concatenate([vp_ref[:, lo], vc_ref[:, lo], vp_ref[:, hi], vc_ref[:, hi]], axis=0)
        qst = jnp.concatenate([q_ref[:, (n * n_tile + s) * LANES:(n * n_tile + s + 1) * LANES] for s in range(n_tile)], axis=0)
        logits = _nt(qst, kcat)
        rows = []
        for s in range(n_tile):
            halves = []
            for par in range(2):
                sink = sink_ref[n * per_kv + 2 * s + par]
                lg = jnp.where(mask, logits[s * qb:(s + 1) * qb, par * 2 * qb:(par + 1) * 2 * qb], -jnp.inf)
                m = jnp.maximum(jnp.max(lg, axis=-1, keepdims=True), sink)
                e = jnp.exp(lg - m)
                p = e / (jnp.sum(e, axis=-1, keepdims=True) + jnp.exp(sink - m))
                halves.append(p.astype(BF16))
            rows.append(jnp.concatenate(halves, axis=1))
        p2 = jnp.concatenate(rows, axis=0)
        o = jnp.dot(p2, vcat, preferred_element_type=F32)
        for s in range(n_tile):
            t = n * n_tile + s
            o_ref[:, t * LANES:(t + 1) * LANES] = o[s * qb:(s + 1) * qb].astype(BF16)


def _swa(sinks, q, k2, v2, bsz, t_len):
    qb = LANES
    nq = t_len // qb
    m, width = q.shape
    wide = k2.shape[1]
    cur = lambda w: pl.BlockSpec((qb, w), lambda b, j: (b * nq + j, 0))
    prev = lambda w: pl.BlockSpec((qb, w), lambda b, j: (b * nq + jnp.maximum(j - 1, 0), 0))
    return pl.pallas_call(
        _swa_kernel,
        grid=(bsz, nq),
        in_specs=[pl.BlockSpec(memory_space=pltpu.SMEM), cur(width), prev(wide), cur(wide), prev(wide), cur(wide)],
        out_specs=cur(width),
        out_shape=jax.ShapeDtypeStruct((m, width), BF16),
        compiler_params=_cparams(("parallel", "parallel")),
        name="swa",
    )(sinks, q, k2, k2, v2, v2)


def _rope_tables(t_len, head_dim, reps):
    half = head_dim // ROPE_FRACTION // 2
    inv = ROPE_THETA ** (-jnp.arange(half, dtype=F32) / half)
    ang = jnp.arange(t_len, dtype=jnp.int32).astype(F32)[:, None] * inv[None, :]
    cos, sin = jnp.cos(ang), jnp.sin(ang)
    lp = jnp.arange(LANES) % head_dim
    idx = lp % half
    c = jnp.where(lp < 2 * half, cos[:, idx], 1.0)
    a = jnp.where(lp < half, -sin[:, idx], 0.0)
    b = jnp.where((lp >= half) & (lp < 2 * half), sin[:, idx], 0.0)
    return tuple(jnp.tile(t, (reps, 1)) for t in (c, a, b))


def _pad_cols(w, mult):
    pad = (-w.shape[-1]) % mult
    return jnp.pad(w, ((0, 0), (0, pad))) if pad else w


def kernel(x, meta_tokens, ffn1_norm, ffn1_w_gate, ffn1_w_up, ffn1_w_down, ffn2_norm, ffn2_w_gate, ffn2_w_up, ffn2_w_down, a_norm, a_w_in, a_q_norm, a_k_norm, a_idx_k_norm, a_w_out, kv_norm, kv_w, kv_k_norm, b_norm, b_w_q, b_q_norm, b_sinks, b_w_out):
    bsz, s_len, d = x.shape
    depth = ffn1_norm.shape[0]
    n_a = a_norm.shape[0]
    topk = min(TOPK_MAX, s_len // 4)
    t_real = s_len + N_META
    t_len = -(-t_real // LANES) * LANES
    meta = jnp.broadcast_to(meta_tokens[None].astype(x.dtype), (bsz, N_META, d))
    h = jnp.concatenate([meta, x, jnp.zeros((bsz, t_len - t_real, d), x.dtype)], axis=1).reshape(bsz * t_len, d)

    t128 = _rope_tables(t_len, A_HEAD_DIM, bsz)
    t64 = _rope_tables(t_len, B_HEAD_DIM, bsz)
    seg = jnp.arange(LANES) // B_HEAD_DIM
    pavg = jnp.where(seg[:, None] == seg[None, :], 1.0 / B_HEAD_DIM, 0.0).astype(BF16)
    tile2 = lambda g: jnp.tile(g.astype(F32), 2)[None, :]

    def ffn(h, g, wg, wu, wd):
        wg = _pad_cols(wg.astype(BF16), 512)
        wu = _pad_cols(wu.astype(BF16), 512)
        wd = jnp.pad(wd.astype(BF16), ((0, (-wd.shape[0]) % 512), (0, 0)))
        return _ffn(h, g[None, :], wg, wu, wd)

    q_dim, kv_dim, qi_dim = A_HEADS * A_HEAD_DIM, A_KV_HEADS * A_HEAD_DIM, IDX_HEADS * IDX_DIM
    k2 = v2 = None
    for layer in range(depth):
        if layer == n_a:
            k2, v2 = _kvproj(h, kv_norm[None, :], kv_w.astype(BF16), tile2(kv_k_norm), pavg, t64)
        h = ffn(h, ffn1_norm[layer], ffn1_w_gate[layer], ffn1_w_up[layer], ffn1_w_down[layer])
        if layer < n_a:
            i = layer
            w = a_w_in[i].astype(BF16)
            o0, o1, o2, o3 = q_dim, q_dim + kv_dim, q_dim + 2 * kv_dim, q_dim + 2 * kv_dim + qi_dim
            wkw = _pad_cols(w[:, o3:], LANES)
            ikn = _pad_cols(a_idx_k_norm[i][None, :], LANES)
            q, k, vt, qi, ki2, wit = _aproj(
                h, a_norm[i][None, :], w[:, :o0], w[:, o0:o1], w[:, o1:o2].T, w[:, o2:o3], wkw,
                a_q_norm[i][None, :], a_k_norm[i][None, :], ikn, t128, t64)
            o = _dsa(qi, wit, q, ki2, k, vt, bsz, t_len, topk)
            h = _oproj(h, o, a_w_out[i].astype(BF16))
        else:
            i = layer - n_a
            q = _bqproj(h, b_norm[i][None, :], b_w_q[i].astype(BF16), tile2(b_q_norm[i]), pavg, t64)
            o = _swa(b_sinks[i].astype(F32), q, k2, v2, bsz, t_len)
            h = _oproj(h, o, b_w_out[i].astype(BF16))
        h = ffn(h, ffn2_norm[layer], ffn2_w_gate[layer], ffn2_w_up[layer], ffn2_w_down[layer])
    return h.reshape(bsz, t_len, d)[:, N_META:N_META + s_len]
```

```python
import functools
import math

import jax
import jax.numpy as jnp
from jax import lax
from jax.experimental import pallas as pl
from jax.experimental.pallas import tpu as pltpu

F32 = jnp.float32
BF16 = jnp.bfloat16

LANES = 128
N_META = 16
EPS = 1e-6
ROPE_THETA = 500000.0
ROPE_FRACTION = 4
A_HEADS, A_KV_HEADS, A_HEAD_DIM = 16, 4, 128
IDX_HEADS, IDX_DIM = 16, 64
TOPK_MAX = 256
B_HEADS, B_KV_HEADS, B_HEAD_DIM = 32, 4, 64
WINDOW = 128
VMEM_LIMIT = 56 * 1024 * 1024
NEG_INIT = -1e30
N_BISECT = 26
VT_ROWS = A_HEAD_DIM + 16


def _cparams(sem):
    return pltpu.CompilerParams(dimension_semantics=sem, vmem_limit_bytes=VMEM_LIMIT)


def _resident(shape):
    nd = len(shape)
    return pl.BlockSpec(shape, lambda *_: (0,) * nd, pipeline_mode=pl.Buffered(1))


def _rms(x, g):
    ms = jnp.mean(x * x, axis=-1, keepdims=True)
    return x * lax.rsqrt(ms + EPS) * g


def _rope(y, c, a, b, half):
    return y * c + pltpu.roll(y, LANES - half, 1) * a + pltpu.roll(y, half, 1) * b


def _nt(a, b):
    return lax.dot_general(a, b, (((1,), (1,)), ((), ())), preferred_element_type=F32)


def _ffn_kernel(tail, h_ref, g_ref, wg_ref, wu_ref, wd_ref, o_ref, n_ref):
    f = pl.program_id(1)
    last = pl.num_programs(1) - 1
    tf = wg_ref.shape[-1]

    @pl.when(f == 0)
    def _():
        x = h_ref[...]
        n_ref[...] = _rms(x, g_ref[...]).astype(BF16)
        o_ref[...] = x

    def step(width):
        n = n_ref[...]
        gate = jnp.dot(n, wg_ref[:, :width], preferred_element_type=F32)
        up = jnp.dot(n, wu_ref[:, :width], preferred_element_type=F32)
        act = (0.5 * gate) * jax.nn.sigmoid(gate) * up
        o_ref[...] += jnp.dot(act.astype(BF16), wd_ref[:width, :], preferred_element_type=F32)

    if tail == tf:
        step(tf)
    else:
        pl.when(f < last)(lambda: step(tf))
        pl.when(f == last)(lambda: step(tail))


def _ffn(h, g, wg, wu, wd, layer, tm=512, tf=512):
    m, d = h.shape
    assert m % tm == 0, (m, tm)
    ff = wg.shape[-1]
    nf = pl.cdiv(ff, tf)
    tail = ff - (nf - 1) * tf
    assert tail % LANES == 0, ff
    return pl.pallas_call(
        functools.partial(_ffn_kernel, tail),
        grid=(m // tm, nf),
        in_specs=[
            pl.BlockSpec((tm, d), lambda i, f: (i, 0)),
            pl.BlockSpec((1, d), lambda i, f: (0, 0)),
            pl.BlockSpec((None, d, tf), lambda i, f: (layer, 0, f)),
            pl.BlockSpec((None, d, tf), lambda i, f: (layer, 0, f)),
            pl.BlockSpec((None, tf, d), lambda i, f: (layer, f, 0)),
        ],
        out_specs=pl.BlockSpec((tm, d), lambda i, f: (i, 0)),
        out_shape=jax.ShapeDtypeStruct((m, d), F32),
        scratch_shapes=[pltpu.VMEM((tm, d), BF16)],
        compiler_params=_cparams(("parallel", "arbitrary")),
        name="ffn",
    )(h, g, wg, wu, wd)


def _oproj_kernel(h_ref, o_ref, w_ref, out_ref):
    out_ref[...] = h_ref[...] + jnp.dot(o_ref[...], w_ref[...], preferred_element_type=F32)


def _oproj(h, o, w, tm=512):
    m, d = h.shape
    assert m % tm == 0, (m, tm)
    return pl.pallas_call(
        _oproj_kernel,
        grid=(m // tm,),
        in_specs=[
            pl.BlockSpec((tm, d), lambda i: (i, 0)),
            pl.BlockSpec((tm, o.shape[1]), lambda i: (i, 0)),
            _resident(w.shape),
        ],
        out_specs=pl.BlockSpec((tm, d), lambda i: (i, 0)),
        out_shape=jax.ShapeDtypeStruct((m, d), F32),
        compiler_params=_cparams(("parallel",)),
        name="oproj",
    )(h, o, w)


def _aproj_kernel(h_ref, g_ref, wq_ref, wk_ref, wvt_ref, wqi_ref, wkw_ref, qn_ref, kn_ref, ikn_ref,
                  c128_ref, a128_ref, b128_ref, c64_ref, a64_ref, b64_ref,
                  q_ref, k_ref, vt_ref, qi_ref, ki2_ref, wit_ref):
    tm = h_ref.shape[0]
    hn = _rms(h_ref[...], g_ref[...]).astype(BF16)
    c128, a128, b128 = c128_ref[...], a128_ref[...], b128_ref[...]
    c64, a64, b64 = c64_ref[...], a64_ref[...], b64_ref[...]
    half128 = A_HEAD_DIM // ROPE_FRACTION // 2
    half64 = IDX_DIM // ROPE_FRACTION // 2

    qf = jnp.dot(hn, wq_ref[...], preferred_element_type=F32)
    for hd in range(A_HEADS):
        sl = slice(hd * LANES, (hd + 1) * LANES)
        q_ref[:, sl] = _rope(_rms(qf[:, sl], qn_ref[...]), c128, a128, b128, half128).astype(BF16)

    kf = jnp.dot(hn, wk_ref[...], preferred_element_type=F32)
    for hd in range(A_KV_HEADS):
        sl = slice(hd * LANES, (hd + 1) * LANES)
        k_ref[:, sl] = _rope(_rms(kf[:, sl], kn_ref[...]), c128, a128, b128, half128).astype(BF16)

    vt = _nt(wvt_ref[...], hn)
    tail = jnp.where(lax.broadcasted_iota(jnp.int32, (VT_ROWS - A_HEAD_DIM, LANES), 0) == 0, 1.0, 0.0).astype(BF16)
    for c in range(tm // LANES):
        for n in range(A_KV_HEADS):
            vt_ref[c, n * VT_ROWS:n * VT_ROWS + A_HEAD_DIM, :] = (
                vt[n * A_HEAD_DIM:(n + 1) * A_HEAD_DIM, c * LANES:(c + 1) * LANES].astype(BF16))
            vt_ref[c, n * VT_ROWS + A_HEAD_DIM:(n + 1) * VT_ROWS, :] = tail

    qif = jnp.dot(hn, wqi_ref[...], preferred_element_type=F32)
    for s in range(IDX_HEADS * IDX_DIM // LANES):
        sl = slice(s * LANES, (s + 1) * LANES)
        qi_ref[:, sl] = _rope(qif[:, sl], c64, a64, b64, half64).astype(BF16)

    kw = jnp.dot(hn, wkw_ref[...], preferred_element_type=F32)
    lane = lax.broadcasted_iota(jnp.int32, kw.shape, 1)
    ms = jnp.sum(jnp.where(lane < IDX_DIM, kw * kw, 0.0), axis=-1, keepdims=True) * (1.0 / IDX_DIM)
    ki = _rope(kw * lax.rsqrt(ms + EPS) * ikn_ref[...], c64, a64, b64, half64)
    ki2_ref[:, :LANES] = ki.astype(BF16)
    ki2_ref[:, LANES:] = pltpu.roll(ki, IDX_DIM, 1).astype(BF16)
    wit_ref[...] = kw.T[IDX_DIM:IDX_DIM + IDX_HEADS, :] * ((IDX_HEADS * IDX_DIM) ** -0.5)


def _aproj(h, g, wq, wk, wvt, wqi, wkw, qn, kn, ikn, t128, t64, tm=256):
    m, d = h.shape
    assert m % tm == 0, (m, tm)
    row = lambda w: pl.BlockSpec((tm, w), lambda i: (i, 0))
    tab = pl.BlockSpec((tm, LANES), lambda i: (i, 0))
    vec = lambda w: pl.BlockSpec((1, w), lambda i: (0, 0))
    kvd = wk.shape[1]
    return pl.pallas_call(
        _aproj_kernel,
        grid=(m // tm,),
        in_specs=[row(d), vec(d), _resident(wq.shape), _resident(wk.shape), _resident(wvt.shape),
                  _resident(wqi.shape), _resident(wkw.shape), vec(LANES), vec(LANES), vec(LANES),
                  tab, tab, tab, tab, tab, tab],
        out_specs=[row(wq.shape[1]), row(kvd),
                   pl.BlockSpec((tm // LANES, A_KV_HEADS * VT_ROWS, LANES), lambda i: (i, 0, 0)),
                   row(wqi.shape[1]), row(2 * LANES),
                   pl.BlockSpec((IDX_HEADS, tm), lambda i: (0, i))],
        out_shape=[jax.ShapeDtypeStruct((m, wq.shape[1]), BF16),
                   jax.ShapeDtypeStruct((m, kvd), BF16),
                   jax.ShapeDtypeStruct((m // LANES, A_KV_HEADS * VT_ROWS, LANES), BF16),
                   jax.ShapeDtypeStruct((m, wqi.shape[1]), BF16),
                   jax.ShapeDtypeStruct((m, 2 * LANES), BF16),
                   jax.ShapeDtypeStruct((IDX_HEADS, m), F32)],
        compiler_params=_cparams(("parallel",)),
        name="aproj",
    )(h, g, wq, wk, wvt, wqi, wkw, qn, kn, ikn, *t128, *t64)


def _dsa_kernel(topk, nq, qi_ref, wit_ref, q_ref, ki2_ref, k_ref, vt_ref, o_ref, score_ref, acc_ref):
    qb = q_ref.shape[0]
    ck = 2 * LANES
    j = pl.program_id(1)
    npair = (j + 2) // 2
    kf = jnp.float32(topk)
    group = A_HEADS // A_KV_HEADS
    n_slab = IDX_HEADS * IDX_DIM // LANES
    t_all = score_ref.shape[0]

    def chunk_rows(i):
        c0 = 2 * i
        c1 = jnp.minimum(c0 + 1, nq - 1)
        return pl.multiple_of(c0 * LANES, LANES), pl.multiple_of(c1 * LANES, LANES), c0, c1

    kpos = lax.broadcasted_iota(jnp.int32, (ck, qb), 0)
    qpos = lax.broadcasted_iota(jnp.int32, (ck, qb), 1) + j * qb

    qi = qi_ref[...]
    qs = jnp.concatenate([qi[:, s * LANES:(s + 1) * LANES] for s in range(n_slab)], axis=0)

    def score_pair(i, carry):
        mn, mx = carry
        r0, r1, c0, _ = chunk_rows(i)
        ka = ki2_ref[pl.ds(r0, LANES), :]
        kb = ki2_ref[pl.ds(r1, LANES), :]
        kcat = jnp.concatenate([ka[:, :LANES], kb[:, :LANES], ka[:, LANES:], kb[:, LANES:]], axis=0)
        s2 = _nt(kcat, qs)
        acc = jnp.zeros((ck, qb), F32)
        for s in range(n_slab):
            for par in range(2):
                blk = s2[par * ck:(par + 1) * ck, s * qb:(s + 1) * qb]
                acc = acc + jnp.maximum(blk, 0.0) * wit_ref[2 * s + par:2 * s + par + 1, :]
        causal = (kpos + c0 * LANES) <= qpos
        score_ref[pl.ds(r0, ck), :] = jnp.where(causal, acc, -jnp.inf)
        mn = jnp.minimum(mn, jnp.min(jnp.where(causal, acc, jnp.inf), axis=0, keepdims=True))
        mx = jnp.maximum(mx, jnp.max(jnp.where(causal, acc, -jnp.inf), axis=0, keepdims=True))
        return mn, mx

    mn, mx = lax.fori_loop(0, npair, score_pair,
                           (jnp.full((1, qb), jnp.inf, F32), jnp.full((1, qb), -jnp.inf, F32)))

    n_part = 8

    def reduce_keys(fn, init, combine):
        def body(i, acc):
            r0, _, c0, _ = chunk_rows(i)
            blk = score_ref[pl.ds(r0, ck), :]
            kidx = (kpos + c0 * LANES).astype(F32)
            return combine(acc, fn(blk, kidx).reshape(ck // (8 * n_part), n_part * 8, qb))
        return lax.fori_loop(0, npair, body, jnp.full((n_part * 8, qb), init, F32))

    def count(pred):
        acc = reduce_keys(lambda blk, kidx: jnp.where(pred(blk, kidx), 1.0, 0.0), 0.0,
                          lambda a, x: a + jnp.sum(x, axis=0))
        return jnp.sum(acc, axis=0, keepdims=True)

    def min_where(pred):
        acc = reduce_keys(lambda blk, kidx: jnp.where(pred(blk, kidx), blk, jnp.inf), jnp.inf,
                          lambda a, x: jnp.minimum(a, jnp.min(x, axis=0)))
        return jnp.min(acc, axis=0, keepdims=True)

    def bisect(_, lohi):
        lo, hi = lohi
        mid = 0.5 * lo + 0.5 * hi
        ok = count(lambda blk, kidx: blk >= mid) >= kf
        return jnp.where(ok, mid, lo), jnp.where(ok, hi, mid)

    lo, _ = lax.fori_loop(0, N_BISECT, bisect, (mn, mx))
    c_lo = count(lambda blk, kidx: blk >= lo)
    inexact = jnp.max(jnp.where(c_lo > kf, 1.0, 0.0)) > 0.0

    def exact_threshold(lo):
        def step(state):
            lo, _ = state
            m = min_where(lambda blk, kidx: blk >= lo)
            adv = count(lambda blk, kidx: blk > m) >= kf
            nxt = min_where(lambda blk, kidx: blk > m)
            return jnp.where(adv, nxt, lo), jnp.max(jnp.where(adv, 1.0, 0.0))

        lo, _ = lax.while_loop(lambda st: st[1] > 0.0, step, (lo, jnp.float32(1.0)))
        m = min_where(lambda blk, kidx: blk >= lo)
        r = kf - count(lambda blk, kidx: blk > m)
        tied = count(lambda blk, kidx: blk == m) > r

        def idx_bisect(_, ab):
            a, b = ab
            mid = jnp.floor(0.5 * (a + b))
            ok = count(lambda blk, kidx: (blk == m) & (kidx <= mid)) >= r
            return jnp.where(ok, a, mid), jnp.where(ok, mid, b)

        _, b = lax.fori_loop(0, int(math.ceil(math.log2(t_all))) + 1, idx_bisect,
                             (jnp.full((1, qb), -1.0, F32), jnp.full((1, qb), float(t_all - 1), F32)))
        return m, jnp.where(tied, b, float(t_all))

    thr, cidx = lax.cond(inexact, exact_threshold,
                         lambda lo: (lo, jnp.full((1, qb), float(t_all), F32)), lo)

    scale2 = A_HEAD_DIM ** -0.5 * math.log2(math.e)
    qg = [jnp.concatenate([q_ref[:, (n * group + g) * LANES:(n * group + g + 1) * LANES] for g in range(group)], axis=0)
          for n in range(A_KV_HEADS)]
    acc_ref[...] = jnp.zeros_like(acc_ref)

    def attend(i, ms):
        r0, r1, c0, c1 = chunk_rows(i)
        blk = score_ref[pl.ds(r0, ck), :]
        kidx = (kpos + c0 * LANES).astype(F32)
        sel = (blk > thr) | ((blk == thr) & (kidx <= cidx))
        bias = jnp.where(sel, 0.0, -jnp.inf)
        bias = jnp.concatenate([bias] * group, axis=1)
        sts = []
        for n in range(A_KV_HEADS):
            sl = slice(n * LANES, (n + 1) * LANES)
            kc = jnp.concatenate([k_ref[pl.ds(r0, LANES), sl], k_ref[pl.ds(r1, LANES), sl]], axis=0)
            sts.append(_nt(kc, qg[n]))
        new_ms, ps, alphas = [], [], []
        for n in range(A_KV_HEADS):
            st = sts[n] * scale2 + bias
            m_new = jnp.maximum(ms[n], jnp.max(st, axis=0, keepdims=True))
            ps.append(jnp.exp2(st - m_new).astype(BF16))
            alphas.append(jnp.exp2(ms[n] - m_new))
            new_ms.append(m_new)
        for n in range(A_KV_HEADS):
            vs = slice(n * VT_ROWS, (n + 1) * VT_ROWS)
            vtc = jnp.concatenate([vt_ref[c0, vs, :], vt_ref[c1, vs, :]], axis=1)
            acc_ref[n] = alphas[n] * acc_ref[n] + jnp.dot(vtc, ps[n], preferred_element_type=F32)
        return tuple(new_ms)

    lax.fori_loop(0, npair, attend, tuple(jnp.full((1, group * qb), NEG_INIT, F32) for _ in range(A_KV_HEADS)))

    for n in range(A_KV_HEADS):
        a = acc_ref[n]
        ot = a[:A_HEAD_DIM] / a[A_HEAD_DIM:A_HEAD_DIM + 1]
        for g in range(group):
            hd = n * group + g
            o_ref[:, hd * LANES:(hd + 1) * LANES] = ot[:, g * qb:(g + 1) * qb].T.astype(BF16)


def _dsa(qi, wit, q, ki2, k, vt, bsz, t_len, topk):
    qb = LANES
    nq = t_len // qb
    m = q.shape[0]
    kvd = k.shape[1]
    group = A_HEADS // A_KV_HEADS
    qrow = lambda w: pl.BlockSpec((qb, w), lambda b, j: (b * nq + j, 0))
    return pl.pallas_call(
        functools.partial(_dsa_kernel, topk, nq),
        grid=(bsz, nq),
        in_specs=[qrow(qi.shape[1]),
                  pl.BlockSpec((IDX_HEADS, qb), lambda b, j: (0, b * nq + j)),
                  qrow(q.shape[1]),
                  pl.BlockSpec((t_len, 2 * LANES), lambda b, j: (b, 0)),
                  pl.BlockSpec((t_len, kvd), lambda b, j: (b, 0)),
                  pl.BlockSpec((nq, vt.shape[1], LANES), lambda b, j: (b, 0, 0))],
        out_specs=qrow(q.shape[1]),
        out_shape=jax.ShapeDtypeStruct((m, q.shape[1]), BF16),
        scratch_shapes=[pltpu.VMEM((2 * ((nq + 1) // 2) * LANES, qb), F32),
                        pltpu.VMEM((A_KV_HEADS, VT_ROWS, group * qb), F32)],
        compiler_params=_cparams(("parallel", "arbitrary")),
        name="dsa",
    )(qi, wit, q, ki2, k, vt)


def _rms64(x, g, pavg):
    x2 = x * x
    hi = x2.astype(BF16)
    lo = (x2 - hi.astype(F32)).astype(BF16)
    ms = jnp.dot(hi, pavg, preferred_element_type=F32) + jnp.dot(lo, pavg, preferred_element_type=F32)
    return x * lax.rsqrt(ms + EPS) * g


def _split_halves(x, par):
    lane = lax.broadcasted_iota(jnp.int32, x.shape, 1)
    keep = jnp.where(lane < B_HEAD_DIM if par == 0 else lane >= B_HEAD_DIM, x, 0.0)
    moved = pltpu.roll(keep, B_HEAD_DIM, 1)
    return (keep, moved) if par == 0 else (moved, keep)


def _kvproj_kernel(h_ref, g_ref, w_ref, kn_ref, pavg_ref, c64_ref, a64_ref, b64_ref, k2_ref, v2_ref):
    hn = _rms(h_ref[...], g_ref[...]).astype(BF16)
    kv = jnp.dot(hn, w_ref[...], preferred_element_type=F32)
    half = B_HEAD_DIM // ROPE_FRACTION // 2
    kvd = B_KV_HEADS * B_HEAD_DIM
    for s in range(kvd // LANES):
        ks = kv[:, s * LANES:(s + 1) * LANES]
        ks = _rope(_rms64(ks, kn_ref[...], pavg_ref[...]), c64_ref[...], a64_ref[...], b64_ref[...], half)
        vs = kv[:, kvd + s * LANES:kvd + (s + 1) * LANES]
        for par in range(2):
            n = 2 * s + par
            for src, dst in ((ks, k2_ref), (vs, v2_ref)):
                low, high = _split_halves(src, par)
                dst[:, (2 * n) * LANES:(2 * n + 1) * LANES] = low.astype(BF16)
                dst[:, (2 * n + 1) * LANES:(2 * n + 2) * LANES] = high.astype(BF16)


def _kvproj(h, g, w, kn, pavg, t64, tm=512):
    m, d = h.shape
    assert m % tm == 0, (m, tm)
    row = lambda w_: pl.BlockSpec((tm, w_), lambda i: (i, 0))
    vec = lambda w_: pl.BlockSpec((1, w_), lambda i: (0, 0))
    tab = pl.BlockSpec((tm, LANES), lambda i: (i, 0))
    wide = B_KV_HEADS * 2 * LANES
    return pl.pallas_call(
        _kvproj_kernel,
        grid=(m // tm,),
        in_specs=[row(d), vec(d), _resident(w.shape), vec(LANES), _resident(pavg.shape), tab, tab, tab],
        out_specs=[row(wide), row(wide)],
        out_shape=[jax.ShapeDtypeStruct((m, wide), BF16)] * 2,
        compiler_params=_cparams(("parallel",)),
        name="kvproj",
    )(h, g, w, kn, pavg, *t64)


def _bqproj_kernel(h_ref, g_ref, w_ref, qn_ref, pavg_ref, c64_ref, a64_ref, b64_ref, q_ref):
    hn = _rms(h_ref[...], g_ref[...]).astype(BF16)
    qf = jnp.dot(hn, w_ref[...], preferred_element_type=F32)
    half = B_HEAD_DIM // ROPE_FRACTION // 2
    scale = B_HEAD_DIM ** -0.5
    for s in range(B_HEADS * B_HEAD_DIM // LANES):
        sl = slice(s * LANES, (s + 1) * LANES)
        y = _rope(_rms64(qf[:, sl], qn_ref[...], pavg_ref[...]), c64_ref[...], a64_ref[...], b64_ref[...], half)
        q_ref[:, sl] = (y * scale).astype(BF16)


def _bqproj(h, g, w, qn, pavg, t64, tm=256):
    m, d = h.shape
    assert m % tm == 0, (m, tm)
    row = lambda w_: pl.BlockSpec((tm, w_), lambda i: (i, 0))
    vec = lambda w_: pl.BlockSpec((1, w_), lambda i: (0, 0))
    tab = pl.BlockSpec((tm, LANES), lambda i: (i, 0))
    return pl.pallas_call(
        _bqproj_kernel,
        grid=(m // tm,),
        in_specs=[row(d), vec(d), _resident(w.shape), vec(LANES), _resident(pavg.shape), tab, tab, tab],
        out_specs=row(w.shape[1]),
        out_shape=jax.ShapeDtypeStruct((m, w.shape[1]), BF16),
        compiler_params=_cparams(("parallel",)),
        name="bqproj",
    )(h, g, w, qn, pavg, *t64)


def _swa_kernel(sink_ref, q_ref, kp_ref, kc_ref, vp_ref, vc_ref, o_ref):
    qb = q_ref.shape[0]
    j = pl.program_id(1)
    per_kv = B_HEADS // B_KV_HEADS
    n_tile = per_kv // 2
    r = lax.broadcasted_iota(jnp.int32, (qb, 2 * qb), 0)
    col = lax.broadcasted_iota(jnp.int32, (qb, 2 * qb), 1)
    rel = qb + r - col
    first_col = jnp.where(j > 0, 0, qb)
    mask = (rel >= 0) & (rel < WINDOW) & (col >= first_col)
    for n in range(B_KV_HEADS):
        lo = slice(2 * n * LANES, (2 * n + 1) * LANES)
        hi = slice((2 * n + 1) * LANES, (2 * n + 2) * LANES)
        kcat = jnp.concatenate([kp_ref[:, lo], kc_ref[:, lo], kp_ref[:, hi], kc_ref[:, hi]], axis=0)
        vcat = jnp.concatenate([vp_ref[:, lo], vc_ref[:, lo], vp_ref[:, hi], vc_ref[:, hi]], axis=0)
        qst = jnp.concatenate([q_ref[:, (n * n_tile + s) * LANES:(n * n_tile + s + 1) * LANES] for s in range(n_tile)], axis=0)
        logits = _nt(qst, kcat)
        rows = []
        for s in range(n_tile):
            halves = []
            for par in range(2):
                sink = sink_ref[n * per_kv + 2 * s + par]
                lg = jnp.where(mask, logits[s * qb:(s + 1) * qb, par * 2 * qb:(par + 1) * 2 * qb], -jnp.inf)
                m = jnp.maximum(jnp.max(lg, axis=-1, keepdims=True), sink)
                e = jnp.exp(lg - m)
                p = e / (jnp.sum(e, axis=-1, keepdims=True) + jnp.exp(sink - m))
                halves.append(p.astype(BF16))
            rows.append(jnp.concatenate(halves, axis=1))
        p2 = jnp.concatenate(rows, axis=0)
        o = jnp.dot(p2, vcat, preferred_element_type=F32)
        for s in range(n_tile):
            t = n * n_tile + s
            o_ref[:, t * LANES:(t + 1) * LANES] = o[s * qb:(s + 1) * qb].astype(BF16)


def _swa(sinks, q, k2, v2, bsz, t_len):
    qb = LANES
    nq = t_len // qb
    m, width = q.shape
    wide = k2.shape[1]
    cur = lambda w: pl.BlockSpec((qb, w), lambda b, j: (b * nq + j, 0))
    prev = lambda w: pl.BlockSpec((qb, w), lambda b, j: (b * nq + jnp.maximum(j - 1, 0), 0))
    return pl.pallas_call(
        _swa_kernel,
        grid=(bsz, nq),
        in_specs=[pl.BlockSpec(memory_space=pltpu.SMEM), cur(width), prev(wide), cur(wide), prev(wide), cur(wide)],
        out_specs=cur(width),
        out_shape=jax.ShapeDtypeStruct((m, width), BF16),
        compiler_params=_cparams(("parallel", "parallel")),
        name="swa",
    )(sinks, q, k2, k2, v2, v2)


def _rope_tables(t_len, head_dim, reps):
    half = head_dim // ROPE_FRACTION // 2
    inv = ROPE_THETA ** (-jnp.arange(half, dtype=F32) / half)
    ang = jnp.arange(t_len, dtype=jnp.int32).astype(F32)[:, None] * inv[None, :]
    cos, sin = jnp.cos(ang), jnp.sin(ang)
    lp = jnp.arange(LANES) % head_dim
    idx = lp % half
    c = jnp.where(lp < 2 * half, cos[:, idx], 1.0)
    a = jnp.where(lp < half, -sin[:, idx], 0.0)
    b = jnp.where((lp >= half) & (lp < 2 * half), sin[:, idx], 0.0)
    return tuple(jnp.tile(t, (reps, 1)) for t in (c, a, b))


def _pad_cols(w, mult):
    pad = (-w.shape[-1]) % mult
    return jnp.pad(w, ((0, 0), (0, pad))) if pad else w


def kernel(x, meta_tokens, ffn1_norm, ffn1_w_gate, ffn1_w_up, ffn1_w_down, ffn2_norm, ffn2_w_gate, ffn2_w_up, ffn2_w_down, a_norm, a_w_in, a_q_norm, a_k_norm, a_idx_k_norm, a_w_out, kv_norm, kv_w, kv_k_norm, b_norm, b_w_q, b_q_norm, b_sinks, b_w_out):
    bsz, s_len, d = x.shape
    depth = ffn1_norm.shape[0]
    n_a = a_norm.shape[0]
    topk = min(TOPK_MAX, s_len // 4)
    t_real = s_len + N_META
    t_len = -(-t_real // LANES) * LANES
    meta = jnp.broadcast_to(meta_tokens[None].astype(x.dtype), (bsz, N_META, d))
    h = jnp.concatenate([meta, x, jnp.zeros((bsz, t_len - t_real, d), x.dtype)], axis=1).reshape(bsz * t_len, d)

    t128 = _rope_tables(t_len, A_HEAD_DIM, bsz)
    t64 = _rope_tables(t_len, B_HEAD_DIM, bsz)
    seg = jnp.arange(LANES) // B_HEAD_DIM
    pavg = jnp.where(seg[:, None] == seg[None, :], 1.0 / B_HEAD_DIM, 0.0).astype(BF16)
    tile2 = lambda g: jnp.tile(g.astype(F32), 2)[None, :]

    w1 = tuple(w.astype(BF16) for w in (ffn1_w_gate, ffn1_w_up, ffn1_w_down))
    w2 = tuple(w.astype(BF16) for w in (ffn2_w_gate, ffn2_w_up, ffn2_w_down))

    q_dim, kv_dim, qi_dim = A_HEADS * A_HEAD_DIM, A_KV_HEADS * A_HEAD_DIM, IDX_HEADS * IDX_DIM
    k2 = v2 = None
    for layer in range(depth):
        if layer == n_a:
            k2, v2 = _kvproj(h, kv_norm[None, :], kv_w.astype(BF16), tile2(kv_k_norm), pavg, t64)
        h = _ffn(h, ffn1_norm[layer][None, :], *w1, layer)
        if layer < n_a:
            i = layer
            w = a_w_in[i].astype(BF16)
            o0, o1, o2, o3 = q_dim, q_dim + kv_dim, q_dim + 2 * kv_dim, q_dim + 2 * kv_dim + qi_dim
            wkw = _pad_cols(w[:, o3:], LANES)
            ikn = _pad_cols(a_idx_k_norm[i][None, :], LANES)
            q, k, vt, qi, ki2, wit = _aproj(
                h, a_norm[i][None, :], w[:, :o0], w[:, o0:o1], w[:, o1:o2].T, w[:, o2:o3], wkw,
                a_q_norm[i][None, :], a_k_norm[i][None, :], ikn, t128, t64)
            o = _dsa(qi, wit, q, ki2, k, vt, bsz, t_len, topk)
            h = _oproj(h, o, a_w_out[i].astype(BF16))
        else:
            i = layer - n_a
            q = _bqproj(h, b_norm[i][None, :], b_w_q[i].astype(BF16), tile2(b_q_norm[i]), pavg, t64)
            o = _swa(b_sinks[i].astype(F32), q, k2, v2, bsz, t_len)
            h = _oproj(h, o, b_w_out[i].astype(BF16))
        h = _ffn(h, ffn2_norm[layer][None, :], *w2, layer)
    return h.reshape(bsz, t_len, d)[:, N_META:N_META + s_len]
```

```python
import functools
import math

import jax
import jax.numpy as jnp
from jax import lax
from jax.experimental import pallas as pl
from jax.experimental.pallas import tpu as pltpu

F32 = jnp.float32
BF16 = jnp.bfloat16

LANES = 128
N_META = 16
META_ROW0 = LANES - N_META
TAIL_ROWS = 512
EPS = 1e-6
ROPE_THETA = 500000.0
ROPE_FRACTION = 4
A_HEADS, A_KV_HEADS, A_HEAD_DIM = 16, 4, 128
IDX_HEADS, IDX_DIM = 16, 64
TOPK_MAX = 256
B_HEADS, B_KV_HEADS, B_HEAD_DIM = 32, 4, 64
WINDOW = 128
VMEM_LIMIT = 56 * 1024 * 1024
NEG_INIT = -1e30
N_BISECT = 26
VT_ROWS = A_HEAD_DIM + 16


def _cparams(sem):
    return pltpu.CompilerParams(dimension_semantics=sem, vmem_limit_bytes=VMEM_LIMIT)


def _resident(shape):
    nd = len(shape)
    return pl.BlockSpec(shape, lambda *_: (0,) * nd, pipeline_mode=pl.Buffered(1))


def _rms(x, g):
    ms = jnp.mean(x * x, axis=-1, keepdims=True)
    return x * lax.rsqrt(ms + EPS) * g


def _rope(y, c, a, b, half):
    return y * c + pltpu.roll(y, LANES - half, 1) * a + pltpu.roll(y, half, 1) * b


def _nt(a, b):
    return lax.dot_general(a, b, (((1,), (1,)), ((), ())), preferred_element_type=F32)


def _ffn_kernel(tail, n_main, h_ref, t_ref, g_ref, wg_ref, wu_ref, wd_ref, o_ref, n_ref):
    i = pl.program_id(0)
    f = pl.program_id(1)
    last = pl.num_programs(1) - 1
    tf = wg_ref.shape[-1]

    def hidden(n, width):
        gate = jnp.dot(n, wg_ref[:, :width], preferred_element_type=F32)
        up = jnp.dot(n, wu_ref[:, :width], preferred_element_type=F32)
        act = (0.5 * gate) * jax.nn.sigmoid(gate) * up
        return jnp.dot(act.astype(BF16), wd_ref[:width, :], preferred_element_type=F32)

    def first(src_ref):
        x = src_ref[...]
        n = _rms(x, g_ref[...]).astype(BF16)
        n_ref[...] = n
        o_ref[...] = x + hidden(n, tf)

    pl.when((f == 0) & (i < n_main))(lambda: first(h_ref))
    if t_ref is not None:
        pl.when((f == 0) & (i >= n_main))(lambda: first(t_ref))

    def step(width):
        o_ref[...] += hidden(n_ref[...], width)

    if tail == tf:
        pl.when(f > 0)(lambda: step(tf))
    else:
        pl.when((f > 0) & (f < last))(lambda: step(tf))
        pl.when(f == last)(lambda: step(tail))


def _ffn_kernel_1src(tail, n_main, h_ref, *rest):
    _ffn_kernel(tail, n_main, h_ref, None, *rest)


def _ffn(h, g, wg, wu, wd, layer, rows=None, tail_rows=None, tm=512, tf=512):
    m, d = h.shape
    m = m if rows is None else rows
    assert m % tm == 0, (m, tm)
    n_main = m // tm
    n_tile = n_main
    ff = wg.shape[-1]
    nf = pl.cdiv(ff, tf)
    tail = ff - (nf - 1) * tf
    assert tail % LANES == 0 and nf >= 2, ff
    in_specs = [pl.BlockSpec((tm, d), lambda i, f: (jnp.minimum(i, n_main - 1), 0))]
    args = [h]
    kern = _ffn_kernel_1src
    if tail_rows is not None:
        assert tail_rows.shape == (tm, d)
        n_tile += 1
        in_specs.append(pl.BlockSpec((tm, d), lambda i, f: (0, 0)))
        args.append(tail_rows)
        kern = _ffn_kernel
    return pl.pallas_call(
        functools.partial(kern, tail, n_main),
        grid=(n_tile, nf),
        in_specs=in_specs + [
            pl.BlockSpec((1, d), lambda i, f: (0, 0)),
            pl.BlockSpec((None, d, tf), lambda i, f: (layer, 0, f)),
            pl.BlockSpec((None, d, tf), lambda i, f: (layer, 0, f)),
            pl.BlockSpec((None, tf, d), lambda i, f: (layer, f, 0)),
        ],
        out_specs=pl.BlockSpec((tm, d), lambda i, f: (i, 0)),
        out_shape=jax.ShapeDtypeStruct((n_tile * tm, d), F32),
        scratch_shapes=[pltpu.VMEM((tm, d), BF16)],
        compiler_params=_cparams(("parallel", "arbitrary")),
        name="ffn",
    )(*args, g, wg, wu, wd)


def _oproj_kernel(h_ref, o_ref, w_ref, out_ref):
    out_ref[...] = h_ref[...] + jnp.dot(o_ref[...], w_ref[...], preferred_element_type=F32)


def _oproj(h, o, w, tm=512):
    m, d = h.shape
    assert m % tm == 0, (m, tm)
    return pl.pallas_call(
        _oproj_kernel,
        grid=(m // tm,),
        in_specs=[
            pl.BlockSpec((tm, d), lambda i: (i, 0)),
            pl.BlockSpec((tm, o.shape[1]), lambda i: (i, 0)),
            _resident(w.shape),
        ],
        out_specs=pl.BlockSpec((tm, d), lambda i: (i, 0)),
        out_shape=jax.ShapeDtypeStruct((m, d), F32),
        compiler_params=_cparams(("parallel",)),
        name="oproj",
    )(h, o, w)


def _aproj_kernel(h_ref, g_ref, wq_ref, wk_ref, wvt_ref, wqi_ref, wkw_ref, qn_ref, kn_ref, ikn_ref,
                  c128_ref, a128_ref, b128_ref, c64_ref, a64_ref, b64_ref,
                  q_ref, k_ref, vt_ref, qi_ref, ki2_ref, wit_ref):
    tm = h_ref.shape[0]
    hn = _rms(h_ref[...], g_ref[...]).astype(BF16)
    c128, a128, b128 = c128_ref[...], a128_ref[...], b128_ref[...]
    c64, a64, b64 = c64_ref[...], a64_ref[...], b64_ref[...]
    half128 = A_HEAD_DIM // ROPE_FRACTION // 2
    half64 = IDX_DIM // ROPE_FRACTION // 2

    qf = jnp.dot(hn, wq_ref[...], preferred_element_type=F32)
    for hd in range(A_HEADS):
        sl = slice(hd * LANES, (hd + 1) * LANES)
        q_ref[:, sl] = _rope(_rms(qf[:, sl], qn_ref[...]), c128, a128, b128, half128).astype(BF16)

    kf = jnp.dot(hn, wk_ref[...], preferred_element_type=F32)
    for hd in range(A_KV_HEADS):
        sl = slice(hd * LANES, (hd + 1) * LANES)
        k_ref[:, sl] = _rope(_rms(kf[:, sl], kn_ref[...]), c128, a128, b128, half128).astype(BF16)

    vt = _nt(wvt_ref[...], hn)
    tail = jnp.where(lax.broadcasted_iota(jnp.int32, (VT_ROWS - A_HEAD_DIM, LANES), 0) == 0, 1.0, 0.0).astype(BF16)
    for c in range(tm // LANES):
        for n in range(A_KV_HEADS):
            vt_ref[c, n * VT_ROWS:n * VT_ROWS + A_HEAD_DIM, :] = (
                vt[n * A_HEAD_DIM:(n + 1) * A_HEAD_DIM, c * LANES:(c + 1) * LANES].astype(BF16))
            vt_ref[c, n * VT_ROWS + A_HEAD_DIM:(n + 1) * VT_ROWS, :] = tail

    qif = jnp.dot(hn, wqi_ref[...], preferred_element_type=F32)
    for s in range(IDX_HEADS * IDX_DIM // LANES):
        sl = slice(s * LANES, (s + 1) * LANES)
        qi_ref[:, sl] = _rope(qif[:, sl], c64, a64, b64, half64).astype(BF16)

    kw = jnp.dot(hn, wkw_ref[...], preferred_element_type=F32)
    lane = lax.broadcasted_iota(jnp.int32, kw.shape, 1)
    ms = jnp.sum(jnp.where(lane < IDX_DIM, kw * kw, 0.0), axis=-1, keepdims=True) * (1.0 / IDX_DIM)
    ki = _rope(kw * lax.rsqrt(ms + EPS) * ikn_ref[...], c64, a64, b64, half64)
    ki2_ref[:, :LANES] = ki.astype(BF16)
    ki2_ref[:, LANES:] = pltpu.roll(ki, IDX_DIM, 1).astype(BF16)
    wit_ref[...] = kw.T[IDX_DIM:IDX_DIM + IDX_HEADS, :] * ((IDX_HEADS * IDX_DIM) ** -0.5)


def _aproj(h, g, wq, wk, wvt, wqi, wkw, qn, kn, ikn, t128, t64, tm=256):
    m, d = h.shape
    assert m % tm == 0, (m, tm)
    row = lambda w: pl.BlockSpec((tm, w), lambda i: (i, 0))
    tab = pl.BlockSpec((tm, LANES), lambda i: (i, 0))
    vec = lambda w: pl.BlockSpec((1, w), lambda i: (0, 0))
    kvd = wk.shape[1]
    return pl.pallas_call(
        _aproj_kernel,
        grid=(m // tm,),
        in_specs=[row(d), vec(d), _resident(wq.shape), _resident(wk.shape), _resident(wvt.shape),
                  _resident(wqi.shape), _resident(wkw.shape), vec(LANES), vec(LANES), vec(LANES),
                  tab, tab, tab, tab, tab, tab],
        out_specs=[row(wq.shape[1]), row(kvd),
                   pl.BlockSpec((tm // LANES, A_KV_HEADS * VT_ROWS, LANES), lambda i: (i, 0, 0)),
                   row(wqi.shape[1]), row(2 * LANES),
                   pl.BlockSpec((IDX_HEADS, tm), lambda i: (0, i))],
        out_shape=[jax.ShapeDtypeStruct((m, wq.shape[1]), BF16),
                   jax.ShapeDtypeStruct((m, kvd), BF16),
                   jax.ShapeDtypeStruct((m // LANES, A_KV_HEADS * VT_ROWS, LANES), BF16),
                   jax.ShapeDtypeStruct((m, wqi.shape[1]), BF16),
                   jax.ShapeDtypeStruct((m, 2 * LANES), BF16),
                   jax.ShapeDtypeStruct((IDX_HEADS, m), F32)],
        compiler_params=_cparams(("parallel",)),
        name="aproj",
    )(h, g, wq, wk, wvt, wqi, wkw, qn, kn, ikn, *t128, *t64)


def _dsa_kernel(topk, n_xblk, nq, qi_ref, wit_ref, q_ref, kix_ref, kim_ref, kx_ref, km_ref, vtx_ref, vtm_ref,
                o_ref, score_ref, acc_ref):
    qb = q_ref.shape[0]
    ck = 2 * LANES
    s_len = kx_ref.shape[0]
    g = pl.program_id(0)
    is_tail = g >= n_xblk
    j = lax.rem(g, nq)
    npair = jnp.where(is_tail, 0, (j + 2) // 2)
    kf = jnp.float32(topk)
    group = A_HEADS // A_KV_HEADS
    n_slab = IDX_HEADS * IDX_DIM // LANES
    n_pos = s_len + N_META

    lane = lax.broadcasted_iota(jnp.int32, (1, qb), 1)
    tail_pos = jnp.where(g == n_xblk, jnp.maximum(lane - META_ROW0, 0), 0)
    qpos = jnp.where(is_tail, tail_pos, N_META + j * qb + lane)
    kpos_m = lax.broadcasted_iota(jnp.int32, (N_META, qb), 0)
    meta_rows = slice(META_ROW0, LANES)
    kpos_x = lax.broadcasted_iota(jnp.int32, (ck, qb), 0) + N_META

    def pair_row(i):
        return pl.multiple_of(i * ck, ck)

    qi = qi_ref[...]
    qs = jnp.concatenate([qi[:, s * LANES:(s + 1) * LANES] for s in range(n_slab)], axis=0)

    def scores(kc, kpos):
        n = kc.shape[0]
        kcat = jnp.concatenate([kc[:, :LANES], kc[:, LANES:]], axis=0)
        s2 = _nt(kcat, qs)
        acc = jnp.zeros((n, qb), F32)
        for s in range(n_slab):
            for par in range(2):
                blk = s2[par * n:(par + 1) * n, s * qb:(s + 1) * qb]
                acc = acc + jnp.maximum(blk, 0.0) * wit_ref[2 * s + par:2 * s + par + 1, :]
        vis = kpos <= qpos
        lo = jnp.min(jnp.where(vis, acc, jnp.inf), axis=0, keepdims=True)
        hi = jnp.max(jnp.where(vis, acc, -jnp.inf), axis=0, keepdims=True)
        return jnp.where(vis, acc, -jnp.inf), lo, hi

    sc, mn, mx = scores(kim_ref[meta_rows, :], kpos_m)
    score_ref[pl.ds(s_len, N_META), :] = sc

    def score_pair(i, carry):
        mn, mx = carry
        r0 = pair_row(i)
        sc, lo, hi = scores(kix_ref[pl.ds(r0, ck), :], kpos_x + i * ck)
        score_ref[pl.ds(r0, ck), :] = sc
        return jnp.minimum(mn, lo), jnp.maximum(mx, hi)

    mn, mx = lax.fori_loop(0, npair, score_pair, (mn, mx))

    n_part = 8

    def reduce_keys(fn, init, combine):
        def part(r0, n, kpos):
            return fn(score_ref[pl.ds(r0, n), :], kpos.astype(F32))
        acc = jnp.concatenate([part(s_len, N_META, kpos_m), jnp.full((n_part * 8 - N_META, qb), init, F32)], axis=0)
        return lax.fori_loop(
            0, npair,
            lambda i, a: combine(a, part(pair_row(i), ck, kpos_x + i * ck).reshape(ck // (8 * n_part), n_part * 8, qb)),
            acc)

    def count(pred):
        acc = reduce_keys(lambda blk, kidx: jnp.where(pred(blk, kidx), 1.0, 0.0), 0.0,
                          lambda a, x: a + jnp.sum(x, axis=0))
        return jnp.sum(acc, axis=0, keepdims=True)

    def min_where(pred):
        acc = reduce_keys(lambda blk, kidx: jnp.where(pred(blk, kidx), blk, jnp.inf), jnp.inf,
                          lambda a, x: jnp.minimum(a, jnp.min(x, axis=0)))
        return jnp.min(acc, axis=0, keepdims=True)

    def bisect(_, lohi):
        lo, hi = lohi
        mid = 0.5 * lo + 0.5 * hi
        ok = count(lambda blk, kidx: blk >= mid) >= kf
        return jnp.where(ok, mid, lo), jnp.where(ok, hi, mid)

    lo, _ = lax.fori_loop(0, N_BISECT, bisect, (mn, mx))
    c_lo = count(lambda blk, kidx: blk >= lo)
    inexact = jnp.max(jnp.where(c_lo > kf, 1.0, 0.0)) > 0.0

    def exact_threshold(lo):
        def step(state):
            lo, _ = state
            m = min_where(lambda blk, kidx: blk >= lo)
            adv = count(lambda blk, kidx: blk > m) >= kf
            nxt = min_where(lambda blk, kidx: blk > m)
            return jnp.where(adv, nxt, lo), jnp.max(jnp.where(adv, 1.0, 0.0))

        lo, _ = lax.while_loop(lambda st: st[1] > 0.0, step, (lo, jnp.float32(1.0)))
        m = min_where(lambda blk, kidx: blk >= lo)
        r = kf - count(lambda blk, kidx: blk > m)
        tied = count(lambda blk, kidx: blk == m) > r

        def idx_bisect(_, ab):
            a, b = ab
            mid = jnp.floor(0.5 * (a + b))
            ok = count(lambda blk, kidx: (blk == m) & (kidx <= mid)) >= r
            return jnp.where(ok, a, mid), jnp.where(ok, mid, b)

        _, b = lax.fori_loop(0, int(math.ceil(math.log2(n_pos))) + 1, idx_bisect,
                             (jnp.full((1, qb), -1.0, F32), jnp.full((1, qb), float(n_pos - 1), F32)))
        return m, jnp.where(tied, b, float(n_pos))

    thr, cidx = lax.cond(inexact, exact_threshold,
                         lambda lo: (lo, jnp.full((1, qb), float(n_pos), F32)), lo)

    scale2 = A_HEAD_DIM ** -0.5 * math.log2(math.e)
    qg = [jnp.concatenate([q_ref[:, (n * group + h) * LANES:(n * group + h + 1) * LANES] for h in range(group)], axis=0)
          for n in range(A_KV_HEADS)]
    acc_ref[...] = jnp.zeros_like(acc_ref)

    def attend(ms, blk, kpos, kcs, vtcs):
        sel = (blk > thr) | ((blk == thr) & (kpos.astype(F32) <= cidx))
        bias = jnp.where(sel, 0.0, -jnp.inf)
        bias = jnp.concatenate([bias] * group, axis=1)
        sts = [_nt(kcs[n], qg[n]) for n in range(A_KV_HEADS)]
        new_ms, ps, alphas = [], [], []
        for n in range(A_KV_HEADS):
            st = sts[n] * scale2 + bias
            m_new = jnp.maximum(ms[n], jnp.max(st, axis=0, keepdims=True))
            ps.append(jnp.exp2(st - m_new).astype(BF16))
            alphas.append(jnp.exp2(ms[n] - m_new))
            new_ms.append(m_new)
        for n in range(A_KV_HEADS):
            p = ps[n]
            if p.shape[0] < vtcs[n].shape[1]:
                p = jnp.concatenate([jnp.zeros((vtcs[n].shape[1] - p.shape[0], p.shape[1]), BF16), p], axis=0)
            acc_ref[n] = alphas[n] * acc_ref[n] + jnp.dot(vtcs[n], p, preferred_element_type=F32)
        return tuple(new_ms)

    head = lambda n: slice(n * LANES, (n + 1) * LANES)
    vrow = lambda n: slice(n * VT_ROWS, (n + 1) * VT_ROWS)
    ms = tuple(jnp.full((1, group * qb), NEG_INIT, F32) for _ in range(A_KV_HEADS))
    ms = attend(ms, score_ref[pl.ds(s_len, N_META), :], kpos_m,
                [km_ref[meta_rows, head(n)] for n in range(A_KV_HEADS)],
                [vtm_ref[0, vrow(n), :] for n in range(A_KV_HEADS)])

    def attend_pair(i, ms):
        r0 = pair_row(i)
        return attend(ms, score_ref[pl.ds(r0, ck), :], kpos_x + i * ck,
                      [kx_ref[pl.ds(r0, ck), head(n)] for n in range(A_KV_HEADS)],
                      [jnp.concatenate([vtx_ref[2 * i, vrow(n), :], vtx_ref[2 * i + 1, vrow(n), :]], axis=1)
                       for n in range(A_KV_HEADS)])

    lax.fori_loop(0, npair, attend_pair, ms)

    for n in range(A_KV_HEADS):
        a = acc_ref[n]
        ot = a[:A_HEAD_DIM] / a[A_HEAD_DIM:A_HEAD_DIM + 1]
        for h in range(group):
            hd = n * group + h
            o_ref[:, hd * LANES:(hd + 1) * LANES] = ot[:, h * qb:(h + 1) * qb].T.astype(BF16)


def _dsa(qi, wit, q, ki2, k, vt, bsz, s_len, topk):
    qb = LANES
    assert s_len % (2 * qb) == 0, s_len
    nq = s_len // qb
    n_xblk = bsz * nq
    m = q.shape[0]
    n_blk = m // qb
    kvd = k.shape[1]
    group = A_HEADS // A_KV_HEADS
    qrow = lambda w: pl.BlockSpec((qb, w), lambda g: (g, 0))
    batch = lambda g: jnp.minimum(g // nq, bsz - 1)
    return pl.pallas_call(
        functools.partial(_dsa_kernel, topk, n_xblk, nq),
        grid=(n_blk,),
        in_specs=[qrow(qi.shape[1]),
                  pl.BlockSpec((IDX_HEADS, qb), lambda g: (0, g)),
                  qrow(q.shape[1]),
                  pl.BlockSpec((s_len, 2 * LANES), lambda g: (batch(g), 0)),
                  pl.BlockSpec((qb, 2 * LANES), lambda g: (n_xblk, 0)),
                  pl.BlockSpec((s_len, kvd), lambda g: (batch(g), 0)),
                  pl.BlockSpec((qb, kvd), lambda g: (n_xblk, 0)),
                  pl.BlockSpec((nq, vt.shape[1], LANES), lambda g: (batch(g), 0, 0)),
                  pl.BlockSpec((1, vt.shape[1], LANES), lambda g: (n_xblk, 0, 0))],
        out_specs=qrow(q.shape[1]),
        out_shape=jax.ShapeDtypeStruct((m, q.shape[1]), BF16),
        scratch_shapes=[pltpu.VMEM((s_len + qb, qb), F32),
                        pltpu.VMEM((A_KV_HEADS, VT_ROWS, group * qb), F32)],
        compiler_params=_cparams(("arbitrary",)),
        name="dsa",
    )(qi, wit, q, ki2, ki2, k, k, vt, vt)


def _rms64(x, g, pavg):
    x2 = x * x
    hi = x2.astype(BF16)
    lo = (x2 - hi.astype(F32)).astype(BF16)
    ms = jnp.dot(hi, pavg, preferred_element_type=F32) + jnp.dot(lo, pavg, preferred_element_type=F32)
    return x * lax.rsqrt(ms + EPS) * g


def _split_halves(x, par):
    lane = lax.broadcasted_iota(jnp.int32, x.shape, 1)
    keep = jnp.where(lane < B_HEAD_DIM if par == 0 else lane >= B_HEAD_DIM, x, 0.0)
    moved = pltpu.roll(keep, B_HEAD_DIM, 1)
    return (keep, moved) if par == 0 else (moved, keep)


def _kvproj_kernel(h_ref, g_ref, w_ref, kn_ref, pavg_ref, c64_ref, a64_ref, b64_ref, k2_ref, v2_ref):
    hn = _rms(h_ref[...], g_ref[...]).astype(BF16)
    kv = jnp.dot(hn, w_ref[...], preferred_element_type=F32)
    half = B_HEAD_DIM // ROPE_FRACTION // 2
    kvd = B_KV_HEADS * B_HEAD_DIM
    for s in range(kvd // LANES):
        ks = kv[:, s * LANES:(s + 1) * LANES]
        ks = _rope(_rms64(ks, kn_ref[...], pavg_ref[...]), c64_ref[...], a64_ref[...], b64_ref[...], half)
        vs = kv[:, kvd + s * LANES:kvd + (s + 1) * LANES]
        for par in range(2):
            n = 2 * s + par
            for src, dst in ((ks, k2_ref), (vs, v2_ref)):
                low, high = _split_halves(src, par)
                dst[:, (2 * n) * LANES:(2 * n + 1) * LANES] = low.astype(BF16)
                dst[:, (2 * n + 1) * LANES:(2 * n + 2) * LANES] = high.astype(BF16)


def _kvproj(h, g, w, kn, pavg, t64, tm=512):
    m, d = h.shape
    assert m % tm == 0, (m, tm)
    row = lambda w_: pl.BlockSpec((tm, w_), lambda i: (i, 0))
    vec = lambda w_: pl.BlockSpec((1, w_), lambda i: (0, 0))
    tab = pl.BlockSpec((tm, LANES), lambda i: (i, 0))
    wide = B_KV_HEADS * 2 * LANES
    return pl.pallas_call(
        _kvproj_kernel,
        grid=(m // tm,),
        in_specs=[row(d), vec(d), _resident(w.shape), vec(LANES), _resident(pavg.shape), tab, tab, tab],
        out_specs=[row(wide), row(wide)],
        out_shape=[jax.ShapeDtypeStruct((m, wide), BF16)] * 2,
        compiler_params=_cparams(("parallel",)),
        name="kvproj",
    )(h, g, w, kn, pavg, *t64)


def _bqproj_kernel(h_ref, g_ref, w_ref, qn_ref, pavg_ref, c64_ref, a64_ref, b64_ref, q_ref):
    hn = _rms(h_ref[...], g_ref[...]).astype(BF16)
    qf = jnp.dot(hn, w_ref[...], preferred_element_type=F32)
    half = B_HEAD_DIM // ROPE_FRACTION // 2
    scale = B_HEAD_DIM ** -0.5
    for s in range(B_HEADS * B_HEAD_DIM // LANES):
        sl = slice(s * LANES, (s + 1) * LANES)
        y = _rope(_rms64(qf[:, sl], qn_ref[...], pavg_ref[...]), c64_ref[...], a64_ref[...], b64_ref[...], half)
        q_ref[:, sl] = (y * scale).astype(BF16)


def _bqproj(h, g, w, qn, pavg, t64, tm=256):
    m, d = h.shape
    assert m % tm == 0, (m, tm)
    row = lambda w_: pl.BlockSpec((tm, w_), lambda i: (i, 0))
    vec = lambda w_: pl.BlockSpec((1, w_), lambda i: (0, 0))
    tab = pl.BlockSpec((tm, LANES), lambda i: (i, 0))
    return pl.pallas_call(
        _bqproj_kernel,
        grid=(m // tm,),
        in_specs=[row(d), vec(d), _resident(w.shape), vec(LANES), _resident(pavg.shape), tab, tab, tab],
        out_specs=row(w.shape[1]),
        out_shape=jax.ShapeDtypeStruct((m, w.shape[1]), BF16),
        compiler_params=_cparams(("parallel",)),
        name="bqproj",
    )(h, g, w, qn, pavg, *t64)


def _swa_kernel(n_xblk, nq, sink_ref, q_ref, kp_ref, kc_ref, vp_ref, vc_ref, o_ref):
    qb = q_ref.shape[0]
    g = pl.program_id(0)
    per_kv = B_HEADS // B_KV_HEADS
    n_tile = per_kv // 2
    r = lax.broadcasted_iota(jnp.int32, (qb, 2 * qb), 0)
    col = lax.broadcasted_iota(jnp.int32, (qb, 2 * qb), 1)
    rel = qb + r - col
    first_x = jnp.where(lax.rem(g, nq) == 0, META_ROW0, 0)
    first_tail = jnp.where(g == n_xblk, qb + META_ROW0, qb)
    first_col = jnp.where(g >= n_xblk, first_tail, first_x)
    mask = (rel >= 0) & (rel < WINDOW) & (col >= first_col)
    for n in range(B_KV_HEADS):
        lo = slice(2 * n * LANES, (2 * n + 1) * LANES)
        hi = slice((2 * n + 1) * LANES, (2 * n + 2) * LANES)
        kcat = jnp.concatenate([kp_ref[:, lo], kc_ref[:, lo], kp_ref[:, hi], kc_ref[:, hi]], axis=0)
        vcat = jnp.concatenate([vp_ref[:, lo], vc_ref[:, lo], vp_ref[:, hi], vc_ref[:, hi]], axis=0)
        qst = jnp.concatenate([q_ref[:, (n * n_tile + s) * LANES:(n * n_tile + s + 1) * LANES] for s in range(n_tile)], axis=0)
        logits = _nt(qst, kcat)
        rows = []
        for s in range(n_tile):
            halves = []
            for par in range(2):
                sink = sink_ref[n * per_kv + 2 * s + par]
                lg = jnp.where(mask, logits[s * qb:(s + 1) * qb, par * 2 * qb:(par + 1) * 2 * qb], -jnp.inf)
                m = jnp.maximum(jnp.max(lg, axis=-1, keepdims=True), sink)
                e = jnp.exp(lg - m)
                p = e / (jnp.sum(e, axis=-1, keepdims=True) + jnp.exp(sink - m))
                halves.append(p.astype(BF16))
            rows.append(jnp.concatenate(halves, axis=1))
        p2 = jnp.concatenate(rows, axis=0)
        o = jnp.dot(p2, vcat, preferred_element_type=F32)
        for s in range(n_tile):
            t = n * n_tile + s
            o_ref[:, t * LANES:(t + 1) * LANES] = o[s * qb:(s + 1) * qb].astype(BF16)


def _swa(sinks, q, k2, v2, bsz, s_len):
    qb = LANES
    nq = s_len // qb
    n_xblk = bsz * nq
    m, width = q.shape
    wide = k2.shape[1]
    cur = lambda w: pl.BlockSpec((qb, w), lambda g: (g, 0))
    prev = lambda w: pl.BlockSpec(
        (qb, w), lambda g: (jnp.where((g >= n_xblk) | (lax.rem(g, nq) == 0), n_xblk, g - 1), 0))
    return pl.pallas_call(
        functools.partial(_swa_kernel, n_xblk, nq),
        grid=(m // qb,),
        in_specs=[pl.BlockSpec(memory_space=pltpu.SMEM), cur(width), prev(wide), cur(wide), prev(wide), cur(wide)],
        out_specs=cur(width),
        out_shape=jax.ShapeDtypeStruct((m, width), BF16),
        compiler_params=_cparams(("parallel",)),
        name="swa",
    )(sinks, q, k2, k2, v2, v2)


def _rope_tables(pos, head_dim):
    half = head_dim // ROPE_FRACTION // 2
    inv = ROPE_THETA ** (-jnp.arange(half, dtype=F32) / half)
    ang = pos.astype(F32)[:, None] * inv[None, :]
    cos, sin = jnp.cos(ang), jnp.sin(ang)
    lp = jnp.arange(LANES) % head_dim
    idx = lp % half
    c = jnp.where(lp < 2 * half, cos[:, idx], 1.0)
    a = jnp.where(lp < half, -sin[:, idx], 0.0)
    b = jnp.where((lp >= half) & (lp < 2 * half), sin[:, idx], 0.0)
    return c, a, b


def _pad_cols(w, mult):
    pad = (-w.shape[-1]) % mult
    return jnp.pad(w, ((0, 0), (0, pad))) if pad else w


def kernel(x, meta_tokens, ffn1_norm, ffn1_w_gate, ffn1_w_up, ffn1_w_down, ffn2_norm, ffn2_w_gate, ffn2_w_up, ffn2_w_down, a_norm, a_w_in, a_q_norm, a_k_norm, a_idx_k_norm, a_w_out, kv_norm, kv_w, kv_k_norm, b_norm, b_w_q, b_q_norm, b_sinks, b_w_out):
    bsz, s_len, d = x.shape
    depth = ffn1_norm.shape[0]
    n_a = a_norm.shape[0]
    topk = min(TOPK_MAX, s_len // 4)
    n_x = bsz * s_len
    assert n_x % TAIL_ROWS == 0 and meta_tokens.shape[0] == N_META

    zeros = lambda n: jnp.zeros((n, d), x.dtype)
    tail_rows = jnp.concatenate([zeros(META_ROW0), meta_tokens.astype(x.dtype), zeros(TAIL_ROWS - LANES)], axis=0)
    h = x.reshape(n_x, d)
    seq_pos = N_META + jnp.arange(s_len, dtype=jnp.int32)
    tail_pos = jnp.clip(jnp.arange(TAIL_ROWS, dtype=jnp.int32) - META_ROW0, 0, N_META - 1)
    tail_pos = jnp.where(jnp.arange(TAIL_ROWS) < LANES, tail_pos, 0)
    pos = jnp.concatenate([jnp.tile(seq_pos, bsz), tail_pos])
    t128 = _rope_tables(pos, A_HEAD_DIM)
    t64 = _rope_tables(pos, B_HEAD_DIM)
    seg = jnp.arange(LANES) // B_HEAD_DIM
    pavg = jnp.where(seg[:, None] == seg[None, :], 1.0 / B_HEAD_DIM, 0.0).astype(BF16)
    tile2 = lambda g: jnp.tile(g.astype(F32), 2)[None, :]

    w1 = tuple(w.astype(BF16) for w in (ffn1_w_gate, ffn1_w_up, ffn1_w_down))
    w2 = tuple(w.astype(BF16) for w in (ffn2_w_gate, ffn2_w_up, ffn2_w_down))

    q_dim, kv_dim, qi_dim = A_HEADS * A_HEAD_DIM, A_KV_HEADS * A_HEAD_DIM, IDX_HEADS * IDX_DIM
    k2 = v2 = None
    for layer in range(depth):
        if layer == n_a:
            k2, v2 = _kvproj(h, kv_norm[None, :], kv_w.astype(BF16), tile2(kv_k_norm), pavg, t64)
        h = _ffn(h, ffn1_norm[layer][None, :], *w1, layer, tail_rows=tail_rows if layer == 0 else None)
        if layer < n_a:
            i = layer
            w = a_w_in[i].astype(BF16)
            o0, o1, o2, o3 = q_dim, q_dim + kv_dim, q_dim + 2 * kv_dim, q_dim + 2 * kv_dim + qi_dim
            wkw = _pad_cols(w[:, o3:], LANES)
            ikn = _pad_cols(a_idx_k_norm[i][None, :], LANES)
            q, k, vt, qi, ki2, wit = _aproj(
                h, a_norm[i][None, :], w[:, :o0], w[:, o0:o1], w[:, o1:o2].T, w[:, o2:o3], wkw,
                a_q_norm[i][None, :], a_k_norm[i][None, :], ikn, t128, t64)
            o = _dsa(qi, wit, q, ki2, k, vt, bsz, s_len, topk)
            h = _oproj(h, o, a_w_out[i].astype(BF16))
        else:
            i = layer - n_a
            q = _bqproj(h, b_norm[i][None, :], b_w_q[i].astype(BF16), tile2(b_q_norm[i]), pavg, t64)
            o = _swa(b_sinks[i].astype(F32), q, k2, v2, bsz, s_len)
            h = _oproj(h, o, b_w_out[i].astype(BF16))
        h = _ffn(h, ffn2_norm[layer][None, :], *w2, layer, rows=n_x if layer == depth - 1 else None)
    return h.reshape(bsz, s_len, d)
```

```python
import functools
import math

import jax
import jax.numpy as jnp
from jax import lax
from jax.experimental import pallas as pl
from jax.experimental.pallas import tpu as pltpu

F32 = jnp.float32
BF16 = jnp.bfloat16

LANES = 128
N_META = 16
META_ROW0 = LANES - N_META
TAIL_ROWS = 512
EPS = 1e-6
ROPE_THETA = 500000.0
ROPE_FRACTION = 4
A_HEADS, A_KV_HEADS, A_HEAD_DIM = 16, 4, 128
IDX_HEADS, IDX_DIM = 16, 64
TOPK_MAX = 256
B_HEADS, B_KV_HEADS, B_HEAD_DIM = 32, 4, 64
WINDOW = 128
VMEM_LIMIT = 56 * 1024 * 1024
NEG_INIT = -1e30
N_BISECT = 26
KEY_CHUNKS = 4
VT_ROWS = A_HEAD_DIM + 16


def _cparams(sem):
    return pltpu.CompilerParams(dimension_semantics=sem, vmem_limit_bytes=VMEM_LIMIT)


def _resident(shape):
    nd = len(shape)
    return pl.BlockSpec(shape, lambda *_: (0,) * nd, pipeline_mode=pl.Buffered(1))


def _rms(x, g):
    ms = jnp.mean(x * x, axis=-1, keepdims=True)
    return x * lax.rsqrt(ms + EPS) * g


def _rope(y, c, a, b, half):
    return y * c + pltpu.roll(y, LANES - half, 1) * a + pltpu.roll(y, half, 1) * b


def _nt(a, b):
    return lax.dot_general(a, b, (((1,), (1,)), ((), ())), preferred_element_type=F32)


def _ffn_kernel(tail, n_main, h_ref, t_ref, g_ref, wg_ref, wu_ref, wd_ref, o_ref, n_ref):
    i = pl.program_id(0)
    f = pl.program_id(1)
    last = pl.num_programs(1) - 1
    tf = wg_ref.shape[-1]

    def hidden(n, width):
        gate = jnp.dot(n, wg_ref[:, :width], preferred_element_type=F32)
        up = jnp.dot(n, wu_ref[:, :width], preferred_element_type=F32)
        act = (0.5 * gate) * jax.nn.sigmoid(gate) * up
        return jnp.dot(act.astype(BF16), wd_ref[:width, :], preferred_element_type=F32)

    def first(src_ref):
        x = src_ref[...]
        n = _rms(x, g_ref[...]).astype(BF16)
        n_ref[...] = n
        o_ref[...] = x + hidden(n, tf)

    pl.when((f == 0) & (i < n_main))(lambda: first(h_ref))
    if t_ref is not None:
        pl.when((f == 0) & (i >= n_main))(lambda: first(t_ref))

    def step(width):
        o_ref[...] += hidden(n_ref[...], width)

    if tail == tf:
        pl.when(f > 0)(lambda: step(tf))
    else:
        pl.when((f > 0) & (f < last))(lambda: step(tf))
        pl.when(f == last)(lambda: step(tail))


def _ffn_kernel_1src(tail, n_main, h_ref, *rest):
    _ffn_kernel(tail, n_main, h_ref, None, *rest)


def _ffn(h, g, wg, wu, wd, layer, rows=None, tail_rows=None, tm=512, tf=1024):
    m, d = h.shape
    m = m if rows is None else rows
    assert m % tm == 0, (m, tm)
    n_main = m // tm
    n_tile = n_main
    ff = wg.shape[-1]
    nf = pl.cdiv(ff, tf)
    tail = ff - (nf - 1) * tf
    assert tail % LANES == 0 and nf >= 2, ff
    in_specs = [pl.BlockSpec((tm, d), lambda i, f: (jnp.minimum(i, n_main - 1), 0))]
    args = [h]
    kern = _ffn_kernel_1src
    if tail_rows is not None:
        assert tail_rows.shape == (tm, d)
        n_tile += 1
        in_specs.append(pl.BlockSpec((tm, d), lambda i, f: (0, 0)))
        args.append(tail_rows)
        kern = _ffn_kernel
    return pl.pallas_call(
        functools.partial(kern, tail, n_main),
        grid=(n_tile, nf),
        in_specs=in_specs + [
            pl.BlockSpec((1, d), lambda i, f: (0, 0)),
            pl.BlockSpec((None, d, tf), lambda i, f: (layer, 0, f)),
            pl.BlockSpec((None, d, tf), lambda i, f: (layer, 0, f)),
            pl.BlockSpec((None, tf, d), lambda i, f: (layer, f, 0)),
        ],
        out_specs=pl.BlockSpec((tm, d), lambda i, f: (i, 0)),
        out_shape=jax.ShapeDtypeStruct((n_tile * tm, d), F32),
        scratch_shapes=[pltpu.VMEM((tm, d), BF16)],
        compiler_params=_cparams(("parallel", "arbitrary")),
        name="ffn",
    )(*args, g, wg, wu, wd)


def _oproj_kernel(h_ref, o_ref, w_ref, out_ref):
    out_ref[...] = h_ref[...] + jnp.dot(o_ref[...], w_ref[...], preferred_element_type=F32)


def _oproj(h, o, w, tm=512):
    m, d = h.shape
    assert m % tm == 0, (m, tm)
    return pl.pallas_call(
        _oproj_kernel,
        grid=(m // tm,),
        in_specs=[
            pl.BlockSpec((tm, d), lambda i: (i, 0)),
            pl.BlockSpec((tm, o.shape[1]), lambda i: (i, 0)),
            _resident(w.shape),
        ],
        out_specs=pl.BlockSpec((tm, d), lambda i: (i, 0)),
        out_shape=jax.ShapeDtypeStruct((m, d), F32),
        compiler_params=_cparams(("parallel",)),
        name="oproj",
    )(h, o, w)


def _aproj_kernel(h_ref, g_ref, wq_ref, wk_ref, wvt_ref, wqi_ref, wkw_ref, qn_ref, kn_ref, ikn_ref,
                  c128_ref, a128_ref, b128_ref, c64_ref, a64_ref, b64_ref,
                  q_ref, k_ref, vt_ref, qi_ref, ki2_ref, wit_ref):
    tm = h_ref.shape[0]
    hn = _rms(h_ref[...], g_ref[...]).astype(BF16)
    c128, a128, b128 = c128_ref[...], a128_ref[...], b128_ref[...]
    c64, a64, b64 = c64_ref[...], a64_ref[...], b64_ref[...]
    half128 = A_HEAD_DIM // ROPE_FRACTION // 2
    half64 = IDX_DIM // ROPE_FRACTION // 2

    qf = jnp.dot(hn, wq_ref[...], preferred_element_type=F32)
    for hd in range(A_HEADS):
        sl = slice(hd * LANES, (hd + 1) * LANES)
        q_ref[:, sl] = _rope(_rms(qf[:, sl], qn_ref[...]), c128, a128, b128, half128).astype(BF16)

    kf = jnp.dot(hn, wk_ref[...], preferred_element_type=F32)
    for hd in range(A_KV_HEADS):
        sl = slice(hd * LANES, (hd + 1) * LANES)
        k_ref[:, sl] = _rope(_rms(kf[:, sl], kn_ref[...]), c128, a128, b128, half128).astype(BF16)

    vt = _nt(wvt_ref[...], hn)
    tail = jnp.where(lax.broadcasted_iota(jnp.int32, (VT_ROWS - A_HEAD_DIM, LANES), 0) == 0, 1.0, 0.0).astype(BF16)
    for c in range(tm // LANES):
        for n in range(A_KV_HEADS):
            vt_ref[c, n * VT_ROWS:n * VT_ROWS + A_HEAD_DIM, :] = (
                vt[n * A_HEAD_DIM:(n + 1) * A_HEAD_DIM, c * LANES:(c + 1) * LANES].astype(BF16))
            vt_ref[c, n * VT_ROWS + A_HEAD_DIM:(n + 1) * VT_ROWS, :] = tail

    qif = jnp.dot(hn, wqi_ref[...], preferred_element_type=F32)
    for s in range(IDX_HEADS * IDX_DIM // LANES):
        sl = slice(s * LANES, (s + 1) * LANES)
        qi_ref[:, sl] = _rope(qif[:, sl], c64, a64, b64, half64).astype(BF16)

    kw = jnp.dot(hn, wkw_ref[...], preferred_element_type=F32)
    lane = lax.broadcasted_iota(jnp.int32, kw.shape, 1)
    ms = jnp.sum(jnp.where(lane < IDX_DIM, kw * kw, 0.0), axis=-1, keepdims=True) * (1.0 / IDX_DIM)
    ki = _rope(kw * lax.rsqrt(ms + EPS) * ikn_ref[...], c64, a64, b64, half64)
    ki2_ref[:, :LANES] = ki.astype(BF16)
    ki2_ref[:, LANES:] = pltpu.roll(ki, IDX_DIM, 1).astype(BF16)
    wit_ref[...] = kw.T[IDX_DIM:IDX_DIM + IDX_HEADS, :] * ((IDX_HEADS * IDX_DIM) ** -0.5)


def _aproj(h, g, wq, wk, wvt, wqi, wkw, qn, kn, ikn, t128, t64, tm=256):
    m, d = h.shape
    assert m % tm == 0, (m, tm)
    row = lambda w: pl.BlockSpec((tm, w), lambda i: (i, 0))
    tab = pl.BlockSpec((tm, LANES), lambda i: (i, 0))
    vec = lambda w: pl.BlockSpec((1, w), lambda i: (0, 0))
    kvd = wk.shape[1]
    return pl.pallas_call(
        _aproj_kernel,
        grid=(m // tm,),
        in_specs=[row(d), vec(d), _resident(wq.shape), _resident(wk.shape), _resident(wvt.shape),
                  _resident(wqi.shape), _resident(wkw.shape), vec(LANES), vec(LANES), vec(LANES),
                  tab, tab, tab, tab, tab, tab],
        out_specs=[row(wq.shape[1]), row(kvd),
                   pl.BlockSpec((tm // LANES, A_KV_HEADS * VT_ROWS, LANES), lambda i: (i, 0, 0)),
                   row(wqi.shape[1]), row(2 * LANES),
                   pl.BlockSpec((IDX_HEADS, tm), lambda i: (0, i))],
        out_shape=[jax.ShapeDtypeStruct((m, wq.shape[1]), BF16),
                   jax.ShapeDtypeStruct((m, kvd), BF16),
                   jax.ShapeDtypeStruct((m // LANES, A_KV_HEADS * VT_ROWS, LANES), BF16),
                   jax.ShapeDtypeStruct((m, wqi.shape[1]), BF16),
                   jax.ShapeDtypeStruct((m, 2 * LANES), BF16),
                   jax.ShapeDtypeStruct((IDX_HEADS, m), F32)],
        compiler_params=_cparams(("parallel",)),
        name="aproj",
    )(h, g, wq, wk, wvt, wqi, wkw, qn, kn, ikn, *t128, *t64)


def _dsa_kernel(topk, n_xblk, nq, qi_ref, wit_ref, q_ref, kix_ref, kim_ref, kx_ref, km_ref, vtx_ref, vtm_ref,
                o_ref, score_ref, acc_ref):
    qb = q_ref.shape[0]
    ck = KEY_CHUNKS * LANES
    s_len = kx_ref.shape[0]
    g = pl.program_id(0)
    is_tail = g >= n_xblk
    j = lax.rem(g, nq)
    npair = jnp.where(is_tail, 0, (j + KEY_CHUNKS) // KEY_CHUNKS)
    kf = jnp.float32(topk)
    group = A_HEADS // A_KV_HEADS
    n_slab = IDX_HEADS * IDX_DIM // LANES
    n_pos = s_len + N_META

    lane = lax.broadcasted_iota(jnp.int32, (1, qb), 1)
    tail_pos = jnp.where(g == n_xblk, jnp.maximum(lane - META_ROW0, 0), 0)
    qpos = jnp.where(is_tail, tail_pos, N_META + j * qb + lane)
    kpos_m = lax.broadcasted_iota(jnp.int32, (N_META, qb), 0)
    meta_rows = slice(META_ROW0, LANES)
    kpos_x = lax.broadcasted_iota(jnp.int32, (ck, qb), 0) + N_META

    def pair_row(i):
        return pl.multiple_of(i * ck, ck)

    qi = qi_ref[...]
    qs = jnp.concatenate([qi[:, s * LANES:(s + 1) * LANES] for s in range(n_slab)], axis=0)

    def scores(kc, kpos):
        n = kc.shape[0]
        kcat = jnp.concatenate([kc[:, :LANES], kc[:, LANES:]], axis=0)
        s2 = _nt(kcat, qs)
        acc = jnp.zeros((n, qb), F32)
        for s in range(n_slab):
            for par in range(2):
                blk = s2[par * n:(par + 1) * n, s * qb:(s + 1) * qb]
                acc = acc + jnp.maximum(blk, 0.0) * wit_ref[2 * s + par:2 * s + par + 1, :]
        vis = kpos <= qpos
        lo = jnp.min(jnp.where(vis, acc, jnp.inf), axis=0, keepdims=True)
        hi = jnp.max(jnp.where(vis, acc, -jnp.inf), axis=0, keepdims=True)
        return jnp.where(vis, acc, -jnp.inf), lo, hi

    sc, mn, mx = scores(kim_ref[meta_rows, :], kpos_m)
    score_ref[pl.ds(s_len, N_META), :] = sc

    def score_pair(i, carry):
        mn, mx = carry
        r0 = pair_row(i)
        sc, lo, hi = scores(kix_ref[pl.ds(r0, ck), :], kpos_x + i * ck)
        score_ref[pl.ds(r0, ck), :] = sc
        return jnp.minimum(mn, lo), jnp.maximum(mx, hi)

    mn, mx = lax.fori_loop(0, npair, score_pair, (mn, mx))

    n_part = 8

    def reduce_keys(fn, init, combine):
        def part(r0, n, kpos):
            return fn(score_ref[pl.ds(r0, n), :], kpos.astype(F32))
        acc = jnp.concatenate([part(s_len, N_META, kpos_m), jnp.full((n_part * 8 - N_META, qb), init, F32)], axis=0)
        return lax.fori_loop(
            0, npair,
            lambda i, a: combine(a, part(pair_row(i), ck, kpos_x + i * ck).reshape(ck // (8 * n_part), n_part * 8, qb)),
            acc)

    def count(pred):
        acc = reduce_keys(lambda blk, kidx: jnp.where(pred(blk, kidx), 1.0, 0.0), 0.0,
                          lambda a, x: a + jnp.sum(x, axis=0))
        return jnp.sum(acc, axis=0, keepdims=True)

    def min_where(pred):
        acc = reduce_keys(lambda blk, kidx: jnp.where(pred(blk, kidx), blk, jnp.inf), jnp.inf,
                          lambda a, x: jnp.minimum(a, jnp.min(x, axis=0)))
        return jnp.min(acc, axis=0, keepdims=True)

    def bisect(_, lohi):
        lo, hi = lohi
        mid = 0.5 * lo + 0.5 * hi
        ok = count(lambda blk, kidx: blk >= mid) >= kf
        return jnp.where(ok, mid, lo), jnp.where(ok, hi, mid)

    lo, _ = lax.fori_loop(0, N_BISECT, bisect, (mn, mx))
    c_lo = count(lambda blk, kidx: blk >= lo)
    inexact = jnp.max(jnp.where(c_lo > kf, 1.0, 0.0)) > 0.0

    def exact_threshold(lo):
        def step(state):
            lo, _ = state
            m = min_where(lambda blk, kidx: blk >= lo)
            adv = count(lambda blk, kidx: blk > m) >= kf
            nxt = min_where(lambda blk, kidx: blk > m)
            return jnp.where(adv, nxt, lo), jnp.max(jnp.where(adv, 1.0, 0.0))

        lo, _ = lax.while_loop(lambda st: st[1] > 0.0, step, (lo, jnp.float32(1.0)))
        m = min_where(lambda blk, kidx: blk >= lo)
        r = kf - count(lambda blk, kidx: blk > m)
        tied = count(lambda blk, kidx: blk == m) > r

        def idx_bisect(_, ab):
            a, b = ab
            mid = jnp.floor(0.5 * (a + b))
            ok = count(lambda blk, kidx: (blk == m) & (kidx <= mid)) >= r
            return jnp.where(ok, a, mid), jnp.where(ok, mid, b)

        _, b = lax.fori_loop(0, int(math.ceil(math.log2(n_pos))) + 1, idx_bisect,
                             (jnp.full((1, qb), -1.0, F32), jnp.full((1, qb), float(n_pos - 1), F32)))
        return m, jnp.where(tied, b, float(n_pos))

    thr, cidx = lax.cond(inexact, exact_threshold,
                         lambda lo: (lo, jnp.full((1, qb), float(n_pos), F32)), lo)

    scale2 = A_HEAD_DIM ** -0.5 * math.log2(math.e)
    qg = [jnp.concatenate([q_ref[:, (n * group + h) * LANES:(n * group + h + 1) * LANES] for h in range(group)], axis=0)
          for n in range(A_KV_HEADS)]
    acc_ref[...] = jnp.zeros_like(acc_ref)

    def logits(kcs):
        return tuple(_nt(kcs[n], qg[n]) for n in range(A_KV_HEADS))

    def attend(ms, sts, blk, kpos, vtcs):
        sel = (blk > thr) | ((blk == thr) & (kpos.astype(F32) <= cidx))
        bias = jnp.where(sel, 0.0, -jnp.inf)
        bias = jnp.concatenate([bias] * group, axis=1)
        new_ms, ps, alphas = [], [], []
        for n in range(A_KV_HEADS):
            st = sts[n] * scale2 + bias
            m_new = jnp.maximum(ms[n], jnp.max(st, axis=0, keepdims=True))
            ps.append(jnp.exp2(st - m_new).astype(BF16))
            alphas.append(jnp.exp2(ms[n] - m_new))
            new_ms.append(m_new)
        for n in range(A_KV_HEADS):
            p = ps[n]
            if p.shape[0] < vtcs[n].shape[1]:
                p = jnp.concatenate([jnp.zeros((vtcs[n].shape[1] - p.shape[0], p.shape[1]), BF16), p], axis=0)
            acc_ref[n] = alphas[n] * acc_ref[n] + jnp.dot(vtcs[n], p, preferred_element_type=F32)
        return tuple(new_ms)

    head = lambda n: slice(n * LANES, (n + 1) * LANES)
    vrow = lambda n: slice(n * VT_ROWS, (n + 1) * VT_ROWS)
    ms = tuple(jnp.full((1, group * qb), NEG_INIT, F32) for _ in range(A_KV_HEADS))
    ms = attend(ms, logits([km_ref[meta_rows, head(n)] for n in range(A_KV_HEADS)]),
                score_ref[pl.ds(s_len, N_META), :], kpos_m,
                [vtm_ref[0, vrow(n), :] for n in range(A_KV_HEADS)])

    def attend_pair(i, ms):
        r0 = pair_row(i)
        return attend(ms, logits([kx_ref[pl.ds(r0, ck), head(n)] for n in range(A_KV_HEADS)]),
                      score_ref[pl.ds(r0, ck), :], kpos_x + i * ck,
                      [jnp.concatenate([vtx_ref[KEY_CHUNKS * i + c, vrow(n), :] for c in range(KEY_CHUNKS)], axis=1)
                       for n in range(A_KV_HEADS)])

    lax.fori_loop(0, npair, attend_pair, ms)

    for n in range(A_KV_HEADS):
        a = acc_ref[n]
        ot = a[:A_HEAD_DIM] / a[A_HEAD_DIM:A_HEAD_DIM + 1]
        for h in range(group):
            hd = n * group + h
            o_ref[:, hd * LANES:(hd + 1) * LANES] = ot[:, h * qb:(h + 1) * qb].T.astype(BF16)


def _dsa(qi, wit, q, ki2, k, vt, bsz, s_len, topk):
    qb = LANES
    assert s_len % (KEY_CHUNKS * qb) == 0, s_len
    nq = s_len // qb
    n_xblk = bsz * nq
    m = q.shape[0]
    n_blk = m // qb
    kvd = k.shape[1]
    group = A_HEADS // A_KV_HEADS
    qrow = lambda w: pl.BlockSpec((qb, w), lambda g: (g, 0))
    batch = lambda g: jnp.minimum(g // nq, bsz - 1)
    return pl.pallas_call(
        functools.partial(_dsa_kernel, topk, n_xblk, nq),
        grid=(n_blk,),
        in_specs=[qrow(qi.shape[1]),
                  pl.BlockSpec((IDX_HEADS, qb), lambda g: (0, g)),
                  qrow(q.shape[1]),
                  pl.BlockSpec((s_len, 2 * LANES), lambda g: (batch(g), 0)),
                  pl.BlockSpec((qb, 2 * LANES), lambda g: (n_xblk, 0)),
                  pl.BlockSpec((s_len, kvd), lambda g: (batch(g), 0)),
                  pl.BlockSpec((qb, kvd), lambda g: (n_xblk, 0)),
                  pl.BlockSpec((nq, vt.shape[1], LANES), lambda g: (batch(g), 0, 0)),
                  pl.BlockSpec((1, vt.shape[1], LANES), lambda g: (n_xblk, 0, 0))],
        out_specs=qrow(q.shape[1]),
        out_shape=jax.ShapeDtypeStruct((m, q.shape[1]), BF16),
        scratch_shapes=[pltpu.VMEM((s_len + qb, qb), F32),
                        pltpu.VMEM((A_KV_HEADS, VT_ROWS, group * qb), F32)],
        compiler_params=_cparams(("arbitrary",)),
        name="dsa",
    )(qi, wit, q, ki2, ki2, k, k, vt, vt)


def _rms64_tiles(x, g, pavg):
    x2 = x * x
    hi = x2.astype(BF16)
    lo = (x2 - hi.astype(F32)).astype(BF16)
    ms = jnp.dot(hi, pavg, preferred_element_type=F32) + jnp.dot(lo, pavg, preferred_element_type=F32)
    y = x * lax.rsqrt(ms + EPS)
    return [y[:, t * LANES:(t + 1) * LANES] * g for t in range(2)]


def _split_halves(x, par):
    lane = lax.broadcasted_iota(jnp.int32, x.shape, 1)
    keep = jnp.where(lane < B_HEAD_DIM if par == 0 else lane >= B_HEAD_DIM, x, 0.0)
    moved = pltpu.roll(keep, B_HEAD_DIM, 1)
    return (keep, moved) if par == 0 else (moved, keep)


def _kvproj_kernel(h_ref, g_ref, w_ref, kn_ref, pavg_ref, c64_ref, a64_ref, b64_ref, k2_ref, v2_ref):
    hn = _rms(h_ref[...], g_ref[...]).astype(BF16)
    kv = jnp.dot(hn, w_ref[...], preferred_element_type=F32)
    half = B_HEAD_DIM // ROPE_FRACTION // 2
    kvd = B_KV_HEADS * B_HEAD_DIM
    assert kvd == 2 * LANES
    k_tiles = _rms64_tiles(kv[:, :kvd], kn_ref[...], pavg_ref[...])
    for s in range(kvd // LANES):
        ks = _rope(k_tiles[s], c64_ref[...], a64_ref[...], b64_ref[...], half)
        vs = kv[:, kvd + s * LANES:kvd + (s + 1) * LANES]
        for par in range(2):
            n = 2 * s + par
            for src, dst in ((ks, k2_ref), (vs, v2_ref)):
                low, high = _split_halves(src, par)
                dst[:, (2 * n) * LANES:(2 * n + 1) * LANES] = low.astype(BF16)
                dst[:, (2 * n + 1) * LANES:(2 * n + 2) * LANES] = high.astype(BF16)


def _kvproj(h, g, w, kn, pavg, t64, tm=512):
    m, d = h.shape
    assert m % tm == 0, (m, tm)
    row = lambda w_: pl.BlockSpec((tm, w_), lambda i: (i, 0))
    vec = lambda w_: pl.BlockSpec((1, w_), lambda i: (0, 0))
    tab = pl.BlockSpec((tm, LANES), lambda i: (i, 0))
    wide = B_KV_HEADS * 2 * LANES
    return pl.pallas_call(
        _kvproj_kernel,
        grid=(m // tm,),
        in_specs=[row(d), vec(d), _resident(w.shape), vec(LANES), _resident(pavg.shape), tab, tab, tab],
        out_specs=[row(wide), row(wide)],
        out_shape=[jax.ShapeDtypeStruct((m, wide), BF16)] * 2,
        compiler_params=_cparams(("parallel",)),
        name="kvproj",
    )(h, g, w, kn, pavg, *t64)


def _bqproj_kernel(h_ref, g_ref, w_ref, qn_ref, pavg_ref, c64_ref, a64_ref, b64_ref, q_ref):
    hn = _rms(h_ref[...], g_ref[...]).astype(BF16)
    qf = jnp.dot(hn, w_ref[...], preferred_element_type=F32)
    half = B_HEAD_DIM // ROPE_FRACTION // 2
    scale = B_HEAD_DIM ** -0.5
    for s in range(B_HEADS * B_HEAD_DIM // (2 * LANES)):
        tiles = _rms64_tiles(qf[:, 2 * s * LANES:(2 * s + 2) * LANES], qn_ref[...], pavg_ref[...])
        for t in range(2):
            sl = slice((2 * s + t) * LANES, (2 * s + t + 1) * LANES)
            y = _rope(tiles[t], c64_ref[...], a64_ref[...], b64_ref[...], half)
            q_ref[:, sl] = (y * scale).astype(BF16)


def _bqproj(h, g, w, qn, pavg, t64, tm=256):
    m, d = h.shape
    assert m % tm == 0, (m, tm)
    row = lambda w_: pl.BlockSpec((tm, w_), lambda i: (i, 0))
    vec = lambda w_: pl.BlockSpec((1, w_), lambda i: (0, 0))
    tab = pl.BlockSpec((tm, LANES), lambda i: (i, 0))
    return pl.pallas_call(
        _bqproj_kernel,
        grid=(m // tm,),
        in_specs=[row(d), vec(d), _resident(w.shape), vec(LANES), _resident(pavg.shape), tab, tab, tab],
        out_specs=row(w.shape[1]),
        out_shape=jax.ShapeDtypeStruct((m, w.shape[1]), BF16),
        compiler_params=_cparams(("parallel",)),
        name="bqproj",
    )(h, g, w, qn, pavg, *t64)


def _swa_kernel(n_xblk, nq, sink_ref, q_ref, kp_ref, kc_ref, vp_ref, vc_ref, o_ref):
    qb = q_ref.shape[0]
    g = pl.program_id(0)
    per_kv = B_HEADS // B_KV_HEADS
    n_tile = per_kv // 2
    r = lax.broadcasted_iota(jnp.int32, (qb, 2 * qb), 0)
    col = lax.broadcasted_iota(jnp.int32, (qb, 2 * qb), 1)
    rel = qb + r - col
    first_x = jnp.where(lax.rem(g, nq) == 0, META_ROW0, 0)
    first_tail = jnp.where(g == n_xblk, qb + META_ROW0, qb)
    first_col = jnp.where(g >= n_xblk, first_tail, first_x)
    mask = (rel >= 0) & (rel < WINDOW) & (col >= first_col)
    for n in range(B_KV_HEADS):
        lo = slice(2 * n * LANES, (2 * n + 1) * LANES)
        hi = slice((2 * n + 1) * LANES, (2 * n + 2) * LANES)
        kcat = jnp.concatenate([kp_ref[:, lo], kc_ref[:, lo], kp_ref[:, hi], kc_ref[:, hi]], axis=0)
        vcat = jnp.concatenate([vp_ref[:, lo], vc_ref[:, lo], vp_ref[:, hi], vc_ref[:, hi]], axis=0)
        qst = jnp.concatenate([q_ref[:, (n * n_tile + s) * LANES:(n * n_tile + s + 1) * LANES] for s in range(n_tile)], axis=0)
        logits = _nt(qst, kcat)
        rows = []
        for s in range(n_tile):
            halves = []
            for par in range(2):
                sink = sink_ref[n * per_kv + 2 * s + par]
                lg = jnp.where(mask, logits[s * qb:(s + 1) * qb, par * 2 * qb:(par + 1) * 2 * qb], -jnp.inf)
                m = jnp.maximum(jnp.max(lg, axis=-1, keepdims=True), sink)
                e = jnp.exp(lg - m)
                p = e / (jnp.sum(e, axis=-1, keepdims=True) + jnp.exp(sink - m))
                halves.append(p.astype(BF16))
            rows.append(jnp.concatenate(halves, axis=1))
        p2 = jnp.concatenate(rows, axis=0)
        o = jnp.dot(p2, vcat, preferred_element_type=F32)
        for s in range(n_tile):
            t = n * n_tile + s
            o_ref[:, t * LANES:(t + 1) * LANES] = o[s * qb:(s + 1) * qb].astype(BF16)


def _swa(sinks, q, k2, v2, bsz, s_len):
    qb = LANES
    nq = s_len // qb
    n_xblk = bsz * nq
    m, width = q.shape
    wide = k2.shape[1]
    cur = lambda w: pl.BlockSpec((qb, w), lambda g: (g, 0))
    prev = lambda w: pl.BlockSpec(
        (qb, w), lambda g: (jnp.where((g >= n_xblk) | (lax.rem(g, nq) == 0), n_xblk, g - 1), 0))
    return pl.pallas_call(
        functools.partial(_swa_kernel, n_xblk, nq),
        grid=(m // qb,),
        in_specs=[pl.BlockSpec(memory_space=pltpu.SMEM), cur(width), prev(wide), cur(wide), prev(wide), cur(wide)],
        out_specs=cur(width),
        out_shape=jax.ShapeDtypeStruct((m, width), BF16),
        compiler_params=_cparams(("parallel",)),
        name="swa",
    )(sinks, q, k2, k2, v2, v2)


def _rope_tables(pos, head_dim):
    half = head_dim // ROPE_FRACTION // 2
    inv = ROPE_THETA ** (-jnp.arange(half, dtype=F32) / half)
    ang = pos.astype(F32)[:, None] * inv[None, :]
    cos, sin = jnp.cos(ang), jnp.sin(ang)
    lp = jnp.arange(LANES) % head_dim
    idx = lp % half
    c = jnp.where(lp < 2 * half, cos[:, idx], 1.0)
    a = jnp.where(lp < half, -sin[:, idx], 0.0)
    b = jnp.where((lp >= half) & (lp < 2 * half), sin[:, idx], 0.0)
    return c, a, b


def _pad_cols(w, mult):
    pad = (-w.shape[-1]) % mult
    return jnp.pad(w, ((0, 0), (0, pad))) if pad else w


def kernel(x, meta_tokens, ffn1_norm, ffn1_w_gate, ffn1_w_up, ffn1_w_down, ffn2_norm, ffn2_w_gate, ffn2_w_up, ffn2_w_down, a_norm, a_w_in, a_q_norm, a_k_norm, a_idx_k_norm, a_w_out, kv_norm, kv_w, kv_k_norm, b_norm, b_w_q, b_q_norm, b_sinks, b_w_out):
    bsz, s_len, d = x.shape
    depth = ffn1_norm.shape[0]
    n_a = a_norm.shape[0]
    topk = min(TOPK_MAX, s_len // 4)
    n_x = bsz * s_len
    assert n_x % TAIL_ROWS == 0 and meta_tokens.shape[0] == N_META

    zeros = lambda n: jnp.zeros((n, d), x.dtype)
    tail_rows = jnp.concatenate([zeros(META_ROW0), meta_tokens.astype(x.dtype), zeros(TAIL_ROWS - LANES)], axis=0)
    h = x.reshape(n_x, d)
    seq_pos = N_META + jnp.arange(s_len, dtype=jnp.int32)
    tail_pos = jnp.clip(jnp.arange(TAIL_ROWS, dtype=jnp.int32) - META_ROW0, 0, N_META - 1)
    tail_pos = jnp.where(jnp.arange(TAIL_ROWS) < LANES, tail_pos, 0)
    pos = jnp.concatenate([jnp.tile(seq_pos, bsz), tail_pos])
    t128 = _rope_tables(pos, A_HEAD_DIM)
    t64 = _rope_tables(pos, B_HEAD_DIM)
    seg = jnp.arange(2 * LANES) // B_HEAD_DIM
    pavg = jnp.where(seg[:, None] == seg[None, :], 1.0 / B_HEAD_DIM, 0.0).astype(BF16)
    tile2 = lambda g: jnp.tile(g.astype(F32), 2)[None, :]

    w1 = tuple(w.astype(BF16) for w in (ffn1_w_gate, ffn1_w_up, ffn1_w_down))
    w2 = tuple(w.astype(BF16) for w in (ffn2_w_gate, ffn2_w_up, ffn2_w_down))

    q_dim, kv_dim, qi_dim = A_HEADS * A_HEAD_DIM, A_KV_HEADS * A_HEAD_DIM, IDX_HEADS * IDX_DIM
    k2 = v2 = None
    for layer in range(depth):
        if layer == n_a:
            k2, v2 = _kvproj(h, kv_norm[None, :], kv_w.astype(BF16), tile2(kv_k_norm), pavg, t64)
        h = _ffn(h, ffn1_norm[layer][None, :], *w1, layer, tail_rows=tail_rows if layer == 0 else None)
        if layer < n_a:
            i = layer
            w = a_w_in[i].astype(BF16)
            o0, o1, o2, o3 = q_dim, q_dim + kv_dim, q_dim + 2 * kv_dim, q_dim + 2 * kv_dim + qi_dim
            wkw = _pad_cols(w[:, o3:], LANES)
            ikn = _pad_cols(a_idx_k_norm[i][None, :], LANES)
            q, k, vt, qi, ki2, wit = _aproj(
                h, a_norm[i][None, :], w[:, :o0], w[:, o0:o1], w[:, o1:o2].T, w[:, o2:o3], wkw,
                a_q_norm[i][None, :], a_k_norm[i][None, :], ikn, t128, t64)
            o = _dsa(qi, wit, q, ki2, k, vt, bsz, s_len, topk)
            h = _oproj(h, o, a_w_out[i].astype(BF16))
        else:
            i = layer - n_a
            q = _bqproj(h, b_norm[i][None, :], b_w_q[i].astype(BF16), tile2(b_q_norm[i]), pavg, t64)
            o = _swa(b_sinks[i].astype(F32), q, k2, v2, bsz, s_len)
            h = _oproj(h, o, b_w_out[i].astype(BF16))
        h = _ffn(h, ffn2_norm[layer][None, :], *w2, layer, rows=n_x if layer == depth - 1 else None)
    return h.reshape(bsz, s_len, d)
```

```python
import functools
import math

import jax
import jax.numpy as jnp
from jax import lax
from jax.experimental import pallas as pl
from jax.experimental.pallas import tpu as pltpu

F32 = jnp.float32
BF16 = jnp.bfloat16

LANES = 128
N_META = 16
META_ROW0 = LANES - N_META
TAIL_ROWS = 512
EPS = 1e-6
ROPE_THETA = 500000.0
ROPE_FRACTION = 4
A_HEADS, A_KV_HEADS, A_HEAD_DIM = 16, 4, 128
IDX_HEADS, IDX_DIM = 16, 64
TOPK_MAX = 256
B_HEADS, B_KV_HEADS, B_HEAD_DIM = 32, 4, 64
WINDOW = 128
VMEM_LIMIT = 56 * 1024 * 1024
NEG_INIT = -1e30
N_BISECT = 26
KEY_CHUNKS = 4
VT_ROWS = A_HEAD_DIM + 16


def _cparams(sem):
    return pltpu.CompilerParams(dimension_semantics=sem, vmem_limit_bytes=VMEM_LIMIT)


def _resident(shape):
    nd = len(shape)
    return pl.BlockSpec(shape, lambda *_: (0,) * nd, pipeline_mode=pl.Buffered(1))


def _rms(x, g):
    ms = jnp.mean(x * x, axis=-1, keepdims=True)
    return x * lax.rsqrt(ms + EPS) * g


def _rope(y, c, a, b, half):
    return y * c + pltpu.roll(y, LANES - half, 1) * a + pltpu.roll(y, half, 1) * b


def _nt(a, b):
    return lax.dot_general(a, b, (((1,), (1,)), ((), ())), preferred_element_type=F32)


def _ffn_kernel(tail, short_tile, h_ref, t_ref, g_ref, wg_ref, wu_ref, wd_ref, o_ref, n_ref):
    i = pl.program_id(0)
    f = pl.program_id(1)
    last = pl.num_programs(1) - 1
    tf = wg_ref.shape[-1]
    tm = o_ref.shape[0]

    def hidden(n, width):
        gate = jnp.dot(n, wg_ref[:, :width], preferred_element_type=F32)
        up = jnp.dot(n, wu_ref[:, :width], preferred_element_type=F32)
        act = (0.5 * gate) * jax.nn.sigmoid(gate) * up
        return jnp.dot(act.astype(BF16), wd_ref[:width, :], preferred_element_type=F32)

    def first(src_ref, rows):
        x = src_ref[:rows]
        n = _rms(x, g_ref[...]).astype(BF16)
        n_ref[:rows] = n
        o_ref[:rows] = x + hidden(n, tf)
        if rows < tm:
            o_ref[rows:] = jnp.zeros((tm - rows, o_ref.shape[1]), F32)

    def step(width, rows):
        o_ref[:rows] += hidden(n_ref[:rows], width)

    def row_tile(on, src_ref, rows):
        pl.when(on & (f == 0))(lambda: first(src_ref, rows))
        if tail == tf:
            pl.when(on & (f > 0))(lambda: step(tf, rows))
        else:
            pl.when(on & (f > 0) & (f < last))(lambda: step(tf, rows))
            pl.when(on & (f == last))(lambda: step(tail, rows))

    if short_tile is None:
        row_tile(i >= 0, h_ref, tm)
    else:
        row_tile(i != short_tile, h_ref, tm)
        row_tile(i == short_tile, h_ref if t_ref is None else t_ref, LANES)


def _ffn_kernel_1src(tail, short_tile, h_ref, *rest):
    _ffn_kernel(tail, short_tile, h_ref, None, *rest)


def _ffn(h, g, wg, wu, wd, layer, rows=None, tail_rows=None, short_last=False, tm=512, tf=512):
    m, d = h.shape
    m = m if rows is None else rows
    assert m % tm == 0, (m, tm)
    n_main = m // tm
    n_tile = n_main
    ff = wg.shape[-1]
    nf = pl.cdiv(ff, tf)
    tail = ff - (nf - 1) * tf
    assert tail % LANES == 0 and nf >= 2, ff
    in_specs = [pl.BlockSpec((tm, d), lambda i, f: (jnp.minimum(i, n_main - 1), 0))]
    args = [h]
    kern = _ffn_kernel_1src
    if tail_rows is not None:
        assert tail_rows.shape == (tm, d)
        n_tile += 1
        in_specs.append(pl.BlockSpec((tm, d), lambda i, f: (0, 0)))
        args.append(tail_rows)
        kern = _ffn_kernel
    assert short_last or tail_rows is None
    return pl.pallas_call(
        functools.partial(kern, tail, n_tile - 1 if short_last else None),
        grid=(n_tile, nf),
        in_specs=in_specs + [
            pl.BlockSpec((1, d), lambda i, f: (0, 0)),
            pl.BlockSpec((None, d, tf), lambda i, f: (layer, 0, f)),
            pl.BlockSpec((None, d, tf), lambda i, f: (layer, 0, f)),
            pl.BlockSpec((None, tf, d), lambda i, f: (layer, f, 0)),
        ],
        out_specs=pl.BlockSpec((tm, d), lambda i, f: (i, 0)),
        out_shape=jax.ShapeDtypeStruct((n_tile * tm, d), F32),
        scratch_shapes=[pltpu.VMEM((tm, d), BF16)],
        compiler_params=_cparams(("parallel", "arbitrary")),
        name="ffn",
    )(*args, g, wg, wu, wd)


def _oproj_kernel(h_ref, o_ref, w_ref, out_ref):
    out_ref[...] = h_ref[...] + jnp.dot(o_ref[...], w_ref[...], preferred_element_type=F32)


def _oproj(h, o, w, tm=512):
    m, d = h.shape
    assert m % tm == 0, (m, tm)
    return pl.pallas_call(
        _oproj_kernel,
        grid=(m // tm,),
        in_specs=[
            pl.BlockSpec((tm, d), lambda i: (i, 0)),
            pl.BlockSpec((tm, o.shape[1]), lambda i: (i, 0)),
            _resident(w.shape),
        ],
        out_specs=pl.BlockSpec((tm, d), lambda i: (i, 0)),
        out_shape=jax.ShapeDtypeStruct((m, d), F32),
        compiler_params=_cparams(("parallel",)),
        name="oproj",
    )(h, o, w)


def _aproj_kernel(h_ref, g_ref, wq_ref, wk_ref, wvt_ref, wqi_ref, wkw_ref, qn_ref, kn_ref, ikn_ref,
                  c128_ref, a128_ref, b128_ref, c64_ref, a64_ref, b64_ref,
                  q_ref, k_ref, vt_ref, qi_ref, ki2_ref, wit_ref):
    tm = h_ref.shape[0]
    hn = _rms(h_ref[...], g_ref[...]).astype(BF16)
    c128, a128, b128 = c128_ref[...], a128_ref[...], b128_ref[...]
    c64, a64, b64 = c64_ref[...], a64_ref[...], b64_ref[...]
    half128 = A_HEAD_DIM // ROPE_FRACTION // 2
    half64 = IDX_DIM // ROPE_FRACTION // 2

    qf = jnp.dot(hn, wq_ref[...], preferred_element_type=F32)
    for hd in range(A_HEADS):
        sl = slice(hd * LANES, (hd + 1) * LANES)
        q_ref[:, sl] = _rope(_rms(qf[:, sl], qn_ref[...]), c128, a128, b128, half128).astype(BF16)

    kf = jnp.dot(hn, wk_ref[...], preferred_element_type=F32)
    for hd in range(A_KV_HEADS):
        sl = slice(hd * LANES, (hd + 1) * LANES)
        k_ref[:, sl] = _rope(_rms(kf[:, sl], kn_ref[...]), c128, a128, b128, half128).astype(BF16)

    vt = _nt(wvt_ref[...], hn)
    tail = jnp.where(lax.broadcasted_iota(jnp.int32, (VT_ROWS - A_HEAD_DIM, LANES), 0) == 0, 1.0, 0.0).astype(BF16)
    for c in range(tm // LANES):
        for n in range(A_KV_HEADS):
            vt_ref[c, n * VT_ROWS:n * VT_ROWS + A_HEAD_DIM, :] = (
                vt[n * A_HEAD_DIM:(n + 1) * A_HEAD_DIM, c * LANES:(c + 1) * LANES].astype(BF16))
            vt_ref[c, n * VT_ROWS + A_HEAD_DIM:(n + 1) * VT_ROWS, :] = tail

    qif = jnp.dot(hn, wqi_ref[...], preferred_element_type=F32)
    for s in range(IDX_HEADS * IDX_DIM // LANES):
        sl = slice(s * LANES, (s + 1) * LANES)
        qi_ref[:, sl] = _rope(qif[:, sl], c64, a64, b64, half64).astype(BF16)

    kw = jnp.dot(hn, wkw_ref[...], preferred_element_type=F32)
    lane = lax.broadcasted_iota(jnp.int32, kw.shape, 1)
    ms = jnp.sum(jnp.where(lane < IDX_DIM, kw * kw, 0.0), axis=-1, keepdims=True) * (1.0 / IDX_DIM)
    ki = _rope(kw * lax.rsqrt(ms + EPS) * ikn_ref[...], c64, a64, b64, half64)
    ki2_ref[:, :LANES] = ki.astype(BF16)
    ki2_ref[:, LANES:] = pltpu.roll(ki, IDX_DIM, 1).astype(BF16)
    wit_ref[...] = kw.T[IDX_DIM:IDX_DIM + IDX_HEADS, :] * ((IDX_HEADS * IDX_DIM) ** -0.5)


def _aproj(h, g, wq, wk, wvt, wqi, wkw, qn, kn, ikn, t128, t64, tm=256):
    m, d = h.shape
    assert m % tm == 0, (m, tm)
    row = lambda w: pl.BlockSpec((tm, w), lambda i: (i, 0))
    tab = pl.BlockSpec((tm, LANES), lambda i: (i, 0))
    vec = lambda w: pl.BlockSpec((1, w), lambda i: (0, 0))
    kvd = wk.shape[1]
    return pl.pallas_call(
        _aproj_kernel,
        grid=(m // tm,),
        in_specs=[row(d), vec(d), _resident(wq.shape), _resident(wk.shape), _resident(wvt.shape),
                  _resident(wqi.shape), _resident(wkw.shape), vec(LANES), vec(LANES), vec(LANES),
                  tab, tab, tab, tab, tab, tab],
        out_specs=[row(wq.shape[1]), row(kvd),
                   pl.BlockSpec((tm // LANES, A_KV_HEADS * VT_ROWS, LANES), lambda i: (i, 0, 0)),
                   row(wqi.shape[1]), row(2 * LANES),
                   pl.BlockSpec((IDX_HEADS, tm), lambda i: (0, i))],
        out_shape=[jax.ShapeDtypeStruct((m, wq.shape[1]), BF16),
                   jax.ShapeDtypeStruct((m, kvd), BF16),
                   jax.ShapeDtypeStruct((m // LANES, A_KV_HEADS * VT_ROWS, LANES), BF16),
                   jax.ShapeDtypeStruct((m, wqi.shape[1]), BF16),
                   jax.ShapeDtypeStruct((m, 2 * LANES), BF16),
                   jax.ShapeDtypeStruct((IDX_HEADS, m), F32)],
        compiler_params=_cparams(("parallel",)),
        name="aproj",
    )(h, g, wq, wk, wvt, wqi, wkw, qn, kn, ikn, *t128, *t64)


def _dsa_kernel(topk, n_xblk, nq, qi_ref, wit_ref, q_ref, kix_ref, kim_ref, kx_ref, km_ref, vtx_ref, vtm_ref,
                o_ref, score_ref, acc_ref):
    qb = q_ref.shape[0]
    ck = KEY_CHUNKS * LANES
    s_len = kx_ref.shape[0]
    g = pl.program_id(0)
    is_tail = g >= n_xblk
    j = lax.rem(g, nq)
    npair = jnp.where(is_tail, 0, (j + KEY_CHUNKS) // KEY_CHUNKS)
    kf = jnp.float32(topk)
    group = A_HEADS // A_KV_HEADS
    n_slab = IDX_HEADS * IDX_DIM // LANES
    n_pos = s_len + N_META

    lane = lax.broadcasted_iota(jnp.int32, (1, qb), 1)
    tail_pos = jnp.where(g == n_xblk, jnp.maximum(lane - META_ROW0, 0), 0)
    qpos = jnp.where(is_tail, tail_pos, N_META + j * qb + lane)
    kpos_m = lax.broadcasted_iota(jnp.int32, (N_META, qb), 0)
    meta_rows = slice(META_ROW0, LANES)
    kpos_x = lax.broadcasted_iota(jnp.int32, (ck, qb), 0) + N_META

    def pair_row(i):
        return pl.multiple_of(i * ck, ck)

    qi = qi_ref[...]
    qs = jnp.concatenate([qi[:, s * LANES:(s + 1) * LANES] for s in range(n_slab)], axis=0)

    def scores(kc, kpos):
        n = kc.shape[0]
        kcat = jnp.concatenate([kc[:, :LANES], kc[:, LANES:]], axis=0)
        s2 = _nt(kcat, qs)
        acc = jnp.zeros((n, qb), F32)
        for s in range(n_slab):
            for par in range(2):
                blk = s2[par * n:(par + 1) * n, s * qb:(s + 1) * qb]
                acc = acc + jnp.maximum(blk, 0.0) * wit_ref[2 * s + par:2 * s + par + 1, :]
        vis = kpos <= qpos
        lo = jnp.min(jnp.where(vis, acc, jnp.inf), axis=0, keepdims=True)
        hi = jnp.max(jnp.where(vis, acc, -jnp.inf), axis=0, keepdims=True)
        return jnp.where(vis, acc, -jnp.inf), lo, hi

    sc, mn, mx = scores(kim_ref[meta_rows, :], kpos_m)
    score_ref[pl.ds(s_len, N_META), :] = sc

    def score_pair(i, carry):
        mn, mx = carry
        r0 = pair_row(i)
        sc, lo, hi = scores(kix_ref[pl.ds(r0, ck), :], kpos_x + i * ck)
        score_ref[pl.ds(r0, ck), :] = sc
        return jnp.minimum(mn, lo), jnp.maximum(mx, hi)

    mn, mx = lax.fori_loop(0, npair, score_pair, (mn, mx))

    n_part = 8

    def reduce_keys(fn, init, combine):
        def part(r0, n, kpos):
            return fn(score_ref[pl.ds(r0, n), :], kpos.astype(F32))
        acc = jnp.concatenate([part(s_len, N_META, kpos_m), jnp.full((n_part * 8 - N_META, qb), init, F32)], axis=0)
        return lax.fori_loop(
            0, npair,
            lambda i, a: combine(a, part(pair_row(i), ck, kpos_x + i * ck).reshape(ck // (8 * n_part), n_part * 8, qb)),
            acc)

    def count(pred):
        acc = reduce_keys(lambda blk, kidx: jnp.where(pred(blk, kidx), 1.0, 0.0), 0.0,
                          lambda a, x: a + jnp.sum(x, axis=0))
        return jnp.sum(acc, axis=0, keepdims=True)

    def min_where(pred):
        acc = reduce_keys(lambda blk, kidx: jnp.where(pred(blk, kidx), blk, jnp.inf), jnp.inf,
                          lambda a, x: jnp.minimum(a, jnp.min(x, axis=0)))
        return jnp.min(acc, axis=0, keepdims=True)

    def bisect(_, lohi):
        lo, hi = lohi
        mid = 0.5 * lo + 0.5 * hi
        ok = count(lambda blk, kidx: blk >= mid) >= kf
        return jnp.where(ok, mid, lo), jnp.where(ok, hi, mid)

    lo, _ = lax.fori_loop(0, N_BISECT, bisect, (mn, mx))
    c_lo = count(lambda blk, kidx: blk >= lo)
    inexact = jnp.max(jnp.where(c_lo > kf, 1.0, 0.0)) > 0.0

    def exact_threshold(lo):
        def step(state):
            lo, _ = state
            m = min_where(lambda blk, kidx: blk >= lo)
            adv = count(lambda blk, kidx: blk > m) >= kf
            nxt = min_where(lambda blk, kidx: blk > m)
            return jnp.where(adv, nxt, lo), jnp.max(jnp.where(adv, 1.0, 0.0))

        lo, _ = lax.while_loop(lambda st: st[1] > 0.0, step, (lo, jnp.float32(1.0)))
        m = min_where(lambda blk, kidx: blk >= lo)
        r = kf - count(lambda blk, kidx: blk > m)
        tied = count(lambda blk, kidx: blk == m) > r

        def idx_bisect(_, ab):
            a, b = ab
            mid = jnp.floor(0.5 * (a + b))
            ok = count(lambda blk, kidx: (blk == m) & (kidx <= mid)) >= r
            return jnp.where(ok, a, mid), jnp.where(ok, mid, b)

        _, b = lax.fori_loop(0, int(math.ceil(math.log2(n_pos))) + 1, idx_bisect,
                             (jnp.full((1, qb), -1.0, F32), jnp.full((1, qb), float(n_pos - 1), F32)))
        return m, jnp.where(tied, b, float(n_pos))

    thr, cidx = lax.cond(inexact, exact_threshold,
                         lambda lo: (lo, jnp.full((1, qb), float(n_pos), F32)), lo)

    scale2 = A_HEAD_DIM ** -0.5 * math.log2(math.e)
    qg = [jnp.concatenate([q_ref[:, (n * group + h) * LANES:(n * group + h + 1) * LANES] for h in range(group)], axis=0)
          for n in range(A_KV_HEADS)]
    acc_ref[...] = jnp.zeros_like(acc_ref)

    def logits(kcs):
        return tuple(_nt(kcs[n], qg[n]) for n in range(A_KV_HEADS))

    def attend(ms, sts, blk, kpos, vtcs):
        sel = (blk > thr) | ((blk == thr) & (kpos.astype(F32) <= cidx))
        bias = jnp.where(sel, 0.0, -jnp.inf)
        bias = jnp.concatenate([bias] * group, axis=1)
        new_ms, ps, alphas = [], [], []
        for n in range(A_KV_HEADS):
            st = sts[n] * scale2 + bias
            m_new = jnp.maximum(ms[n], jnp.max(st, axis=0, keepdims=True))
            ps.append(jnp.exp2(st - m_new).astype(BF16))
            alphas.append(jnp.exp2(ms[n] - m_new))
            new_ms.append(m_new)
        for n in range(A_KV_HEADS):
            p = ps[n]
            if p.shape[0] < vtcs[n].shape[1]:
                p = jnp.concatenate([jnp.zeros((vtcs[n].shape[1] - p.shape[0], p.shape[1]), BF16), p], axis=0)
            acc_ref[n] = alphas[n] * acc_ref[n] + jnp.dot(vtcs[n], p, preferred_element_type=F32)
        return tuple(new_ms)

    head = lambda n: slice(n * LANES, (n + 1) * LANES)
    vrow = lambda n: slice(n * VT_ROWS, (n + 1) * VT_ROWS)
    ms = tuple(jnp.full((1, group * qb), NEG_INIT, F32) for _ in range(A_KV_HEADS))
    ms = attend(ms, logits([km_ref[meta_rows, head(n)] for n in range(A_KV_HEADS)]),
                score_ref[pl.ds(s_len, N_META), :], kpos_m,
                [vtm_ref[0, vrow(n), :] for n in range(A_KV_HEADS)])

    def attend_pair(i, ms):
        r0 = pair_row(i)
        return attend(ms, logits([kx_ref[pl.ds(r0, ck), head(n)] for n in range(A_KV_HEADS)]),
                      score_ref[pl.ds(r0, ck), :], kpos_x + i * ck,
                      [jnp.concatenate([vtx_ref[KEY_CHUNKS * i + c, vrow(n), :] for c in range(KEY_CHUNKS)], axis=1)
                       for n in range(A_KV_HEADS)])

    lax.fori_loop(0, npair, attend_pair, ms)

    for n in range(A_KV_HEADS):
        a = acc_ref[n]
        ot = a[:A_HEAD_DIM] / a[A_HEAD_DIM:A_HEAD_DIM + 1]
        for h in range(group):
            hd = n * group + h
            o_ref[:, hd * LANES:(hd + 1) * LANES] = ot[:, h * qb:(h + 1) * qb].T.astype(BF16)


def _dsa(qi, wit, q, ki2, k, vt, bsz, s_len, topk):
    qb = LANES
    assert s_len % (KEY_CHUNKS * qb) == 0, s_len
    nq = s_len // qb
    n_xblk = bsz * nq
    m = q.shape[0]
    n_blk = m // qb
    kvd = k.shape[1]
    group = A_HEADS // A_KV_HEADS
    qrow = lambda w: pl.BlockSpec((qb, w), lambda g: (g, 0))
    batch = lambda g: jnp.minimum(g // nq, bsz - 1)
    return pl.pallas_call(
        functools.partial(_dsa_kernel, topk, n_xblk, nq),
        grid=(n_blk,),
        in_specs=[qrow(qi.shape[1]),
                  pl.BlockSpec((IDX_HEADS, qb), lambda g: (0, g)),
                  qrow(q.shape[1]),
                  pl.BlockSpec((s_len, 2 * LANES), lambda g: (batch(g), 0)),
                  pl.BlockSpec((qb, 2 * LANES), lambda g: (n_xblk, 0)),
                  pl.BlockSpec((s_len, kvd), lambda g: (batch(g), 0)),
                  pl.BlockSpec((qb, kvd), lambda g: (n_xblk, 0)),
                  pl.BlockSpec((nq, vt.shape[1], LANES), lambda g: (batch(g), 0, 0)),
                  pl.BlockSpec((1, vt.shape[1], LANES), lambda g: (n_xblk, 0, 0))],
        out_specs=qrow(q.shape[1]),
        out_shape=jax.ShapeDtypeStruct((m, q.shape[1]), BF16),
        scratch_shapes=[pltpu.VMEM((s_len + qb, qb), F32),
                        pltpu.VMEM((A_KV_HEADS, VT_ROWS, group * qb), F32)],
        compiler_params=_cparams(("arbitrary",)),
        name="dsa",
    )(qi, wit, q, ki2, ki2, k, k, vt, vt)


def _rms64_tiles(x, g, pavg):
    x2 = x * x
    hi = x2.astype(BF16)
    lo = (x2 - hi.astype(F32)).astype(BF16)
    ms = jnp.dot(hi, pavg, preferred_element_type=F32) + jnp.dot(lo, pavg, preferred_element_type=F32)
    y = x * lax.rsqrt(ms + EPS)
    return [y[:, t * LANES:(t + 1) * LANES] * g for t in range(2)]


def _split_halves(x, par):
    lane = lax.broadcasted_iota(jnp.int32, x.shape, 1)
    keep = jnp.where(lane < B_HEAD_DIM if par == 0 else lane >= B_HEAD_DIM, x, 0.0)
    moved = pltpu.roll(keep, B_HEAD_DIM, 1)
    return (keep, moved) if par == 0 else (moved, keep)


def _kvproj_kernel(h_ref, g_ref, w_ref, kn_ref, pavg_ref, c64_ref, a64_ref, b64_ref, k2_ref, v2_ref):
    hn = _rms(h_ref[...], g_ref[...]).astype(BF16)
    kv = jnp.dot(hn, w_ref[...], preferred_element_type=F32)
    half = B_HEAD_DIM // ROPE_FRACTION // 2
    kvd = B_KV_HEADS * B_HEAD_DIM
    assert kvd == 2 * LANES
    k_tiles = _rms64_tiles(kv[:, :kvd], kn_ref[...], pavg_ref[...])
    for s in range(kvd // LANES):
        ks = _rope(k_tiles[s], c64_ref[...], a64_ref[...], b64_ref[...], half)
        vs = kv[:, kvd + s * LANES:kvd + (s + 1) * LANES]
        for par in range(2):
            n = 2 * s + par
            for src, dst in ((ks, k2_ref), (vs, v2_ref)):
                low, high = _split_halves(src, par)
                dst[:, (2 * n) * LANES:(2 * n + 1) * LANES] = low.astype(BF16)
                dst[:, (2 * n + 1) * LANES:(2 * n + 2) * LANES] = high.astype(BF16)


def _kvproj(h, g, w, kn, pavg, t64, tm=512):
    m, d = h.shape
    assert m % tm == 0, (m, tm)
    row = lambda w_: pl.BlockSpec((tm, w_), lambda i: (i, 0))
    vec = lambda w_: pl.BlockSpec((1, w_), lambda i: (0, 0))
    tab = pl.BlockSpec((tm, LANES), lambda i: (i, 0))
    wide = B_KV_HEADS * 2 * LANES
    return pl.pallas_call(
        _kvproj_kernel,
        grid=(m // tm,),
        in_specs=[row(d), vec(d), _resident(w.shape), vec(LANES), _resident(pavg.shape), tab, tab, tab],
        out_specs=[row(wide), row(wide)],
        out_shape=[jax.ShapeDtypeStruct((m, wide), BF16)] * 2,
        compiler_params=_cparams(("parallel",)),
        name="kvproj",
    )(h, g, w, kn, pavg, *t64)


def _bqproj_kernel(h_ref, g_ref, w_ref, qn_ref, pavg_ref, c64_ref, a64_ref, b64_ref, q_ref):
    hn = _rms(h_ref[...], g_ref[...]).astype(BF16)
    qf = jnp.dot(hn, w_ref[...], preferred_element_type=F32)
    half = B_HEAD_DIM // ROPE_FRACTION // 2
    scale = B_HEAD_DIM ** -0.5
    for s in range(B_HEADS * B_HEAD_DIM // (2 * LANES)):
        tiles = _rms64_tiles(qf[:, 2 * s * LANES:(2 * s + 2) * LANES], qn_ref[...], pavg_ref[...])
        for t in range(2):
            sl = slice((2 * s + t) * LANES, (2 * s + t + 1) * LANES)
            y = _rope(tiles[t], c64_ref[...], a64_ref[...], b64_ref[...], half)
            q_ref[:, sl] = (y * scale).astype(BF16)


def _bqproj(h, g, w, qn, pavg, t64, tm=256):
    m, d = h.shape
    assert m % tm == 0, (m, tm)
    row = lambda w_: pl.BlockSpec((tm, w_), lambda i: (i, 0))
    vec = lambda w_: pl.BlockSpec((1, w_), lambda i: (0, 0))
    tab = pl.BlockSpec((tm, LANES), lambda i: (i, 0))
    return pl.pallas_call(
        _bqproj_kernel,
        grid=(m // tm,),
        in_specs=[row(d), vec(d), _resident(w.shape), vec(LANES), _resident(pavg.shape), tab, tab, tab],
        out_specs=row(w.shape[1]),
        out_shape=jax.ShapeDtypeStruct((m, w.shape[1]), BF16),
        compiler_params=_cparams(("parallel",)),
        name="bqproj",
    )(h, g, w, qn, pavg, *t64)


def _swa_kernel(n_xblk, nq, sink_ref, q_ref, kp_ref, kc_ref, vp_ref, vc_ref, o_ref):
    qb = q_ref.shape[0]
    g = pl.program_id(0)
    per_kv = B_HEADS // B_KV_HEADS
    n_tile = per_kv // 2
    r = lax.broadcasted_iota(jnp.int32, (qb, 2 * qb), 0)
    col = lax.broadcasted_iota(jnp.int32, (qb, 2 * qb), 1)
    rel = qb + r - col
    first_x = jnp.where(lax.rem(g, nq) == 0, META_ROW0, 0)
    first_tail = jnp.where(g == n_xblk, qb + META_ROW0, qb)
    first_col = jnp.where(g >= n_xblk, first_tail, first_x)
    mask = (rel >= 0) & (rel < WINDOW) & (col >= first_col)
    for n in range(B_KV_HEADS):
        lo = slice(2 * n * LANES, (2 * n + 1) * LANES)
        hi = slice((2 * n + 1) * LANES, (2 * n + 2) * LANES)
        kcat = jnp.concatenate([kp_ref[:, lo], kc_ref[:, lo], kp_ref[:, hi], kc_ref[:, hi]], axis=0)
        vcat = jnp.concatenate([vp_ref[:, lo], vc_ref[:, lo], vp_ref[:, hi], vc_ref[:, hi]], axis=0)
        qst = jnp.concatenate([q_ref[:, (n * n_tile + s) * LANES:(n * n_tile + s + 1) * LANES] for s in range(n_tile)], axis=0)
        logits = _nt(qst, kcat)
        rows = []
        for s in range(n_tile):
            halves = []
            for par in range(2):
                sink = sink_ref[n * per_kv + 2 * s + par]
                lg = jnp.where(mask, logits[s * qb:(s + 1) * qb, par * 2 * qb:(par + 1) * 2 * qb], -jnp.inf)
                m = jnp.maximum(jnp.max(lg, axis=-1, keepdims=True), sink)
                e = jnp.exp(lg - m)
                p = e / (jnp.sum(e, axis=-1, keepdims=True) + jnp.exp(sink - m))
                halves.append(p.astype(BF16))
            rows.append(jnp.concatenate(halves, axis=1))
        p2 = jnp.concatenate(rows, axis=0)
        o = jnp.dot(p2, vcat, preferred_element_type=F32)
        for s in range(n_tile):
            t = n * n_tile + s
            o_ref[:, t * LANES:(t + 1) * LANES] = o[s * qb:(s + 1) * qb].astype(BF16)


def _swa(sinks, q, k2, v2, bsz, s_len):
    qb = LANES
    nq = s_len // qb
    n_xblk = bsz * nq
    m, width = q.shape
    wide = k2.shape[1]
    cur = lambda w: pl.BlockSpec((qb, w), lambda g: (g, 0))
    prev = lambda w: pl.BlockSpec(
        (qb, w), lambda g: (jnp.where((g >= n_xblk) | (lax.rem(g, nq) == 0), n_xblk, g - 1), 0))
    return pl.pallas_call(
        functools.partial(_swa_kernel, n_xblk, nq),
        grid=(m // qb,),
        in_specs=[pl.BlockSpec(memory_space=pltpu.SMEM), cur(width), prev(wide), cur(wide), prev(wide), cur(wide)],
        out_specs=cur(width),
        out_shape=jax.ShapeDtypeStruct((m, width), BF16),
        compiler_params=_cparams(("parallel",)),
        name="swa",
    )(sinks, q, k2, k2, v2, v2)


def _rope_tables(pos, head_dim):
    half = head_dim // ROPE_FRACTION // 2
    inv = ROPE_THETA ** (-jnp.arange(half, dtype=F32) / half)
    ang = pos.astype(F32)[:, None] * inv[None, :]
    cos, sin = jnp.cos(ang), jnp.sin(ang)
    lp = jnp.arange(LANES) % head_dim
    idx = lp % half
    c = jnp.where(lp < 2 * half, cos[:, idx], 1.0)
    a = jnp.where(lp < half, -sin[:, idx], 0.0)
    b = jnp.where((lp >= half) & (lp < 2 * half), sin[:, idx], 0.0)
    return c, a, b


def _pad_cols(w, mult):
    pad = (-w.shape[-1]) % mult
    return jnp.pad(w, ((0, 0), (0, pad))) if pad else w


def kernel(x, meta_tokens, ffn1_norm, ffn1_w_gate, ffn1_w_up, ffn1_w_down, ffn2_norm, ffn2_w_gate, ffn2_w_up, ffn2_w_down, a_norm, a_w_in, a_q_norm, a_k_norm, a_idx_k_norm, a_w_out, kv_norm, kv_w, kv_k_norm, b_norm, b_w_q, b_q_norm, b_sinks, b_w_out):
    bsz, s_len, d = x.shape
    depth = ffn1_norm.shape[0]
    n_a = a_norm.shape[0]
    topk = min(TOPK_MAX, s_len // 4)
    n_x = bsz * s_len
    assert n_x % TAIL_ROWS == 0 and meta_tokens.shape[0] == N_META

    zeros = lambda n: jnp.zeros((n, d), x.dtype)
    tail_rows = jnp.concatenate([zeros(META_ROW0), meta_tokens.astype(x.dtype), zeros(TAIL_ROWS - LANES)], axis=0)
    h = x.reshape(n_x, d)
    seq_pos = N_META + jnp.arange(s_len, dtype=jnp.int32)
    tail_pos = jnp.clip(jnp.arange(TAIL_ROWS, dtype=jnp.int32) - META_ROW0, 0, N_META - 1)
    tail_pos = jnp.where(jnp.arange(TAIL_ROWS) < LANES, tail_pos, 0)
    pos = jnp.concatenate([jnp.tile(seq_pos, bsz), tail_pos])
    t128 = _rope_tables(pos, A_HEAD_DIM)
    t64 = _rope_tables(pos, B_HEAD_DIM)
    seg = jnp.arange(2 * LANES) // B_HEAD_DIM
    pavg = jnp.where(seg[:, None] == seg[None, :], 1.0 / B_HEAD_DIM, 0.0).astype(BF16)
    tile2 = lambda g: jnp.tile(g.astype(F32), 2)[None, :]

    w1 = tuple(w.astype(BF16) for w in (ffn1_w_gate, ffn1_w_up, ffn1_w_down))
    w2 = tuple(w.astype(BF16) for w in (ffn2_w_gate, ffn2_w_up, ffn2_w_down))

    q_dim, kv_dim, qi_dim = A_HEADS * A_HEAD_DIM, A_KV_HEADS * A_HEAD_DIM, IDX_HEADS * IDX_DIM
    k2 = v2 = None
    for layer in range(depth):
        if layer == n_a:
            k2, v2 = _kvproj(h, kv_norm[None, :], kv_w.astype(BF16), tile2(kv_k_norm), pavg, t64)
        h = _ffn(h, ffn1_norm[layer][None, :], *w1, layer, tail_rows=tail_rows if layer == 0 else None,
                 short_last=True)
        if layer < n_a:
            i = layer
            w = a_w_in[i].astype(BF16)
            o0, o1, o2, o3 = q_dim, q_dim + kv_dim, q_dim + 2 * kv_dim, q_dim + 2 * kv_dim + qi_dim
            wkw = _pad_cols(w[:, o3:], LANES)
            ikn = _pad_cols(a_idx_k_norm[i][None, :], LANES)
            q, k, vt, qi, ki2, wit = _aproj(
                h, a_norm[i][None, :], w[:, :o0], w[:, o0:o1], w[:, o1:o2].T, w[:, o2:o3], wkw,
                a_q_norm[i][None, :], a_k_norm[i][None, :], ikn, t128, t64)
            o = _dsa(qi, wit, q, ki2, k, vt, bsz, s_len, topk)
            h = _oproj(h, o, a_w_out[i].astype(BF16))
        else:
            i = layer - n_a
            q = _bqproj(h, b_norm[i][None, :], b_w_q[i].astype(BF16), tile2(b_q_norm[i]), pavg, t64)
            o = _swa(b_sinks[i].astype(F32), q, k2, v2, bsz, s_len)
            h = _oproj(h, o, b_w_out[i].astype(BF16))
        final = layer == depth - 1
        h = _ffn(h, ffn2_norm[layer][None, :], *w2, layer, rows=n_x if final else None, short_last=not final)
    return h.reshape(bsz, s_len, d)
```

```python
import functools
import math

import jax
import jax.numpy as jnp
from jax import lax
from jax.experimental import pallas as pl
from jax.experimental.pallas import tpu as pltpu

F32 = jnp.float32
BF16 = jnp.bfloat16

LANES = 128
N_META = 16
META_ROW0 = LANES - N_META
TAIL_ROWS = 512
EPS = 1e-6
ROPE_THETA = 500000.0
ROPE_FRACTION = 4
A_HEADS, A_KV_HEADS, A_HEAD_DIM = 16, 4, 128
IDX_HEADS, IDX_DIM = 16, 64
TOPK_MAX = 256
B_HEADS, B_KV_HEADS, B_HEAD_DIM = 32, 4, 64
WINDOW = 128
VMEM_LIMIT = 56 * 1024 * 1024
NEG_INIT = -1e30
N_BISECT = 26
KEY_CHUNKS = 4
VT_ROWS = A_HEAD_DIM + 16


def _cparams(sem):
    return pltpu.CompilerParams(dimension_semantics=sem, vmem_limit_bytes=VMEM_LIMIT)


def _resident(shape):
    nd = len(shape)
    return pl.BlockSpec(shape, lambda *_: (0,) * nd, pipeline_mode=pl.Buffered(1))


def _rms(x, g):
    ms = jnp.mean(x * x, axis=-1, keepdims=True)
    return x * lax.rsqrt(ms + EPS) * g


def _rope(y, c, a, b, half):
    return y * c + pltpu.roll(y, LANES - half, 1) * a + pltpu.roll(y, half, 1) * b


def _nt(a, b):
    return lax.dot_general(a, b, (((1,), (1,)), ((), ())), preferred_element_type=F32)


def _ffn_block(i, f, last):
    return jnp.where(i % 2 == 0, f, last - f)


def _ffn_kernel(tail, short_tile, h_ref, t_ref, g_ref, wg_ref, wu_ref, wd_ref, o_ref, n_ref):
    i = pl.program_id(0)
    f = pl.program_id(1)
    last = pl.num_programs(1) - 1
    tf = wg_ref.shape[-1]
    tm = o_ref.shape[0]

    def hidden(n, width):
        gate = jnp.dot(n, wg_ref[:, :width], preferred_element_type=F32)
        up = jnp.dot(n, wu_ref[:, :width], preferred_element_type=F32)
        act = (0.5 * gate) * jax.nn.sigmoid(gate) * up
        return jnp.dot(act.astype(BF16), wd_ref[:width, :], preferred_element_type=F32)

    def first(src_ref, rows, width):
        x = src_ref[:rows]
        n = _rms(x, g_ref[...]).astype(BF16)
        n_ref[:rows] = n
        o_ref[:rows] = x + hidden(n, width)
        if rows < tm:
            o_ref[rows:] = jnp.zeros((tm - rows, o_ref.shape[1]), F32)

    def step(rows, width):
        o_ref[:rows] += hidden(n_ref[:rows], width)

    partial = _ffn_block(i, f, last) == last

    def row_tile(on, src_ref, rows):
        if tail == tf:
            pl.when(on & (f == 0))(lambda: first(src_ref, rows, tf))
            pl.when(on & (f > 0))(lambda: step(rows, tf))
        else:
            pl.when(on & (f == 0) & ~partial)(lambda: first(src_ref, rows, tf))
            pl.when(on & (f == 0) & partial)(lambda: first(src_ref, rows, tail))
            pl.when(on & (f > 0) & ~partial)(lambda: step(rows, tf))
            pl.when(on & (f > 0) & partial)(lambda: step(rows, tail))

    if short_tile is None:
        row_tile(i >= 0, h_ref, tm)
    else:
        row_tile(i != short_tile, h_ref, tm)
        row_tile(i == short_tile, h_ref if t_ref is None else t_ref, LANES)


def _ffn_kernel_1src(tail, short_tile, h_ref, *rest):
    _ffn_kernel(tail, short_tile, h_ref, None, *rest)


def _ffn(h, g, wg, wu, wd, layer, rows=None, tail_rows=None, short_last=False, tm=512, tf=1024):
    m, d = h.shape
    m = m if rows is None else rows
    assert m % tm == 0, (m, tm)
    n_main = m // tm
    n_tile = n_main
    ff = wg.shape[-1]
    nf = pl.cdiv(ff, tf)
    tail = ff - (nf - 1) * tf
    assert tail % LANES == 0 and nf >= 2, ff
    in_specs = [pl.BlockSpec((tm, d), lambda i, f: (jnp.minimum(i, n_main - 1), 0))]
    args = [h]
    kern = _ffn_kernel_1src
    if tail_rows is not None:
        assert tail_rows.shape == (tm, d)
        n_tile += 1
        in_specs.append(pl.BlockSpec((tm, d), lambda i, f: (0, 0)))
        args.append(tail_rows)
        kern = _ffn_kernel
    assert short_last or tail_rows is None
    return pl.pallas_call(
        functools.partial(kern, tail, n_tile - 1 if short_last else None),
        grid=(n_tile, nf),
        in_specs=in_specs + [
            pl.BlockSpec((1, d), lambda i, f: (0, 0)),
            pl.BlockSpec((None, d, tf), lambda i, f: (layer, 0, _ffn_block(i, f, nf - 1))),
            pl.BlockSpec((None, d, tf), lambda i, f: (layer, 0, _ffn_block(i, f, nf - 1))),
            pl.BlockSpec((None, tf, d), lambda i, f: (layer, _ffn_block(i, f, nf - 1), 0)),
        ],
        out_specs=pl.BlockSpec((tm, d), lambda i, f: (i, 0)),
        out_shape=jax.ShapeDtypeStruct((n_tile * tm, d), F32),
        scratch_shapes=[pltpu.VMEM((tm, d), BF16)],
        compiler_params=_cparams(("parallel", "arbitrary")),
        name="ffn",
    )(*args, g, wg, wu, wd)


def _oproj_kernel(h_ref, o_ref, w_ref, out_ref):
    out_ref[...] = h_ref[...] + jnp.dot(o_ref[...], w_ref[...], preferred_element_type=F32)


def _oproj(h, o, w, tm=512):
    m, d = h.shape
    assert m % tm == 0, (m, tm)
    return pl.pallas_call(
        _oproj_kernel,
        grid=(m // tm,),
        in_specs=[
            pl.BlockSpec((tm, d), lambda i: (i, 0)),
            pl.BlockSpec((tm, o.shape[1]), lambda i: (i, 0)),
            _resident(w.shape),
        ],
        out_specs=pl.BlockSpec((tm, d), lambda i: (i, 0)),
        out_shape=jax.ShapeDtypeStruct((m, d), F32),
        compiler_params=_cparams(("parallel",)),
        name="oproj",
    )(h, o, w)


def _aproj_kernel(h_ref, g_ref, wq_ref, wk_ref, wvt_ref, wqi_ref, wkw_ref, qn_ref, kn_ref, ikn_ref,
                  c128_ref, a128_ref, b128_ref, c64_ref, a64_ref, b64_ref,
                  q_ref, k_ref, vt_ref, qi_ref, ki2_ref, wit_ref):
    tm = h_ref.shape[0]
    hn = _rms(h_ref[...], g_ref[...]).astype(BF16)
    c128, a128, b128 = c128_ref[...], a128_ref[...], b128_ref[...]
    c64, a64, b64 = c64_ref[...], a64_ref[...], b64_ref[...]
    half128 = A_HEAD_DIM // ROPE_FRACTION // 2
    half64 = IDX_DIM // ROPE_FRACTION // 2

    qf = jnp.dot(hn, wq_ref[...], preferred_element_type=F32)
    for hd in range(A_HEADS):
        sl = slice(hd * LANES, (hd + 1) * LANES)
        q_ref[:, sl] = _rope(_rms(qf[:, sl], qn_ref[...]), c128, a128, b128, half128).astype(BF16)

    kf = jnp.dot(hn, wk_ref[...], preferred_element_type=F32)
    for hd in range(A_KV_HEADS):
        sl = slice(hd * LANES, (hd + 1) * LANES)
        k_ref[:, sl] = _rope(_rms(kf[:, sl], kn_ref[...]), c128, a128, b128, half128).astype(BF16)

    vt = _nt(wvt_ref[...], hn)
    tail = jnp.where(lax.broadcasted_iota(jnp.int32, (VT_ROWS - A_HEAD_DIM, LANES), 0) == 0, 1.0, 0.0).astype(BF16)
    for c in range(tm // LANES):
        for n in range(A_KV_HEADS):
            vt_ref[c, n * VT_ROWS:n * VT_ROWS + A_HEAD_DIM, :] = (
                vt[n * A_HEAD_DIM:(n + 1) * A_HEAD_DIM, c * LANES:(c + 1) * LANES].astype(BF16))
            vt_ref[c, n * VT_ROWS + A_HEAD_DIM:(n + 1) * VT_ROWS, :] = tail

    qif = jnp.dot(hn, wqi_ref[...], preferred_element_type=F32)
    for s in range(IDX_HEADS * IDX_DIM // LANES):
        sl = slice(s * LANES, (s + 1) * LANES)
        qi_ref[:, sl] = _rope(qif[:, sl], c64, a64, b64, half64).astype(BF16)

    kw = jnp.dot(hn, wkw_ref[...], preferred_element_type=F32)
    lane = lax.broadcasted_iota(jnp.int32, kw.shape, 1)
    ms = jnp.sum(jnp.where(lane < IDX_DIM, kw * kw, 0.0), axis=-1, keepdims=True) * (1.0 / IDX_DIM)
    ki = _rope(kw * lax.rsqrt(ms + EPS) * ikn_ref[...], c64, a64, b64, half64)
    ki2_ref[:, :LANES] = ki.astype(BF16)
    ki2_ref[:, LANES:] = pltpu.roll(ki, IDX_DIM, 1).astype(BF16)
    wit_ref[...] = kw.T[IDX_DIM:IDX_DIM + IDX_HEADS, :] * ((IDX_HEADS * IDX_DIM) ** -0.5)


def _tab_spec(tm, s_len, n_x):
    per, n_main = s_len // tm, n_x // tm
    assert s_len % tm == 0 and n_x % s_len == 0
    return pl.BlockSpec((tm, LANES), lambda i: (jnp.where(i < n_main, i % per, per + i - n_main), 0))


def _aproj(h, g, wq, wk, wvt, wqi, wkw, qn, kn, ikn, t128, t64, s_len, tm=256):
    m, d = h.shape
    assert m % tm == 0, (m, tm)
    row = lambda w: pl.BlockSpec((tm, w), lambda i: (i, 0))
    tab = _tab_spec(tm, s_len, m - TAIL_ROWS)
    vec = lambda w: pl.BlockSpec((1, w), lambda i: (0, 0))
    kvd = wk.shape[1]
    return pl.pallas_call(
        _aproj_kernel,
        grid=(m // tm,),
        in_specs=[row(d), vec(d), _resident(wq.shape), _resident(wk.shape), _resident(wvt.shape),
                  _resident(wqi.shape), _resident(wkw.shape), vec(LANES), vec(LANES), vec(LANES),
                  tab, tab, tab, tab, tab, tab],
        out_specs=[row(wq.shape[1]), row(kvd),
                   pl.BlockSpec((tm // LANES, A_KV_HEADS * VT_ROWS, LANES), lambda i: (i, 0, 0)),
                   row(wqi.shape[1]), row(2 * LANES),
                   pl.BlockSpec((IDX_HEADS, tm), lambda i: (0, i))],
        out_shape=[jax.ShapeDtypeStruct((m, wq.shape[1]), BF16),
                   jax.ShapeDtypeStruct((m, kvd), BF16),
                   jax.ShapeDtypeStruct((m // LANES, A_KV_HEADS * VT_ROWS, LANES), BF16),
                   jax.ShapeDtypeStruct((m, wqi.shape[1]), BF16),
                   jax.ShapeDtypeStruct((m, 2 * LANES), BF16),
                   jax.ShapeDtypeStruct((IDX_HEADS, m), F32)],
        compiler_params=_cparams(("parallel",)),
        name="aproj",
    )(h, g, wq, wk, wvt, wqi, wkw, qn, kn, ikn, *t128, *t64)


def _dsa_kernel(topk, n_xblk, nq, qi_ref, wit_ref, q_ref, kix_ref, kim_ref, kx_ref, km_ref, vtx_ref, vtm_ref,
                o_ref, score_ref, acc_ref):
    qb = q_ref.shape[0]
    ck = KEY_CHUNKS * LANES
    s_len = kx_ref.shape[0]
    g = pl.program_id(0)
    is_tail = g >= n_xblk
    j = lax.rem(g, nq)
    npair = jnp.where(is_tail, 0, (j + KEY_CHUNKS) // KEY_CHUNKS)
    kf = jnp.float32(topk)
    group = A_HEADS // A_KV_HEADS
    n_slab = IDX_HEADS * IDX_DIM // LANES
    n_pos = s_len + N_META

    lane = lax.broadcasted_iota(jnp.int32, (1, qb), 1)
    tail_pos = jnp.where(g == n_xblk, jnp.maximum(lane - META_ROW0, 0), 0)
    qpos = jnp.where(is_tail, tail_pos, N_META + j * qb + lane)
    kpos_m = lax.broadcasted_iota(jnp.int32, (N_META, qb), 0)
    meta_rows = slice(META_ROW0, LANES)
    kpos_x = lax.broadcasted_iota(jnp.int32, (ck, qb), 0) + N_META

    def pair_row(i):
        return pl.multiple_of(i * ck, ck)

    qi = qi_ref[...]
    qs = jnp.concatenate([qi[:, s * LANES:(s + 1) * LANES] for s in range(n_slab)], axis=0)

    def scores(kc, kpos):
        n = kc.shape[0]
        kcat = jnp.concatenate([kc[:, :LANES], kc[:, LANES:]], axis=0)
        s2 = _nt(kcat, qs)
        acc = jnp.zeros((n, qb), F32)
        for s in range(n_slab):
            for par in range(2):
                blk = s2[par * n:(par + 1) * n, s * qb:(s + 1) * qb]
                acc = acc + jnp.maximum(blk, 0.0) * wit_ref[2 * s + par:2 * s + par + 1, :]
        vis = kpos <= qpos
        lo = jnp.min(jnp.where(vis, acc, jnp.inf), axis=0, keepdims=True)
        hi = jnp.max(jnp.where(vis, acc, -jnp.inf), axis=0, keepdims=True)
        return jnp.where(vis, acc, -jnp.inf), lo, hi

    sc, mn, mx = scores(kim_ref[meta_rows, :], kpos_m)
    score_ref[pl.ds(s_len, N_META), :] = sc

    def score_pair(i, carry):
        mn, mx = carry
        r0 = pair_row(i)
        sc, lo, hi = scores(kix_ref[pl.ds(r0, ck), :], kpos_x + i * ck)
        score_ref[pl.ds(r0, ck), :] = sc
        return jnp.minimum(mn, lo), jnp.maximum(mx, hi)

    mn, mx = lax.fori_loop(0, npair, score_pair, (mn, mx))

    n_part = 8

    def reduce_keys(fn, init, combine):
        def part(r0, n, kpos):
            return fn(score_ref[pl.ds(r0, n), :], kpos.astype(F32))
        acc = jnp.concatenate([part(s_len, N_META, kpos_m), jnp.full((n_part * 8 - N_META, qb), init, F32)], axis=0)
        return lax.fori_loop(
            0, npair,
            lambda i, a: combine(a, part(pair_row(i), ck, kpos_x + i * ck).reshape(ck // (8 * n_part), n_part * 8, qb)),
            acc)

    def count(pred):
        acc = reduce_keys(lambda blk, kidx: jnp.where(pred(blk, kidx), 1.0, 0.0), 0.0,
                          lambda a, x: a + jnp.sum(x, axis=0))
        return jnp.sum(acc, axis=0, keepdims=True)

    def min_where(pred):
        acc = reduce_keys(lambda blk, kidx: jnp.where(pred(blk, kidx), blk, jnp.inf), jnp.inf,
                          lambda a, x: jnp.minimum(a, jnp.min(x, axis=0)))
        return jnp.min(acc, axis=0, keepdims=True)

    def bisect(_, lohi):
        lo, hi = lohi
        mid = 0.5 * lo + 0.5 * hi
        ok = count(lambda blk, kidx: blk >= mid) >= kf
        return jnp.where(ok, mid, lo), jnp.where(ok, hi, mid)

    lo, _ = lax.fori_loop(0, N_BISECT, bisect, (mn, mx))
    c_lo = count(lambda blk, kidx: blk >= lo)
    inexact = jnp.max(jnp.where(c_lo > kf, 1.0, 0.0)) > 0.0

    def exact_threshold(lo):
        def step(state):
            lo, _ = state
            m = min_where(lambda blk, kidx: blk >= lo)
            adv = count(lambda blk, kidx: blk > m) >= kf
            nxt = min_where(lambda blk, kidx: blk > m)
            return jnp.where(adv, nxt, lo), jnp.max(jnp.where(adv, 1.0, 0.0))

        lo, _ = lax.while_loop(lambda st: st[1] > 0.0, step, (lo, jnp.float32(1.0)))
        m = min_where(lambda blk, kidx: blk >= lo)
        r = kf - count(lambda blk, kidx: blk > m)
        tied = count(lambda blk, kidx: blk == m) > r

        def idx_bisect(_, ab):
            a, b = ab
            mid = jnp.floor(0.5 * (a + b))
            ok = count(lambda blk, kidx: (blk == m) & (kidx <= mid)) >= r
            return jnp.where(ok, a, mid), jnp.where(ok, mid, b)

        _, b = lax.fori_loop(0, int(math.ceil(math.log2(n_pos))) + 1, idx_bisect,
                             (jnp.full((1, qb), -1.0, F32), jnp.full((1, qb), float(n_pos - 1), F32)))
        return m, jnp.where(tied, b, float(n_pos))

    thr, cidx = lax.cond(inexact, exact_threshold,
                         lambda lo: (lo, jnp.full((1, qb), float(n_pos), F32)), lo)

    scale2 = A_HEAD_DIM ** -0.5 * math.log2(math.e)
    qg = [jnp.concatenate([q_ref[:, (n * group + h) * LANES:(n * group + h + 1) * LANES] for h in range(group)], axis=0)
          for n in range(A_KV_HEADS)]
    acc_ref[...] = jnp.zeros_like(acc_ref)

    def logits(kcs):
        return tuple(_nt(kcs[n], qg[n]) for n in range(A_KV_HEADS))

    def attend(ms, sts, blk, kpos, vtcs):
        sel = (blk > thr) | ((blk == thr) & (kpos.astype(F32) <= cidx))
        bias = jnp.where(sel, 0.0, -jnp.inf)
        bias = jnp.concatenate([bias] * group, axis=1)
        new_ms, ps, alphas = [], [], []
        for n in range(A_KV_HEADS):
            st = sts[n] * scale2 + bias
            m_new = jnp.maximum(ms[n], jnp.max(st, axis=0, keepdims=True))
            ps.append(jnp.exp2(st - m_new).astype(BF16))
            alphas.append(jnp.exp2(ms[n] - m_new))
            new_ms.append(m_new)
        for n in range(A_KV_HEADS):
            p = ps[n]
            if p.shape[0] < vtcs[n].shape[1]:
                p = jnp.concatenate([jnp.zeros((vtcs[n].shape[1] - p.shape[0], p.shape[1]), BF16), p], axis=0)
            acc_ref[n] = alphas[n] * acc_ref[n] + jnp.dot(vtcs[n], p, preferred_element_type=F32)
        return tuple(new_ms)

    head = lambda n: slice(n * LANES, (n + 1) * LANES)
    vrow = lambda n: slice(n * VT_ROWS, (n + 1) * VT_ROWS)
    ms = tuple(jnp.full((1, group * qb), NEG_INIT, F32) for _ in range(A_KV_HEADS))
    ms = attend(ms, logits([km_ref[meta_rows, head(n)] for n in range(A_KV_HEADS)]),
                score_ref[pl.ds(s_len, N_META), :], kpos_m,
                [vtm_ref[0, vrow(n), :] for n in range(A_KV_HEADS)])

    def attend_pair(i, ms):
        r0 = pair_row(i)
        return attend(ms, logits([kx_ref[pl.ds(r0, ck), head(n)] for n in range(A_KV_HEADS)]),
                      score_ref[pl.ds(r0, ck), :], kpos_x + i * ck,
                      [jnp.concatenate([vtx_ref[KEY_CHUNKS * i + c, vrow(n), :] for c in range(KEY_CHUNKS)], axis=1)
                       for n in range(A_KV_HEADS)])

    lax.fori_loop(0, npair, attend_pair, ms)

    for n in range(A_KV_HEADS):
        a = acc_ref[n]
        ot = a[:A_HEAD_DIM] / a[A_HEAD_DIM:A_HEAD_DIM + 1]
        for h in range(group):
            hd = n * group + h
            o_ref[:, hd * LANES:(hd + 1) * LANES] = ot[:, h * qb:(h + 1) * qb].T.astype(BF16)


def _dsa(qi, wit, q, ki2, k, vt, bsz, s_len, topk):
    qb = LANES
    assert s_len % (KEY_CHUNKS * qb) == 0, s_len
    nq = s_len // qb
    n_xblk = bsz * nq
    m = q.shape[0]
    n_blk = m // qb
    kvd = k.shape[1]
    group = A_HEADS // A_KV_HEADS
    qrow = lambda w: pl.BlockSpec((qb, w), lambda g: (g, 0))
    batch = lambda g: jnp.minimum(g // nq, bsz - 1)
    return pl.pallas_call(
        functools.partial(_dsa_kernel, topk, n_xblk, nq),
        grid=(n_blk,),
        in_specs=[qrow(qi.shape[1]),
                  pl.BlockSpec((IDX_HEADS, qb), lambda g: (0, g)),
                  qrow(q.shape[1]),
                  pl.BlockSpec((s_len, 2 * LANES), lambda g: (batch(g), 0)),
                  pl.BlockSpec((qb, 2 * LANES), lambda g: (n_xblk, 0)),
                  pl.BlockSpec((s_len, kvd), lambda g: (batch(g), 0)),
                  pl.BlockSpec((qb, kvd), lambda g: (n_xblk, 0)),
                  pl.BlockSpec((nq, vt.shape[1], LANES), lambda g: (batch(g), 0, 0)),
                  pl.BlockSpec((1, vt.shape[1], LANES), lambda g: (n_xblk, 0, 0))],
        out_specs=qrow(q.shape[1]),
        out_shape=jax.ShapeDtypeStruct((m, q.shape[1]), BF16),
        scratch_shapes=[pltpu.VMEM((s_len + qb, qb), F32),
                        pltpu.VMEM((A_KV_HEADS, VT_ROWS, group * qb), F32)],
        compiler_params=_cparams(("arbitrary",)),
        name="dsa",
    )(qi, wit, q, ki2, ki2, k, k, vt, vt)


def _rms64_tiles(x, g, pavg):
    x2 = x * x
    hi = x2.astype(BF16)
    lo = (x2 - hi.astype(F32)).astype(BF16)
    ms = jnp.dot(hi, pavg, preferred_element_type=F32) + jnp.dot(lo, pavg, preferred_element_type=F32)
    y = x * lax.rsqrt(ms + EPS)
    return [y[:, t * LANES:(t + 1) * LANES] * g for t in range(2)]


def _split_halves(x, par):
    lane = lax.broadcasted_iota(jnp.int32, x.shape, 1)
    keep = jnp.where(lane < B_HEAD_DIM if par == 0 else lane >= B_HEAD_DIM, x, 0.0)
    moved = pltpu.roll(keep, B_HEAD_DIM, 1)
    return (keep, moved) if par == 0 else (moved, keep)


def _kvproj_kernel(h_ref, g_ref, w_ref, kn_ref, pavg_ref, c64_ref, a64_ref, b64_ref, k2_ref, v2_ref):
    hn = _rms(h_ref[...], g_ref[...]).astype(BF16)
    kv = jnp.dot(hn, w_ref[...], preferred_element_type=F32)
    half = B_HEAD_DIM // ROPE_FRACTION // 2
    kvd = B_KV_HEADS * B_HEAD_DIM
    assert kvd == 2 * LANES
    k_tiles = _rms64_tiles(kv[:, :kvd], kn_ref[...], pavg_ref[...])
    for s in range(kvd // LANES):
        ks = _rope(k_tiles[s], c64_ref[...], a64_ref[...], b64_ref[...], half)
        vs = kv[:, kvd + s * LANES:kvd + (s + 1) * LANES]
        for par in range(2):
            n = 2 * s + par
            for src, dst in ((ks, k2_ref), (vs, v2_ref)):
                low, high = _split_halves(src, par)
                dst[:, (2 * n) * LANES:(2 * n + 1) * LANES] = low.astype(BF16)
                dst[:, (2 * n + 1) * LANES:(2 * n + 2) * LANES] = high.astype(BF16)


def _kvproj(h, g, w, kn, pavg, t64, s_len, tm=512):
    m, d = h.shape
    assert m % tm == 0, (m, tm)
    row = lambda w_: pl.BlockSpec((tm, w_), lambda i: (i, 0))
    vec = lambda w_: pl.BlockSpec((1, w_), lambda i: (0, 0))
    tab = _tab_spec(tm, s_len, m - TAIL_ROWS)
    wide = B_KV_HEADS * 2 * LANES
    return pl.pallas_call(
        _kvproj_kernel,
        grid=(m // tm,),
        in_specs=[row(d), vec(d), _resident(w.shape), vec(LANES), _resident(pavg.shape), tab, tab, tab],
        out_specs=[row(wide), row(wide)],
        out_shape=[jax.ShapeDtypeStruct((m, wide), BF16)] * 2,
        compiler_params=_cparams(("parallel",)),
        name="kvproj",
    )(h, g, w, kn, pavg, *t64)


def _bqproj_kernel(h_ref, g_ref, w_ref, qn_ref, pavg_ref, c64_ref, a64_ref, b64_ref, q_ref):
    hn = _rms(h_ref[...], g_ref[...]).astype(BF16)
    qf = jnp.dot(hn, w_ref[...], preferred_element_type=F32)
    half = B_HEAD_DIM // ROPE_FRACTION // 2
    scale = B_HEAD_DIM ** -0.5
    for s in range(B_HEADS * B_HEAD_DIM // (2 * LANES)):
        tiles = _rms64_tiles(qf[:, 2 * s * LANES:(2 * s + 2) * LANES], qn_ref[...], pavg_ref[...])
        for t in range(2):
            sl = slice((2 * s + t) * LANES, (2 * s + t + 1) * LANES)
            y = _rope(tiles[t], c64_ref[...], a64_ref[...], b64_ref[...], half)
            q_ref[:, sl] = (y * scale).astype(BF16)


def _bqproj(h, g, w, qn, pavg, t64, s_len, tm=256):
    m, d = h.shape
    assert m % tm == 0, (m, tm)
    row = lambda w_: pl.BlockSpec((tm, w_), lambda i: (i, 0))
    vec = lambda w_: pl.BlockSpec((1, w_), lambda i: (0, 0))
    tab = _tab_spec(tm, s_len, m - TAIL_ROWS)
    return pl.pallas_call(
        _bqproj_kernel,
        grid=(m // tm,),
        in_specs=[row(d), vec(d), _resident(w.shape), vec(LANES), _resident(pavg.shape), tab, tab, tab],
        out_specs=row(w.shape[1]),
        out_shape=jax.ShapeDtypeStruct((m, w.shape[1]), BF16),
        compiler_params=_cparams(("parallel",)),
        name="bqproj",
    )(h, g, w, qn, pavg, *t64)


def _swa_kernel(n_xblk, nq, sink_ref, q_ref, kp_ref, kc_ref, vp_ref, vc_ref, o_ref):
    qb = q_ref.shape[0]
    g = pl.program_id(0)
    per_kv = B_HEADS // B_KV_HEADS
    n_tile = per_kv // 2
    r = lax.broadcasted_iota(jnp.int32, (qb, 2 * qb), 0)
    col = lax.broadcasted_iota(jnp.int32, (qb, 2 * qb), 1)
    rel = qb + r - col
    first_x = jnp.where(lax.rem(g, nq) == 0, META_ROW0, 0)
    first_tail = jnp.where(g == n_xblk, qb + META_ROW0, qb)
    first_col = jnp.where(g >= n_xblk, first_tail, first_x)
    mask = (rel >= 0) & (rel < WINDOW) & (col >= first_col)
    for n in range(B_KV_HEADS):
        lo = slice(2 * n * LANES, (2 * n + 1) * LANES)
        hi = slice((2 * n + 1) * LANES, (2 * n + 2) * LANES)
        kcat = jnp.concatenate([kp_ref[:, lo], kc_ref[:, lo], kp_ref[:, hi], kc_ref[:, hi]], axis=0)
        vcat = jnp.concatenate([vp_ref[:, lo], vc_ref[:, lo], vp_ref[:, hi], vc_ref[:, hi]], axis=0)
        qst = jnp.concatenate([q_ref[:, (n * n_tile + s) * LANES:(n * n_tile + s + 1) * LANES] for s in range(n_tile)], axis=0)
        logits = _nt(qst, kcat)
        rows = []
        for s in range(n_tile):
            halves = []
            for par in range(2):
                sink = sink_ref[n * per_kv + 2 * s + par]
                lg = jnp.where(mask, logits[s * qb:(s + 1) * qb, par * 2 * qb:(par + 1) * 2 * qb], -jnp.inf)
                m = jnp.maximum(jnp.max(lg, axis=-1, keepdims=True), sink)
                e = jnp.exp(lg - m)
                p = e / (jnp.sum(e, axis=-1, keepdims=True) + jnp.exp(sink - m))
                halves.append(p.astype(BF16))
            rows.append(jnp.concatenate(halves, axis=1))
        p2 = jnp.concatenate(rows, axis=0)
        o = jnp.dot(p2, vcat, preferred_element_type=F32)
        for s in range(n_tile):
            t = n * n_tile + s
            o_ref[:, t * LANES:(t + 1) * LANES] = o[s * qb:(s + 1) * qb].astype(BF16)


def _swa(sinks, q, k2, v2, bsz, s_len):
    qb = LANES
    nq = s_len // qb
    n_xblk = bsz * nq
    m, width = q.shape
    wide = k2.shape[1]
    cur = lambda w: pl.BlockSpec((qb, w), lambda g: (g, 0))
    prev = lambda w: pl.BlockSpec(
        (qb, w), lambda g: (jnp.where((g >= n_xblk) | (lax.rem(g, nq) == 0), n_xblk, g - 1), 0))
    return pl.pallas_call(
        functools.partial(_swa_kernel, n_xblk, nq),
        grid=(m // qb,),
        in_specs=[pl.BlockSpec(memory_space=pltpu.SMEM), cur(width), prev(wide), cur(wide), prev(wide), cur(wide)],
        out_specs=cur(width),
        out_shape=jax.ShapeDtypeStruct((m, width), BF16),
        compiler_params=_cparams(("parallel",)),
        name="swa",
    )(sinks, q, k2, k2, v2, v2)


def _rope_tables(pos, head_dim):
    half = head_dim // ROPE_FRACTION // 2
    inv = ROPE_THETA ** (-jnp.arange(half, dtype=F32) / half)
    ang = pos.astype(F32)[:, None] * inv[None, :]
    cos, sin = jnp.cos(ang), jnp.sin(ang)
    lp = jnp.arange(LANES) % head_dim
    idx = lp % half
    c = jnp.where(lp < 2 * half, cos[:, idx], 1.0)
    a = jnp.where(lp < half, -sin[:, idx], 0.0)
    b = jnp.where((lp >= half) & (lp < 2 * half), sin[:, idx], 0.0)
    return c, a, b


def _pad_cols(w, mult):
    pad = (-w.shape[-1]) % mult
    return jnp.pad(w, ((0, 0), (0, pad))) if pad else w


def kernel(x, meta_tokens, ffn1_norm, ffn1_w_gate, ffn1_w_up, ffn1_w_down, ffn2_norm, ffn2_w_gate, ffn2_w_up, ffn2_w_down, a_norm, a_w_in, a_q_norm, a_k_norm, a_idx_k_norm, a_w_out, kv_norm, kv_w, kv_k_norm, b_norm, b_w_q, b_q_norm, b_sinks, b_w_out):
    bsz, s_len, d = x.shape
    depth = ffn1_norm.shape[0]
    n_a = a_norm.shape[0]
    topk = min(TOPK_MAX, s_len // 4)
    n_x = bsz * s_len
    assert n_x % TAIL_ROWS == 0 and meta_tokens.shape[0] == N_META

    zeros = lambda n: jnp.zeros((n, d), x.dtype)
    tail_rows = jnp.concatenate([zeros(META_ROW0), meta_tokens.astype(x.dtype), zeros(TAIL_ROWS - LANES)], axis=0)
    h = x.reshape(n_x, d)
    seq_pos = N_META + jnp.arange(s_len, dtype=jnp.int32)
    tail_pos = jnp.clip(jnp.arange(TAIL_ROWS, dtype=jnp.int32) - META_ROW0, 0, N_META - 1)
    tail_pos = jnp.where(jnp.arange(TAIL_ROWS) < LANES, tail_pos, 0)
    pos = jnp.concatenate([seq_pos, tail_pos])
    t128 = _rope_tables(pos, A_HEAD_DIM)
    t64 = _rope_tables(pos, B_HEAD_DIM)
    seg = jnp.arange(2 * LANES) // B_HEAD_DIM
    pavg = jnp.where(seg[:, None] == seg[None, :], 1.0 / B_HEAD_DIM, 0.0).astype(BF16)
    tile2 = lambda g: jnp.tile(g.astype(F32), 2)[None, :]

    w1 = tuple(w.astype(BF16) for w in (ffn1_w_gate, ffn1_w_up, ffn1_w_down))
    w2 = tuple(w.astype(BF16) for w in (ffn2_w_gate, ffn2_w_up, ffn2_w_down))

    q_dim, kv_dim, qi_dim = A_HEADS * A_HEAD_DIM, A_KV_HEADS * A_HEAD_DIM, IDX_HEADS * IDX_DIM
    k2 = v2 = None
    for layer in range(depth):
        if layer == n_a:
            k2, v2 = _kvproj(h, kv_norm[None, :], kv_w.astype(BF16), tile2(kv_k_norm), pavg, t64, s_len)
        h = _ffn(h, ffn1_norm[layer][None, :], *w1, layer, tail_rows=tail_rows if layer == 0 else None,
                 short_last=True)
        if layer < n_a:
            i = layer
            w = a_w_in[i].astype(BF16)
            o0, o1, o2, o3 = q_dim, q_dim + kv_dim, q_dim + 2 * kv_dim, q_dim + 2 * kv_dim + qi_dim
            wkw = _pad_cols(w[:, o3:], LANES)
            ikn = _pad_cols(a_idx_k_norm[i][None, :], LANES)
            q, k, vt, qi, ki2, wit = _aproj(
                h, a_norm[i][None, :], w[:, :o0], w[:, o0:o1], w[:, o1:o2].T, w[:, o2:o3], wkw,
                a_q_norm[i][None, :], a_k_norm[i][None, :], ikn, t128, t64, s_len)
            o = _dsa(qi, wit, q, ki2, k, vt, bsz, s_len, topk)
            h = _oproj(h, o, a_w_out[i].astype(BF16))
        else:
            i = layer - n_a
            q = _bqproj(h, b_norm[i][None, :], b_w_q[i].astype(BF16), tile2(b_q_norm[i]), pavg, t64, s_len)
            o = _swa(b_sinks[i].astype(F32), q, k2, v2, bsz, s_len)
            h = _oproj(h, o, b_w_out[i].astype(BF16))
        final = layer == depth - 1
        h = _ffn(h, ffn2_norm[layer][None, :], *w2, layer, rows=n_x if final else None, short_last=not final)
    return h.reshape(bsz, s_len, d)
```

```python
import functools
import math

import jax
import jax.numpy as jnp
from jax import lax
from jax.experimental import pallas as pl
from jax.experimental.pallas import tpu as pltpu

F32 = jnp.float32
BF16 = jnp.bfloat16

LANES = 128
N_META = 16
META_ROW0 = LANES - N_META
TAIL_ROWS = 512
EPS = 1e-6
ROPE_THETA = 500000.0
ROPE_FRACTION = 4
A_HEADS, A_KV_HEADS, A_HEAD_DIM = 16, 4, 128
IDX_HEADS, IDX_DIM = 16, 64
TOPK_MAX = 256
B_HEADS, B_KV_HEADS, B_HEAD_DIM = 32, 4, 64
WINDOW = 128
VMEM_LIMIT = 60 * 1024 * 1024
NEG_INIT = -1e30
N_BISECT = 26
KEY_CHUNKS = 4
VT_ROWS = A_HEAD_DIM + 16


def _cparams(sem):
    return pltpu.CompilerParams(dimension_semantics=sem, vmem_limit_bytes=VMEM_LIMIT)


def _resident(shape):
    nd = len(shape)
    return pl.BlockSpec(shape, lambda *_: (0,) * nd, pipeline_mode=pl.Buffered(1))


def _rms(x, g):
    ms = jnp.mean(x * x, axis=-1, keepdims=True)
    return x * lax.rsqrt(ms + EPS) * g


def _rope(y, c, a, b, half):
    return y * c + pltpu.roll(y, LANES - half, 1) * a + pltpu.roll(y, half, 1) * b


def _nt(a, b):
    return lax.dot_general(a, b, (((1,), (1,)), ((), ())), preferred_element_type=F32)


def _ffn_block(i, f, last):
    return jnp.where(i % 2 == 0, f, last - f)


def _cvt_slab(i, f, nf, which, n_slab):
    return jnp.clip(i * nf + f - which * n_slab, 0, n_slab - 1)


def _ffn_kernel(tail, short_tile, has_tail_src, n_slab, *refs):
    refs = list(refs)
    h_ref = refs.pop(0)
    t_ref = refs.pop(0) if has_tail_src else None
    g_ref, wg_ref, wu_ref, wd_ref = refs[:4]
    src = refs[4:7] if n_slab else ()
    rest = refs[7:] if n_slab else refs[4:]
    o_ref = rest[0]
    dst = rest[1:4] if n_slab else ()
    n_ref = rest[-1]
    i = pl.program_id(0)
    f = pl.program_id(1)
    last = pl.num_programs(1) - 1
    tf = wg_ref.shape[-1]
    tm = o_ref.shape[0]

    def hidden(n, width):
        gate = jnp.dot(n, wg_ref[:, :width], preferred_element_type=F32)
        up = jnp.dot(n, wu_ref[:, :width], preferred_element_type=F32)
        act = (0.5 * gate) * jax.nn.sigmoid(gate) * up
        return jnp.dot(act.astype(BF16), wd_ref[:width, :], preferred_element_type=F32)

    def first(src_ref, rows, width):
        x = src_ref[:rows]
        n = _rms(x, g_ref[...]).astype(BF16)
        n_ref[:rows] = n
        o_ref[:rows] = x + hidden(n, width)
        if rows < tm:
            o_ref[rows:] = jnp.zeros((tm - rows, o_ref.shape[1]), F32)

    def step(rows, width):
        o_ref[:rows] += hidden(n_ref[:rows], width)

    partial = _ffn_block(i, f, last) == last

    def row_tile(on, src_ref, rows):
        if tail == tf:
            pl.when(on & (f == 0))(lambda: first(src_ref, rows, tf))
            pl.when(on & (f > 0))(lambda: step(rows, tf))
        else:
            pl.when(on & (f == 0) & ~partial)(lambda: first(src_ref, rows, tf))
            pl.when(on & (f == 0) & partial)(lambda: first(src_ref, rows, tail))
            pl.when(on & (f > 0) & ~partial)(lambda: step(rows, tf))
            pl.when(on & (f > 0) & partial)(lambda: step(rows, tail))

    if short_tile is None:
        row_tile(i >= 0, h_ref, tm)
    else:
        row_tile(i != short_tile, h_ref, tm)
        row_tile(i == short_tile, h_ref if t_ref is None else t_ref, LANES)

    for which in range(len(src)):
        turn = i * (last + 1) + f - which * n_slab

        @pl.when((turn >= 0) & (turn < n_slab))
        def _(which=which):
            dst[which][...] = src[which][...].astype(BF16)


def _ffn(h, g, wg, wu, wd, rows=None, tail_rows=None, short_last=False, convert=None, tm=512, tf=1024):
    m, d = h.shape
    m = m if rows is None else rows
    assert m % tm == 0, (m, tm)
    n_main = m // tm
    n_tile = n_main + (tail_rows is not None)
    ff = wg.shape[-1]
    nf = pl.cdiv(ff, tf)
    tail = ff - (nf - 1) * tf
    assert tail % LANES == 0 and nf >= 2, ff
    assert short_last or tail_rows is None
    blk = lambda i, f: _ffn_block(i, f, nf - 1)
    in_specs = [pl.BlockSpec((tm, d), lambda i, f: (jnp.minimum(i, n_main - 1), 0))]
    args = [h]
    if tail_rows is not None:
        assert tail_rows.shape == (tm, d)
        in_specs.append(_resident((tm, d)))
        args.append(tail_rows)
    in_specs += [pl.BlockSpec((1, d), lambda i, f: (0, 0)),
                 pl.BlockSpec((d, tf), lambda i, f: (0, blk(i, f))),
                 pl.BlockSpec((d, tf), lambda i, f: (0, blk(i, f))),
                 pl.BlockSpec((tf, d), lambda i, f: (blk(i, f), 0))]
    args += [g, wg, wu, wd]
    out_specs = [pl.BlockSpec((tm, d), lambda i, f: (i, 0))]
    out_shape = [jax.ShapeDtypeStruct((n_tile * tm, d), F32)]
    n_slab = 0
    cast_outside = None
    if convert is not None and 3 * (ff // LANES) > n_tile * nf:
        cast_outside = tuple(w[convert[1]].astype(BF16) for w in convert[0])
        convert = None
    if convert is not None:
        (cg, cu, cd), layer = convert
        n_slab = ff // LANES
        assert cg.shape[1:] == (d, ff) and cd.shape[1:] == (ff, d)
        slab = lambda which: (lambda i, f: _cvt_slab(i, f, nf, which, n_slab))
        for which, (w, cols) in enumerate(((cg, True), (cu, True), (cd, False))):
            s = slab(which)
            if cols:
                in_specs.append(pl.BlockSpec((None, d, LANES), lambda i, f, s=s: (layer, 0, s(i, f))))
                out_specs.append(pl.BlockSpec((d, LANES), lambda i, f, s=s: (0, s(i, f))))
                out_shape.append(jax.ShapeDtypeStruct((d, ff), BF16))
            else:
                in_specs.append(pl.BlockSpec((None, LANES, d), lambda i, f, s=s: (layer, s(i, f), 0)))
                out_specs.append(pl.BlockSpec((LANES, d), lambda i, f, s=s: (s(i, f), 0)))
                out_shape.append(jax.ShapeDtypeStruct((ff, d), BF16))
            args.append(w)
    outs = pl.pallas_call(
        functools.partial(_ffn_kernel, tail, n_tile - 1 if short_last else None, tail_rows is not None, n_slab),
        grid=(n_tile, nf),
        in_specs=in_specs,
        out_specs=out_specs,
        out_shape=out_shape,
        scratch_shapes=[pltpu.VMEM((tm, d), BF16)],
        compiler_params=_cparams(("arbitrary", "arbitrary")),
        name="ffn",
    )(*args)
    return outs[0], (tuple(outs[1:]) if convert is not None else cast_outside)


def _oproj_kernel(h_ref, o_ref, w_ref, out_ref):
    out_ref[...] = h_ref[...] + jnp.dot(o_ref[...], w_ref[...], preferred_element_type=F32)


def _oproj(h, o, w, tm=512):
    m, d = h.shape
    assert m % tm == 0, (m, tm)
    return pl.pallas_call(
        _oproj_kernel,
        grid=(m // tm,),
        in_specs=[
            pl.BlockSpec((tm, d), lambda i: (i, 0)),
            pl.BlockSpec((tm, o.shape[1]), lambda i: (i, 0)),
            _resident(w.shape),
        ],
        out_specs=pl.BlockSpec((tm, d), lambda i: (i, 0)),
        out_shape=jax.ShapeDtypeStruct((m, d), F32),
        compiler_params=_cparams(("parallel",)),
        name="oproj",
    )(h, o, w)


def _aproj_kernel(h_ref, g_ref, wq_ref, wk_ref, wvt_ref, wqi_ref, wkw_ref, qn_ref, kn_ref, ikn_ref,
                  c128_ref, a128_ref, b128_ref, c64_ref, a64_ref, b64_ref,
                  q_ref, k_ref, vt_ref, qi_ref, ki2_ref, wit_ref):
    tm = h_ref.shape[0]
    hn = _rms(h_ref[...], g_ref[...]).astype(BF16)
    c128, a128, b128 = c128_ref[...], a128_ref[...], b128_ref[...]
    c64, a64, b64 = c64_ref[...], a64_ref[...], b64_ref[...]
    half128 = A_HEAD_DIM // ROPE_FRACTION // 2
    half64 = IDX_DIM // ROPE_FRACTION // 2

    qf = jnp.dot(hn, wq_ref[...], preferred_element_type=F32)
    for hd in range(A_HEADS):
        sl = slice(hd * LANES, (hd + 1) * LANES)
        q_ref[:, sl] = _rope(_rms(qf[:, sl], qn_ref[...]), c128, a128, b128, half128).astype(BF16)

    kf = jnp.dot(hn, wk_ref[...], preferred_element_type=F32)
    for hd in range(A_KV_HEADS):
        sl = slice(hd * LANES, (hd + 1) * LANES)
        k_ref[:, sl] = _rope(_rms(kf[:, sl], kn_ref[...]), c128, a128, b128, half128).astype(BF16)

    vt = _nt(wvt_ref[...], hn)
    tail = jnp.where(lax.broadcasted_iota(jnp.int32, (VT_ROWS - A_HEAD_DIM, LANES), 0) == 0, 1.0, 0.0).astype(BF16)
    for c in range(tm // LANES):
        for n in range(A_KV_HEADS):
            vt_ref[c, n * VT_ROWS:n * VT_ROWS + A_HEAD_DIM, :] = (
                vt[n * A_HEAD_DIM:(n + 1) * A_HEAD_DIM, c * LANES:(c + 1) * LANES].astype(BF16))
            vt_ref[c, n * VT_ROWS + A_HEAD_DIM:(n + 1) * VT_ROWS, :] = tail

    qif = jnp.dot(hn, wqi_ref[...], preferred_element_type=F32)
    for s in range(IDX_HEADS * IDX_DIM // LANES):
        sl = slice(s * LANES, (s + 1) * LANES)
        qi_ref[:, sl] = _rope(qif[:, sl], c64, a64, b64, half64).astype(BF16)

    kw = jnp.dot(hn, wkw_ref[...], preferred_element_type=F32)
    lane = lax.broadcasted_iota(jnp.int32, kw.shape, 1)
    ms = jnp.sum(jnp.where(lane < IDX_DIM, kw * kw, 0.0), axis=-1, keepdims=True) * (1.0 / IDX_DIM)
    ki = _rope(kw * lax.rsqrt(ms + EPS) * ikn_ref[...], c64, a64, b64, half64)
    ki2_ref[:, :LANES] = ki.astype(BF16)
    ki2_ref[:, LANES:] = pltpu.roll(ki, IDX_DIM, 1).astype(BF16)
    wit_ref[...] = kw.T[IDX_DIM:IDX_DIM + IDX_HEADS, :] * ((IDX_HEADS * IDX_DIM) ** -0.5)


def _tab_spec(tm, s_len, n_x):
    per, n_main = s_len // tm, n_x // tm
    assert s_len % tm == 0 and n_x % s_len == 0
    return pl.BlockSpec((tm, LANES), lambda i: (jnp.where(i < n_main, i % per, per + i - n_main), 0))


def _aproj(h, g, wq, wk, wvt, wqi, wkw, qn, kn, ikn, t128, t64, s_len, tm=256):
    m, d = h.shape
    assert m % tm == 0, (m, tm)
    row = lambda w: pl.BlockSpec((tm, w), lambda i: (i, 0))
    tab = _tab_spec(tm, s_len, m - TAIL_ROWS)
    vec = lambda w: pl.BlockSpec((1, w), lambda i: (0, 0))
    kvd = wk.shape[1]
    return pl.pallas_call(
        _aproj_kernel,
        grid=(m // tm,),
        in_specs=[row(d), vec(d), _resident(wq.shape), _resident(wk.shape), _resident(wvt.shape),
                  _resident(wqi.shape), _resident(wkw.shape), vec(LANES), vec(LANES), vec(LANES),
                  tab, tab, tab, tab, tab, tab],
        out_specs=[row(wq.shape[1]), row(kvd),
                   pl.BlockSpec((tm // LANES, A_KV_HEADS * VT_ROWS, LANES), lambda i: (i, 0, 0)),
                   row(wqi.shape[1]), row(2 * LANES),
                   pl.BlockSpec((IDX_HEADS, tm), lambda i: (0, i))],
        out_shape=[jax.ShapeDtypeStruct((m, wq.shape[1]), BF16),
                   jax.ShapeDtypeStruct((m, kvd), BF16),
                   jax.ShapeDtypeStruct((m // LANES, A_KV_HEADS * VT_ROWS, LANES), BF16),
                   jax.ShapeDtypeStruct((m, wqi.shape[1]), BF16),
                   jax.ShapeDtypeStruct((m, 2 * LANES), BF16),
                   jax.ShapeDtypeStruct((IDX_HEADS, m), F32)],
        compiler_params=_cparams(("parallel",)),
        name="aproj",
    )(h, g, wq, wk, wvt, wqi, wkw, qn, kn, ikn, *t128, *t64)


def _dsa_kernel(topk, n_xblk, nq, qi_ref, wit_ref, q_ref, kix_ref, kim_ref, kx_ref, km_ref, vtx_ref, vtm_ref,
                o_ref, score_ref, acc_ref):
    qb = q_ref.shape[0]
    ck = KEY_CHUNKS * LANES
    s_len = kx_ref.shape[0]
    g = pl.program_id(0)
    is_tail = g >= n_xblk
    j = lax.rem(g, nq)
    npair = jnp.where(is_tail, 0, (j + KEY_CHUNKS) // KEY_CHUNKS)
    kf = jnp.float32(topk)
    group = A_HEADS // A_KV_HEADS
    n_slab = IDX_HEADS * IDX_DIM // LANES
    n_pos = s_len + N_META

    lane = lax.broadcasted_iota(jnp.int32, (1, qb), 1)
    tail_pos = jnp.where(g == n_xblk, jnp.maximum(lane - META_ROW0, 0), 0)
    qpos = jnp.where(is_tail, tail_pos, N_META + j * qb + lane)
    kpos_m = lax.broadcasted_iota(jnp.int32, (N_META, qb), 0)
    meta_rows = slice(META_ROW0, LANES)
    kpos_x = lax.broadcasted_iota(jnp.int32, (ck, qb), 0) + N_META

    def pair_row(i):
        return pl.multiple_of(i * ck, ck)

    qi = qi_ref[...]
    qs = jnp.concatenate([qi[:, s * LANES:(s + 1) * LANES] for s in range(n_slab)], axis=0)

    def scores(kc, kpos):
        n = kc.shape[0]
        kcat = jnp.concatenate([kc[:, :LANES], kc[:, LANES:]], axis=0)
        s2 = _nt(kcat, qs)
        acc = jnp.zeros((n, qb), F32)
        for s in range(n_slab):
            for par in range(2):
                blk = s2[par * n:(par + 1) * n, s * qb:(s + 1) * qb]
                acc = acc + jnp.maximum(blk, 0.0) * wit_ref[2 * s + par:2 * s + par + 1, :]
        vis = kpos <= qpos
        lo = jnp.min(jnp.where(vis, acc, jnp.inf), axis=0, keepdims=True)
        hi = jnp.max(jnp.where(vis, acc, -jnp.inf), axis=0, keepdims=True)
        return jnp.where(vis, acc, -jnp.inf), lo, hi

    sc, mn, mx = scores(kim_ref[meta_rows, :], kpos_m)
    score_ref[pl.ds(s_len, N_META), :] = sc

    def score_pair(i, carry):
        mn, mx = carry
        r0 = pair_row(i)
        sc, lo, hi = scores(kix_ref[pl.ds(r0, ck), :], kpos_x + i * ck)
        score_ref[pl.ds(r0, ck), :] = sc
        return jnp.minimum(mn, lo), jnp.maximum(mx, hi)

    mn, mx = lax.fori_loop(0, npair, score_pair, (mn, mx))

    n_part = 8

    def reduce_keys(fn, init, combine):
        def part(r0, n, kpos):
            return fn(score_ref[pl.ds(r0, n), :], kpos.astype(F32))
        acc = jnp.concatenate([part(s_len, N_META, kpos_m), jnp.full((n_part * 8 - N_META, qb), init, F32)], axis=0)
        return lax.fori_loop(
            0, npair,
            lambda i, a: combine(a, part(pair_row(i), ck, kpos_x + i * ck).reshape(ck // (8 * n_part), n_part * 8, qb)),
            acc)

    def count(pred):
        acc = reduce_keys(lambda blk, kidx: jnp.where(pred(blk, kidx), 1.0, 0.0), 0.0,
                          lambda a, x: a + jnp.sum(x, axis=0))
        return jnp.sum(acc, axis=0, keepdims=True)

    def min_where(pred):
        acc = reduce_keys(lambda blk, kidx: jnp.where(pred(blk, kidx), blk, jnp.inf), jnp.inf,
                          lambda a, x: jnp.minimum(a, jnp.min(x, axis=0)))
        return jnp.min(acc, axis=0, keepdims=True)

    def bisect(_, lohi):
        lo, hi = lohi
        mid = 0.5 * lo + 0.5 * hi
        ok = count(lambda blk, kidx: blk >= mid) >= kf
        return jnp.where(ok, mid, lo), jnp.where(ok, hi, mid)

    lo, _ = lax.fori_loop(0, N_BISECT, bisect, (mn, mx))
    c_lo = count(lambda blk, kidx: blk >= lo)
    inexact = jnp.max(jnp.where(c_lo > kf, 1.0, 0.0)) > 0.0

    def exact_threshold(lo):
        def step(state):
            lo, _ = state
            m = min_where(lambda blk, kidx: blk >= lo)
            adv = count(lambda blk, kidx: blk > m) >= kf
            nxt = min_where(lambda blk, kidx: blk > m)
            return jnp.where(adv, nxt, lo), jnp.max(jnp.where(adv, 1.0, 0.0))

        lo, _ = lax.while_loop(lambda st: st[1] > 0.0, step, (lo, jnp.float32(1.0)))
        m = min_where(lambda blk, kidx: blk >= lo)
        r = kf - count(lambda blk, kidx: blk > m)
        tied = count(lambda blk, kidx: blk == m) > r

        def idx_bisect(_, ab):
            a, b = ab
            mid = jnp.floor(0.5 * (a + b))
            ok = count(lambda blk, kidx: (blk == m) & (kidx <= mid)) >= r
            return jnp.where(ok, a, mid), jnp.where(ok, mid, b)

        _, b = lax.fori_loop(0, int(math.ceil(math.log2(n_pos))) + 1, idx_bisect,
                             (jnp.full((1, qb), -1.0, F32), jnp.full((1, qb), float(n_pos - 1), F32)))
        return m, jnp.where(tied, b, float(n_pos))

    thr, cidx = lax.cond(inexact, exact_threshold,
                         lambda lo: (lo, jnp.full((1, qb), float(n_pos), F32)), lo)

    scale2 = A_HEAD_DIM ** -0.5 * math.log2(math.e)
    qg = [jnp.concatenate([q_ref[:, (n * group + h) * LANES:(n * group + h + 1) * LANES] for h in range(group)], axis=0)
          for n in range(A_KV_HEADS)]
    acc_ref[...] = jnp.zeros_like(acc_ref)

    def logits(kcs):
        return tuple(_nt(kcs[n], qg[n]) for n in range(A_KV_HEADS))

    def attend(ms, sts, blk, kpos, vtcs):
        sel = (blk > thr) | ((blk == thr) & (kpos.astype(F32) <= cidx))
        bias = jnp.where(sel, 0.0, -jnp.inf)
        bias = jnp.concatenate([bias] * group, axis=1)
        new_ms, ps, alphas = [], [], []
        for n in range(A_KV_HEADS):
            st = sts[n] * scale2 + bias
            m_new = jnp.maximum(ms[n], jnp.max(st, axis=0, keepdims=True))
            ps.append(jnp.exp2(st - m_new).astype(BF16))
            alphas.append(jnp.exp2(ms[n] - m_new))
            new_ms.append(m_new)
        for n in range(A_KV_HEADS):
            p = ps[n]
            if p.shape[0] < vtcs[n].shape[1]:
                p = jnp.concatenate([jnp.zeros((vtcs[n].shape[1] - p.shape[0], p.shape[1]), BF16), p], axis=0)
            acc_ref[n] = alphas[n] * acc_ref[n] + jnp.dot(vtcs[n], p, preferred_element_type=F32)
        return tuple(new_ms)

    head = lambda n: slice(n * LANES, (n + 1) * LANES)
    vrow = lambda n: slice(n * VT_ROWS, (n + 1) * VT_ROWS)
    ms = tuple(jnp.full((1, group * qb), NEG_INIT, F32) for _ in range(A_KV_HEADS))
    ms = attend(ms, logits([km_ref[meta_rows, head(n)] for n in range(A_KV_HEADS)]),
                score_ref[pl.ds(s_len, N_META), :], kpos_m,
                [vtm_ref[0, vrow(n), :] for n in range(A_KV_HEADS)])

    def attend_pair(i, ms):
        r0 = pair_row(i)
        return attend(ms, logits([kx_ref[pl.ds(r0, ck), head(n)] for n in range(A_KV_HEADS)]),
                      score_ref[pl.ds(r0, ck), :], kpos_x + i * ck,
                      [jnp.concatenate([vtx_ref[KEY_CHUNKS * i + c, vrow(n), :] for c in range(KEY_CHUNKS)], axis=1)
                       for n in range(A_KV_HEADS)])

    lax.fori_loop(0, npair, attend_pair, ms)

    for n in range(A_KV_HEADS):
        a = acc_ref[n]
        ot = a[:A_HEAD_DIM] / a[A_HEAD_DIM:A_HEAD_DIM + 1]
        for h in range(group):
            hd = n * group + h
            o_ref[:, hd * LANES:(hd + 1) * LANES] = ot[:, h * qb:(h + 1) * qb].T.astype(BF16)


def _dsa(qi, wit, q, ki2, k, vt, bsz, s_len, topk):
    qb = LANES
    assert s_len % (KEY_CHUNKS * qb) == 0, s_len
    nq = s_len // qb
    n_xblk = bsz * nq
    m = q.shape[0]
    n_blk = m // qb
    kvd = k.shape[1]
    group = A_HEADS // A_KV_HEADS
    qrow = lambda w: pl.BlockSpec((qb, w), lambda g: (g, 0))
    batch = lambda g: jnp.minimum(g // nq, bsz - 1)
    return pl.pallas_call(
        functools.partial(_dsa_kernel, topk, n_xblk, nq),
        grid=(n_blk,),
        in_specs=[qrow(qi.shape[1]),
                  pl.BlockSpec((IDX_HEADS, qb), lambda g: (0, g)),
                  qrow(q.shape[1]),
                  pl.BlockSpec((s_len, 2 * LANES), lambda g: (batch(g), 0)),
                  pl.BlockSpec((qb, 2 * LANES), lambda g: (n_xblk, 0)),
                  pl.BlockSpec((s_len, kvd), lambda g: (batch(g), 0)),
                  pl.BlockSpec((qb, kvd), lambda g: (n_xblk, 0)),
                  pl.BlockSpec((nq, vt.shape[1], LANES), lambda g: (batch(g), 0, 0)),
                  pl.BlockSpec((1, vt.shape[1], LANES), lambda g: (n_xblk, 0, 0))],
        out_specs=qrow(q.shape[1]),
        out_shape=jax.ShapeDtypeStruct((m, q.shape[1]), BF16),
        scratch_shapes=[pltpu.VMEM((s_len + qb, qb), F32),
                        pltpu.VMEM((A_KV_HEADS, VT_ROWS, group * qb), F32)],
        compiler_params=_cparams(("arbitrary",)),
        name="dsa",
    )(qi, wit, q, ki2, ki2, k, k, vt, vt)


def _rms64_tiles(x, g, pavg):
    x2 = x * x
    hi = x2.astype(BF16)
    lo = (x2 - hi.astype(F32)).astype(BF16)
    ms = jnp.dot(hi, pavg, preferred_element_type=F32) + jnp.dot(lo, pavg, preferred_element_type=F32)
    y = x * lax.rsqrt(ms + EPS)
    return [y[:, t * LANES:(t + 1) * LANES] * g for t in range(2)]


def _split_halves(x, par):
    lane = lax.broadcasted_iota(jnp.int32, x.shape, 1)
    keep = jnp.where(lane < B_HEAD_DIM if par == 0 else lane >= B_HEAD_DIM, x, 0.0)
    moved = pltpu.roll(keep, B_HEAD_DIM, 1)
    return (keep, moved) if par == 0 else (moved, keep)


def _kvproj_kernel(h_ref, g_ref, w_ref, kn_ref, pavg_ref, c64_ref, a64_ref, b64_ref, k2_ref, v2_ref):
    hn = _rms(h_ref[...], g_ref[...]).astype(BF16)
    kv = jnp.dot(hn, w_ref[...], preferred_element_type=F32)
    half = B_HEAD_DIM // ROPE_FRACTION // 2
    kvd = B_KV_HEADS * B_HEAD_DIM
    assert kvd == 2 * LANES
    k_tiles = _rms64_tiles(kv[:, :kvd], kn_ref[...], pavg_ref[...])
    for s in range(kvd // LANES):
        ks = _rope(k_tiles[s], c64_ref[...], a64_ref[...], b64_ref[...], half)
        vs = kv[:, kvd + s * LANES:kvd + (s + 1) * LANES]
        for par in range(2):
            n = 2 * s + par
            for src, dst in ((ks, k2_ref), (vs, v2_ref)):
                low, high = _split_halves(src, par)
                dst[:, (2 * n) * LANES:(2 * n + 1) * LANES] = low.astype(BF16)
                dst[:, (2 * n + 1) * LANES:(2 * n + 2) * LANES] = high.astype(BF16)


def _kvproj(h, g, w, kn, pavg, t64, s_len, tm=512):
    m, d = h.shape
    assert m % tm == 0, (m, tm)
    row = lambda w_: pl.BlockSpec((tm, w_), lambda i: (i, 0))
    vec = lambda w_: pl.BlockSpec((1, w_), lambda i: (0, 0))
    tab = _tab_spec(tm, s_len, m - TAIL_ROWS)
    wide = B_KV_HEADS * 2 * LANES
    return pl.pallas_call(
        _kvproj_kernel,
        grid=(m // tm,),
        in_specs=[row(d), vec(d), _resident(w.shape), vec(LANES), _resident(pavg.shape), tab, tab, tab],
        out_specs=[row(wide), row(wide)],
        out_shape=[jax.ShapeDtypeStruct((m, wide), BF16)] * 2,
        compiler_params=_cparams(("parallel",)),
        name="kvproj",
    )(h, g, w, kn, pavg, *t64)


def _bqproj_kernel(h_ref, g_ref, w_ref, qn_ref, pavg_ref, c64_ref, a64_ref, b64_ref, q_ref):
    hn = _rms(h_ref[...], g_ref[...]).astype(BF16)
    qf = jnp.dot(hn, w_ref[...], preferred_element_type=F32)
    half = B_HEAD_DIM // ROPE_FRACTION // 2
    scale = B_HEAD_DIM ** -0.5
    for s in range(B_HEADS * B_HEAD_DIM // (2 * LANES)):
        tiles = _rms64_tiles(qf[:, 2 * s * LANES:(2 * s + 2) * LANES], qn_ref[...], pavg_ref[...])
        for t in range(2):
            sl = slice((2 * s + t) * LANES, (2 * s + t + 1) * LANES)
            y = _rope(tiles[t], c64_ref[...], a64_ref[...], b64_ref[...], half)
            q_ref[:, sl] = (y * scale).astype(BF16)


def _bqproj(h, g, w, qn, pavg, t64, s_len, tm=256):
    m, d = h.shape
    assert m % tm == 0, (m, tm)
    row = lambda w_: pl.BlockSpec((tm, w_), lambda i: (i, 0))
    vec = lambda w_: pl.BlockSpec((1, w_), lambda i: (0, 0))
    tab = _tab_spec(tm, s_len, m - TAIL_ROWS)
    return pl.pallas_call(
        _bqproj_kernel,
        grid=(m // tm,),
        in_specs=[row(d), vec(d), _resident(w.shape), vec(LANES), _resident(pavg.shape), tab, tab, tab],
        out_specs=row(w.shape[1]),
        out_shape=jax.ShapeDtypeStruct((m, w.shape[1]), BF16),
        compiler_params=_cparams(("parallel",)),
        name="bqproj",
    )(h, g, w, qn, pavg, *t64)


def _swa_kernel(n_xblk, nq, sink_ref, q_ref, kp_ref, kc_ref, vp_ref, vc_ref, o_ref):
    qb = q_ref.shape[0]
    g = pl.program_id(0)
    per_kv = B_HEADS // B_KV_HEADS
    n_tile = per_kv // 2
    r = lax.broadcasted_iota(jnp.int32, (qb, 2 * qb), 0)
    col = lax.broadcasted_iota(jnp.int32, (qb, 2 * qb), 1)
    rel = qb + r - col
    first_x = jnp.where(lax.rem(g, nq) == 0, META_ROW0, 0)
    first_tail = jnp.where(g == n_xblk, qb + META_ROW0, qb)
    first_col = jnp.where(g >= n_xblk, first_tail, first_x)
    mask = (rel >= 0) & (rel < WINDOW) & (col >= first_col)
    for n in range(B_KV_HEADS):
        lo = slice(2 * n * LANES, (2 * n + 1) * LANES)
        hi = slice((2 * n + 1) * LANES, (2 * n + 2) * LANES)
        kcat = jnp.concatenate([kp_ref[:, lo], kc_ref[:, lo], kp_ref[:, hi], kc_ref[:, hi]], axis=0)
        vcat = jnp.concatenate([vp_ref[:, lo], vc_ref[:, lo], vp_ref[:, hi], vc_ref[:, hi]], axis=0)
        qst = jnp.concatenate([q_ref[:, (n * n_tile + s) * LANES:(n * n_tile + s + 1) * LANES] for s in range(n_tile)], axis=0)
        logits = _nt(qst, kcat)
        rows = []
        for s in range(n_tile):
            halves = []
            for par in range(2):
                sink = sink_ref[n * per_kv + 2 * s + par]
                lg = jnp.where(mask, logits[s * qb:(s + 1) * qb, par * 2 * qb:(par + 1) * 2 * qb], -jnp.inf)
                m = jnp.maximum(jnp.max(lg, axis=-1, keepdims=True), sink)
                e = jnp.exp(lg - m)
                p = e / (jnp.sum(e, axis=-1, keepdims=True) + jnp.exp(sink - m))
                halves.append(p.astype(BF16))
            rows.append(jnp.concatenate(halves, axis=1))
        p2 = jnp.concatenate(rows, axis=0)
        o = jnp.dot(p2, vcat, preferred_element_type=F32)
        for s in range(n_tile):
            t = n * n_tile + s
            o_ref[:, t * LANES:(t + 1) * LANES] = o[s * qb:(s + 1) * qb].astype(BF16)


def _swa(sinks, q, k2, v2, bsz, s_len):
    qb = LANES
    nq = s_len // qb
    n_xblk = bsz * nq
    m, width = q.shape
    wide = k2.shape[1]
    cur = lambda w: pl.BlockSpec((qb, w), lambda g: (g, 0))
    prev = lambda w: pl.BlockSpec(
        (qb, w), lambda g: (jnp.where((g >= n_xblk) | (lax.rem(g, nq) == 0), n_xblk, g - 1), 0))
    return pl.pallas_call(
        functools.partial(_swa_kernel, n_xblk, nq),
        grid=(m // qb,),
        in_specs=[pl.BlockSpec(memory_space=pltpu.SMEM), cur(width), prev(wide), cur(wide), prev(wide), cur(wide)],
        out_specs=cur(width),
        out_shape=jax.ShapeDtypeStruct((m, width), BF16),
        compiler_params=_cparams(("parallel",)),
        name="swa",
    )(sinks, q, k2, k2, v2, v2)


def _rope_tables(pos, head_dim):
    half = head_dim // ROPE_FRACTION // 2
    inv = ROPE_THETA ** (-jnp.arange(half, dtype=F32) / half)
    ang = pos.astype(F32)[:, None] * inv[None, :]
    cos, sin = jnp.cos(ang), jnp.sin(ang)
    lp = jnp.arange(LANES) % head_dim
    idx = lp % half
    c = jnp.where(lp < 2 * half, cos[:, idx], 1.0)
    a = jnp.where(lp < half, -sin[:, idx], 0.0)
    b = jnp.where((lp >= half) & (lp < 2 * half), sin[:, idx], 0.0)
    return c, a, b


def _pad_cols(w, mult):
    pad = (-w.shape[-1]) % mult
    return jnp.pad(w, ((0, 0), (0, pad))) if pad else w


def kernel(x, meta_tokens, ffn1_norm, ffn1_w_gate, ffn1_w_up, ffn1_w_down, ffn2_norm, ffn2_w_gate, ffn2_w_up, ffn2_w_down, a_norm, a_w_in, a_q_norm, a_k_norm, a_idx_k_norm, a_w_out, kv_norm, kv_w, kv_k_norm, b_norm, b_w_q, b_q_norm, b_sinks, b_w_out):
    bsz, s_len, d = x.shape
    depth = ffn1_norm.shape[0]
    n_a = a_norm.shape[0]
    topk = min(TOPK_MAX, s_len // 4)
    n_x = bsz * s_len
    assert n_x % TAIL_ROWS == 0 and meta_tokens.shape[0] == N_META

    zeros = lambda n: jnp.zeros((n, d), x.dtype)
    tail_rows = jnp.concatenate([zeros(META_ROW0), meta_tokens.astype(x.dtype), zeros(TAIL_ROWS - LANES)], axis=0)
    h = x.reshape(n_x, d)
    seq_pos = N_META + jnp.arange(s_len, dtype=jnp.int32)
    tail_pos = jnp.clip(jnp.arange(TAIL_ROWS, dtype=jnp.int32) - META_ROW0, 0, N_META - 1)
    tail_pos = jnp.where(jnp.arange(TAIL_ROWS) < LANES, tail_pos, 0)
    pos = jnp.concatenate([seq_pos, tail_pos])
    t128 = _rope_tables(pos, A_HEAD_DIM)
    t64 = _rope_tables(pos, B_HEAD_DIM)
    seg = jnp.arange(2 * LANES) // B_HEAD_DIM
    pavg = jnp.where(seg[:, None] == seg[None, :], 1.0 / B_HEAD_DIM, 0.0).astype(BF16)
    tile2 = lambda g: jnp.tile(g.astype(F32), 2)[None, :]

    ffn_w = []
    for layer in range(depth):
        ffn_w += [((ffn1_w_gate, ffn1_w_up, ffn1_w_down), layer), ((ffn2_w_gate, ffn2_w_up, ffn2_w_down), layer)]
    state = {"w": tuple(w[0].astype(BF16) for w in ffn_w[0][0]), "k": 0}

    def ffn(h, g, **kw):
        k = state["k"]
        nxt = ffn_w[k + 1] if k + 1 < len(ffn_w) else None
        h, w_next = _ffn(h, g[None, :], *state["w"], convert=nxt, **kw)
        state["w"], state["k"] = w_next, k + 1
        return h

    q_dim, kv_dim, qi_dim = A_HEADS * A_HEAD_DIM, A_KV_HEADS * A_HEAD_DIM, IDX_HEADS * IDX_DIM
    k2 = v2 = None
    for layer in range(depth):
        if layer == n_a:
            k2, v2 = _kvproj(h, kv_norm[None, :], kv_w.astype(BF16), tile2(kv_k_norm), pavg, t64, s_len)
        h = ffn(h, ffn1_norm[layer], tail_rows=tail_rows if layer == 0 else None, short_last=True)
        if layer < n_a:
            i = layer
            w = a_w_in[i].astype(BF16)
            o0, o1, o2, o3 = q_dim, q_dim + kv_dim, q_dim + 2 * kv_dim, q_dim + 2 * kv_dim + qi_dim
            wkw = _pad_cols(w[:, o3:], LANES)
            ikn = _pad_cols(a_idx_k_norm[i][None, :], LANES)
            q, k, vt, qi, ki2, wit = _aproj(
                h, a_norm[i][None, :], w[:, :o0], w[:, o0:o1], w[:, o1:o2].T, w[:, o2:o3], wkw,
                a_q_norm[i][None, :], a_k_norm[i][None, :], ikn, t128, t64, s_len)
            o = _dsa(qi, wit, q, ki2, k, vt, bsz, s_len, topk)
            h = _oproj(h, o, a_w_out[i].astype(BF16))
        else:
            i = layer - n_a
            q = _bqproj(h, b_norm[i][None, :], b_w_q[i].astype(BF16), tile2(b_q_norm[i]), pavg, t64, s_len)
            o = _swa(b_sinks[i].astype(F32), q, k2, v2, bsz, s_len)
            h = _oproj(h, o, b_w_out[i].astype(BF16))
        final = layer == depth - 1
        h = ffn(h, ffn2_norm[layer], rows=n_x if final else None, short_last=not final)
    return h.reshape(bsz, s_len, d)
```

```python
import functools
import math

import jax
import jax.numpy as jnp
from jax import lax
from jax.experimental import pallas as pl
from jax.experimental.pallas import tpu as pltpu

F32 = jnp.float32
BF16 = jnp.bfloat16

LANES = 128
N_META = 16
META_ROW0 = LANES - N_META
TAIL_ROWS = 512
EPS = 1e-6
ROPE_THETA = 500000.0
ROPE_FRACTION = 4
A_HEADS, A_KV_HEADS, A_HEAD_DIM = 16, 4, 128
IDX_HEADS, IDX_DIM = 16, 64
TOPK_MAX = 256
B_HEADS, B_KV_HEADS, B_HEAD_DIM = 32, 4, 64
WINDOW = 128
VMEM_LIMIT = 60 * 1024 * 1024
NEG_INIT = -1e30
N_BISECT = 26
CVT_ROWS_WIDE = 32
KEY_CHUNKS = 4
VT_ROWS = A_HEAD_DIM + 16


def _cparams(sem):
    return pltpu.CompilerParams(dimension_semantics=sem, vmem_limit_bytes=VMEM_LIMIT)


def _resident(shape):
    nd = len(shape)
    return pl.BlockSpec(shape, lambda *_: (0,) * nd, pipeline_mode=pl.Buffered(1))


def _rms(x, g):
    ms = jnp.mean(x * x, axis=-1, keepdims=True)
    return x * lax.rsqrt(ms + EPS) * g


def _rope(y, c, a, b, half):
    return y * c + pltpu.roll(y, LANES - half, 1) * a + pltpu.roll(y, half, 1) * b


def _nt(a, b):
    return lax.dot_general(a, b, (((1,), (1,)), ((), ())), preferred_element_type=F32)


def _ffn_block(i, f, last):
    return jnp.where(i % 2 == 0, f, last - f)


def _cvt_slab(i, f, nf, start, count):
    return jnp.clip(i * nf + f - start, 0, count - 1)


def _ffn_kernel(tail, short_tile, has_tail_src, turns, *refs):
    refs = list(refs)
    h_ref = refs.pop(0)
    t_ref = refs.pop(0) if has_tail_src else None
    g_ref, wg_ref, wu_ref, wd_ref = refs[:4]
    n_cvt = len(turns)
    src = refs[4:4 + n_cvt]
    rest = refs[4 + n_cvt:]
    o_ref = rest[0]
    dst = rest[1:1 + n_cvt]
    n_ref = rest[-1]
    i = pl.program_id(0)
    f = pl.program_id(1)
    last = pl.num_programs(1) - 1
    tf = wg_ref.shape[-1]
    tm = o_ref.shape[0]

    def hidden(n, width):
        gate = jnp.dot(n, wg_ref[:, :width], preferred_element_type=F32)
        up = jnp.dot(n, wu_ref[:, :width], preferred_element_type=F32)
        act = (0.5 * gate) * jax.nn.sigmoid(gate) * up
        return jnp.dot(act.astype(BF16), wd_ref[:width, :], preferred_element_type=F32)

    def first(src_ref, rows, width):
        x = src_ref[:rows]
        n = _rms(x, g_ref[...]).astype(BF16)
        n_ref[:rows] = n
        o_ref[:rows] = x + hidden(n, width)
        if rows < tm:
            o_ref[rows:] = jnp.zeros((tm - rows, o_ref.shape[1]), F32)

    def step(rows, width):
        o_ref[:rows] += hidden(n_ref[:rows], width)

    partial = _ffn_block(i, f, last) == last

    def row_tile(on, src_ref, rows):
        if tail == tf:
            pl.when(on & (f == 0))(lambda: first(src_ref, rows, tf))
            pl.when(on & (f > 0))(lambda: step(rows, tf))
        else:
            pl.when(on & (f == 0) & ~partial)(lambda: first(src_ref, rows, tf))
            pl.when(on & (f == 0) & partial)(lambda: first(src_ref, rows, tail))
            pl.when(on & (f > 0) & ~partial)(lambda: step(rows, tf))
            pl.when(on & (f > 0) & partial)(lambda: step(rows, tail))

    if short_tile is None:
        row_tile(i >= 0, h_ref, tm)
    else:
        row_tile(i != short_tile, h_ref, tm)
        row_tile(i == short_tile, h_ref if t_ref is None else t_ref, LANES)

    for which, (start, count) in enumerate(turns):
        turn = i * (last + 1) + f - start

        @pl.when((turn >= 0) & (turn < count))
        def _(which=which):
            dst[which][...] = src[which][...].astype(BF16)


def _ffn(h, g, wg, wu, wd, rows=None, tail_rows=None, short_last=False, convert=None, tm=512, tf=1024):
    m, d = h.shape
    m = m if rows is None else rows
    assert m % tm == 0, (m, tm)
    n_main = m // tm
    n_tile = n_main + (tail_rows is not None)
    ff = wg.shape[-1]
    nf = pl.cdiv(ff, tf)
    tail = ff - (nf - 1) * tf
    assert tail % LANES == 0 and nf >= 2, ff
    assert short_last or tail_rows is None
    blk = lambda i, f: _ffn_block(i, f, nf - 1)
    in_specs = [pl.BlockSpec((tm, d), lambda i, f: (jnp.minimum(i, n_main - 1), 0))]
    args = [h]
    if tail_rows is not None:
        assert tail_rows.shape == (tm, d)
        in_specs.append(_resident((tm, d)))
        args.append(tail_rows)
    in_specs += [pl.BlockSpec((1, d), lambda i, f: (0, 0)),
                 pl.BlockSpec((d, tf), lambda i, f: (0, blk(i, f))),
                 pl.BlockSpec((d, tf), lambda i, f: (0, blk(i, f))),
                 pl.BlockSpec((tf, d), lambda i, f: (blk(i, f), 0))]
    args += [g, wg, wu, wd]
    out_specs = [pl.BlockSpec((tm, d), lambda i, f: (i, 0))]
    out_shape = [jax.ShapeDtypeStruct((n_tile * tm, d), F32)]
    turns = []
    cast_outside = None
    if convert is not None:
        assert d % CVT_ROWS_WIDE == 0 and ff % LANES == 0
        jobs = [(w, CVT_ROWS_WIDE if w.shape[2] == ff else LANES) for w in convert[0]]
        if sum(w.shape[1] // r for w, r in jobs) > n_tile * nf:
            cast_outside = tuple(w[convert[1]].astype(BF16) for w in convert[0])
            convert = None
    if convert is not None:
        layer = convert[1]
        start = 0
        for w, r in jobs:
            count, width = w.shape[1] // r, w.shape[2]
            s = lambda i, f, start=start, count=count: _cvt_slab(i, f, nf, start, count)
            in_specs.append(pl.BlockSpec((None, r, width), lambda i, f, s=s: (layer, s(i, f), 0)))
            out_specs.append(pl.BlockSpec((r, width), lambda i, f, s=s: (s(i, f), 0)))
            out_shape.append(jax.ShapeDtypeStruct(w.shape[1:], BF16))
            args.append(w)
            turns.append((start, count))
            start += count
    outs = pl.pallas_call(
        functools.partial(_ffn_kernel, tail, n_tile - 1 if short_last else None, tail_rows is not None, tuple(turns)),
        grid=(n_tile, nf),
        in_specs=in_specs,
        out_specs=out_specs,
        out_shape=out_shape,
        scratch_shapes=[pltpu.VMEM((tm, d), BF16)],
        compiler_params=_cparams(("arbitrary", "arbitrary")),
        name="ffn",
    )(*args)
    return outs[0], (tuple(outs[1:]) if convert is not None else cast_outside)


def _oproj_kernel(h_ref, o_ref, w_ref, out_ref):
    out_ref[...] = h_ref[...] + jnp.dot(o_ref[...], w_ref[...], preferred_element_type=F32)


def _oproj(h, o, w, tm=512):
    m, d = h.shape
    assert m % tm == 0, (m, tm)
    return pl.pallas_call(
        _oproj_kernel,
        grid=(m // tm,),
        in_specs=[
            pl.BlockSpec((tm, d), lambda i: (i, 0)),
            pl.BlockSpec((tm, o.shape[1]), lambda i: (i, 0)),
            _resident(w.shape),
        ],
        out_specs=pl.BlockSpec((tm, d), lambda i: (i, 0)),
        out_shape=jax.ShapeDtypeStruct((m, d), F32),
        compiler_params=_cparams(("parallel",)),
        name="oproj",
    )(h, o, w)


def _aproj_kernel(h_ref, g_ref, wq_ref, wk_ref, wvt_ref, wqi_ref, wkw_ref, qn_ref, kn_ref, ikn_ref,
                  c128_ref, a128_ref, b128_ref, c64_ref, a64_ref, b64_ref,
                  q_ref, k_ref, vt_ref, qi_ref, ki2_ref, wit_ref):
    tm = h_ref.shape[0]
    hn = _rms(h_ref[...], g_ref[...]).astype(BF16)
    c128, a128, b128 = c128_ref[...], a128_ref[...], b128_ref[...]
    c64, a64, b64 = c64_ref[...], a64_ref[...], b64_ref[...]
    half128 = A_HEAD_DIM // ROPE_FRACTION // 2
    half64 = IDX_DIM // ROPE_FRACTION // 2

    qf = jnp.dot(hn, wq_ref[...], preferred_element_type=F32)
    for hd in range(A_HEADS):
        sl = slice(hd * LANES, (hd + 1) * LANES)
        q_ref[:, sl] = _rope(_rms(qf[:, sl], qn_ref[...]), c128, a128, b128, half128).astype(BF16)

    kf = jnp.dot(hn, wk_ref[...], preferred_element_type=F32)
    for hd in range(A_KV_HEADS):
        sl = slice(hd * LANES, (hd + 1) * LANES)
        k_ref[:, sl] = _rope(_rms(kf[:, sl], kn_ref[...]), c128, a128, b128, half128).astype(BF16)

    vt = _nt(wvt_ref[...], hn)
    tail = jnp.where(lax.broadcasted_iota(jnp.int32, (VT_ROWS - A_HEAD_DIM, LANES), 0) == 0, 1.0, 0.0).astype(BF16)
    for c in range(tm // LANES):
        for n in range(A_KV_HEADS):
            vt_ref[c, n * VT_ROWS:n * VT_ROWS + A_HEAD_DIM, :] = (
                vt[n * A_HEAD_DIM:(n + 1) * A_HEAD_DIM, c * LANES:(c + 1) * LANES].astype(BF16))
            vt_ref[c, n * VT_ROWS + A_HEAD_DIM:(n + 1) * VT_ROWS, :] = tail

    qif = jnp.dot(hn, wqi_ref[...], preferred_element_type=F32)
    for s in range(IDX_HEADS * IDX_DIM // LANES):
        sl = slice(s * LANES, (s + 1) * LANES)
        qi_ref[:, sl] = _rope(qif[:, sl], c64, a64, b64, half64).astype(BF16)

    kw = jnp.dot(hn, wkw_ref[...], preferred_element_type=F32)
    lane = lax.broadcasted_iota(jnp.int32, kw.shape, 1)
    ms = jnp.sum(jnp.where(lane < IDX_DIM, kw * kw, 0.0), axis=-1, keepdims=True) * (1.0 / IDX_DIM)
    ki = _rope(kw * lax.rsqrt(ms + EPS) * ikn_ref[...], c64, a64, b64, half64)
    ki2_ref[:, :LANES] = ki.astype(BF16)
    ki2_ref[:, LANES:] = pltpu.roll(ki, IDX_DIM, 1).astype(BF16)
    wit_ref[...] = kw.T[IDX_DIM:IDX_DIM + IDX_HEADS, :] * ((IDX_HEADS * IDX_DIM) ** -0.5)


def _tab_spec(tm, s_len, n_x):
    per, n_main = s_len // tm, n_x // tm
    assert s_len % tm == 0 and n_x % s_len == 0
    return pl.BlockSpec((tm, LANES), lambda i: (jnp.where(i < n_main, i % per, per + i - n_main), 0))


def _aproj(h, g, wq, wk, wvt, wqi, wkw, qn, kn, ikn, t128, t64, s_len, tm=256):
    m, d = h.shape
    assert m % tm == 0, (m, tm)
    row = lambda w: pl.BlockSpec((tm, w), lambda i: (i, 0))
    tab = _tab_spec(tm, s_len, m - TAIL_ROWS)
    vec = lambda w: pl.BlockSpec((1, w), lambda i: (0, 0))
    kvd = wk.shape[1]
    return pl.pallas_call(
        _aproj_kernel,
        grid=(m // tm,),
        in_specs=[row(d), vec(d), _resident(wq.shape), _resident(wk.shape), _resident(wvt.shape),
                  _resident(wqi.shape), _resident(wkw.shape), vec(LANES), vec(LANES), vec(LANES),
                  tab, tab, tab, tab, tab, tab],
        out_specs=[row(wq.shape[1]), row(kvd),
                   pl.BlockSpec((tm // LANES, A_KV_HEADS * VT_ROWS, LANES), lambda i: (i, 0, 0)),
                   row(wqi.shape[1]), row(2 * LANES),
                   pl.BlockSpec((IDX_HEADS, tm), lambda i: (0, i))],
        out_shape=[jax.ShapeDtypeStruct((m, wq.shape[1]), BF16),
                   jax.ShapeDtypeStruct((m, kvd), BF16),
                   jax.ShapeDtypeStruct((m // LANES, A_KV_HEADS * VT_ROWS, LANES), BF16),
                   jax.ShapeDtypeStruct((m, wqi.shape[1]), BF16),
                   jax.ShapeDtypeStruct((m, 2 * LANES), BF16),
                   jax.ShapeDtypeStruct((IDX_HEADS, m), F32)],
        compiler_params=_cparams(("parallel",)),
        name="aproj",
    )(h, g, wq, wk, wvt, wqi, wkw, qn, kn, ikn, *t128, *t64)


def _dsa_kernel(topk, n_xblk, nq, qi_ref, wit_ref, q_ref, kix_ref, kim_ref, kx_ref, km_ref, vtx_ref, vtm_ref,
                o_ref, score_ref, acc_ref):
    qb = q_ref.shape[0]
    ck = KEY_CHUNKS * LANES
    s_len = kx_ref.shape[0]
    g = pl.program_id(0)
    is_tail = g >= n_xblk
    j = lax.rem(g, nq)
    npair = jnp.where(is_tail, 0, (j + KEY_CHUNKS) // KEY_CHUNKS)
    kf = jnp.float32(topk)
    group = A_HEADS // A_KV_HEADS
    n_slab = IDX_HEADS * IDX_DIM // LANES
    n_pos = s_len + N_META

    lane = lax.broadcasted_iota(jnp.int32, (1, qb), 1)
    tail_pos = jnp.where(g == n_xblk, jnp.maximum(lane - META_ROW0, 0), 0)
    qpos = jnp.where(is_tail, tail_pos, N_META + j * qb + lane)
    kpos_m = lax.broadcasted_iota(jnp.int32, (N_META, qb), 0)
    meta_rows = slice(META_ROW0, LANES)
    kpos_x = lax.broadcasted_iota(jnp.int32, (ck, qb), 0) + N_META

    def pair_row(i):
        return pl.multiple_of(i * ck, ck)

    qi = qi_ref[...]
    qs = jnp.concatenate([qi[:, s * LANES:(s + 1) * LANES] for s in range(n_slab)], axis=0)

    def scores(kc, kpos):
        n = kc.shape[0]
        kcat = jnp.concatenate([kc[:, :LANES], kc[:, LANES:]], axis=0)
        s2 = _nt(kcat, qs)
        acc = jnp.zeros((n, qb), F32)
        for s in range(n_slab):
            for par in range(2):
                blk = s2[par * n:(par + 1) * n, s * qb:(s + 1) * qb]
                acc = acc + jnp.maximum(blk, 0.0) * wit_ref[2 * s + par:2 * s + par + 1, :]
        vis = kpos <= qpos
        lo = jnp.min(jnp.where(vis, acc, jnp.inf), axis=0, keepdims=True)
        hi = jnp.max(jnp.where(vis, acc, -jnp.inf), axis=0, keepdims=True)
        return jnp.where(vis, acc, -jnp.inf), lo, hi

    sc, mn, mx = scores(kim_ref[meta_rows, :], kpos_m)
    score_ref[pl.ds(s_len, N_META), :] = sc

    def score_pair(i, carry):
        mn, mx = carry
        r0 = pair_row(i)
        sc, lo, hi = scores(kix_ref[pl.ds(r0, ck), :], kpos_x + i * ck)
        score_ref[pl.ds(r0, ck), :] = sc
        return jnp.minimum(mn, lo), jnp.maximum(mx, hi)

    mn, mx = lax.fori_loop(0, npair, score_pair, (mn, mx))

    n_part = 8

    def reduce_keys(fn, init, combine):
        def part(r0, n, kpos):
            return fn(score_ref[pl.ds(r0, n), :], kpos.astype(F32))
        acc = jnp.concatenate([part(s_len, N_META, kpos_m), jnp.full((n_part * 8 - N_META, qb), init, F32)], axis=0)
        return lax.fori_loop(
            0, npair,
            lambda i, a: combine(a, part(pair_row(i), ck, kpos_x + i * ck).reshape(ck // (8 * n_part), n_part * 8, qb)),
            acc)

    def count(pred):
        acc = reduce_keys(lambda blk, kidx: jnp.where(pred(blk, kidx), 1.0, 0.0), 0.0,
                          lambda a, x: a + jnp.sum(x, axis=0))
        return jnp.sum(acc, axis=0, keepdims=True)

    def min_where(pred):
        acc = reduce_keys(lambda blk, kidx: jnp.where(pred(blk, kidx), blk, jnp.inf), jnp.inf,
                          lambda a, x: jnp.minimum(a, jnp.min(x, axis=0)))
        return jnp.min(acc, axis=0, keepdims=True)

    def bisect(_, lohi):
        lo, hi = lohi
        mid = 0.5 * lo + 0.5 * hi
        ok = count(lambda blk, kidx: blk >= mid) >= kf
        return jnp.where(ok, mid, lo), jnp.where(ok, hi, mid)

    lo, _ = lax.fori_loop(0, N_BISECT, bisect, (mn, mx))
    c_lo = count(lambda blk, kidx: blk >= lo)
    inexact = jnp.max(jnp.where(c_lo > kf, 1.0, 0.0)) > 0.0

    def exact_threshold(lo):
        def step(state):
            lo, _ = state
            m = min_where(lambda blk, kidx: blk >= lo)
            adv = count(lambda blk, kidx: blk > m) >= kf
            nxt = min_where(lambda blk, kidx: blk > m)
            return jnp.where(adv, nxt, lo), jnp.max(jnp.where(adv, 1.0, 0.0))

        lo, _ = lax.while_loop(lambda st: st[1] > 0.0, step, (lo, jnp.float32(1.0)))
        m = min_where(lambda blk, kidx: blk >= lo)
        r = kf - count(lambda blk, kidx: blk > m)
        tied = count(lambda blk, kidx: blk == m) > r

        def idx_bisect(_, ab):
            a, b = ab
            mid = jnp.floor(0.5 * (a + b))
            ok = count(lambda blk, kidx: (blk == m) & (kidx <= mid)) >= r
            return jnp.where(ok, a, mid), jnp.where(ok, mid, b)

        _, b = lax.fori_loop(0, int(math.ceil(math.log2(n_pos))) + 1, idx_bisect,
                             (jnp.full((1, qb), -1.0, F32), jnp.full((1, qb), float(n_pos - 1), F32)))
        return m, jnp.where(tied, b, float(n_pos))

    thr, cidx = lax.cond(inexact, exact_threshold,
                         lambda lo: (lo, jnp.full((1, qb), float(n_pos), F32)), lo)

    scale2 = A_HEAD_DIM ** -0.5 * math.log2(math.e)
    qg = [jnp.concatenate([q_ref[:, (n * group + h) * LANES:(n * group + h + 1) * LANES] for h in range(group)], axis=0)
          for n in range(A_KV_HEADS)]

    def logits(kcs):
        return tuple(_nt(kcs[n], qg[n]) for n in range(A_KV_HEADS))

    def attend(ms, sts, blk, kpos, vtcs, first=False):
        sel = (blk > thr) | ((blk == thr) & (kpos.astype(F32) <= cidx))
        bias = jnp.where(sel, 0.0, -jnp.inf)
        bias = jnp.concatenate([bias] * group, axis=1)
        new_ms, ps, alphas = [], [], []
        for n in range(A_KV_HEADS):
            st = sts[n] * scale2 + bias
            m_new = jnp.maximum(ms[n], jnp.max(st, axis=0, keepdims=True))
            ps.append(jnp.exp2(st - m_new).astype(BF16))
            alphas.append(jnp.exp2(ms[n] - m_new))
            new_ms.append(m_new)
        for n in range(A_KV_HEADS):
            p = ps[n]
            if p.shape[0] < vtcs[n].shape[1]:
                p = jnp.concatenate([jnp.zeros((vtcs[n].shape[1] - p.shape[0], p.shape[1]), BF16), p], axis=0)
            pv = jnp.dot(vtcs[n], p, preferred_element_type=F32)
            acc_ref[n] = pv if first else alphas[n] * acc_ref[n] + pv
        return tuple(new_ms)

    head = lambda n: slice(n * LANES, (n + 1) * LANES)
    vrow = lambda n: slice(n * VT_ROWS, (n + 1) * VT_ROWS)
    ms = tuple(jnp.full((1, group * qb), NEG_INIT, F32) for _ in range(A_KV_HEADS))
    ms = attend(ms, logits([km_ref[meta_rows, head(n)] for n in range(A_KV_HEADS)]),
                score_ref[pl.ds(s_len, N_META), :], kpos_m,
                [vtm_ref[0, vrow(n), :] for n in range(A_KV_HEADS)], first=True)

    def attend_pair(i, ms):
        r0 = pair_row(i)
        return attend(ms, logits([kx_ref[pl.ds(r0, ck), head(n)] for n in range(A_KV_HEADS)]),
                      score_ref[pl.ds(r0, ck), :], kpos_x + i * ck,
                      [jnp.concatenate([vtx_ref[KEY_CHUNKS * i + c, vrow(n), :] for c in range(KEY_CHUNKS)], axis=1)
                       for n in range(A_KV_HEADS)])

    lax.fori_loop(0, npair, attend_pair, ms)

    for n in range(A_KV_HEADS):
        a = acc_ref[n]
        ot = a[:A_HEAD_DIM] / a[A_HEAD_DIM:A_HEAD_DIM + 1]
        for h in range(group):
            hd = n * group + h
            o_ref[:, hd * LANES:(hd + 1) * LANES] = ot[:, h * qb:(h + 1) * qb].T.astype(BF16)


def _dsa(qi, wit, q, ki2, k, vt, bsz, s_len, topk):
    qb = LANES
    assert s_len % (KEY_CHUNKS * qb) == 0, s_len
    nq = s_len // qb
    n_xblk = bsz * nq
    m = q.shape[0]
    n_blk = m // qb
    kvd = k.shape[1]
    group = A_HEADS // A_KV_HEADS
    qrow = lambda w: pl.BlockSpec((qb, w), lambda g: (g, 0))
    batch = lambda g: jnp.minimum(g // nq, bsz - 1)
    return pl.pallas_call(
        functools.partial(_dsa_kernel, topk, n_xblk, nq),
        grid=(n_blk,),
        in_specs=[qrow(qi.shape[1]),
                  pl.BlockSpec((IDX_HEADS, qb), lambda g: (0, g)),
                  qrow(q.shape[1]),
                  pl.BlockSpec((s_len, 2 * LANES), lambda g: (batch(g), 0)),
                  pl.BlockSpec((qb, 2 * LANES), lambda g: (n_xblk, 0)),
                  pl.BlockSpec((s_len, kvd), lambda g: (batch(g), 0)),
                  pl.BlockSpec((qb, kvd), lambda g: (n_xblk, 0)),
                  pl.BlockSpec((nq, vt.shape[1], LANES), lambda g: (batch(g), 0, 0)),
                  pl.BlockSpec((1, vt.shape[1], LANES), lambda g: (n_xblk, 0, 0))],
        out_specs=qrow(q.shape[1]),
        out_shape=jax.ShapeDtypeStruct((m, q.shape[1]), BF16),
        scratch_shapes=[pltpu.VMEM((s_len + qb, qb), F32),
                        pltpu.VMEM((A_KV_HEADS, VT_ROWS, group * qb), F32)],
        compiler_params=_cparams(("arbitrary",)),
        name="dsa",
    )(qi, wit, q, ki2, ki2, k, k, vt, vt)


def _rms64_tiles(x, g, pavg):
    x2 = x * x
    hi = x2.astype(BF16)
    lo = (x2 - hi.astype(F32)).astype(BF16)
    ms = jnp.dot(hi, pavg, preferred_element_type=F32) + jnp.dot(lo, pavg, preferred_element_type=F32)
    y = x * lax.rsqrt(ms + EPS)
    return [y[:, t * LANES:(t + 1) * LANES] * g for t in range(2)]


def _split_halves(x, par):
    lane = lax.broadcasted_iota(jnp.int32, x.shape, 1)
    keep = jnp.where(lane < B_HEAD_DIM if par == 0 else lane >= B_HEAD_DIM, x, 0.0)
    moved = pltpu.roll(keep, B_HEAD_DIM, 1)
    return (keep, moved) if par == 0 else (moved, keep)


def _kvproj_kernel(h_ref, g_ref, w_ref, kn_ref, pavg_ref, c64_ref, a64_ref, b64_ref, k2_ref, v2_ref):
    hn = _rms(h_ref[...], g_ref[...]).astype(BF16)
    kv = jnp.dot(hn, w_ref[...], preferred_element_type=F32)
    half = B_HEAD_DIM // ROPE_FRACTION // 2
    kvd = B_KV_HEADS * B_HEAD_DIM
    assert kvd == 2 * LANES
    k_tiles = _rms64_tiles(kv[:, :kvd], kn_ref[...], pavg_ref[...])
    for s in range(kvd // LANES):
        ks = _rope(k_tiles[s], c64_ref[...], a64_ref[...], b64_ref[...], half)
        vs = kv[:, kvd + s * LANES:kvd + (s + 1) * LANES]
        for par in range(2):
            n = 2 * s + par
            for src, dst in ((ks, k2_ref), (vs, v2_ref)):
                low, high = _split_halves(src, par)
                dst[:, (2 * n) * LANES:(2 * n + 1) * LANES] = low.astype(BF16)
                dst[:, (2 * n + 1) * LANES:(2 * n + 2) * LANES] = high.astype(BF16)


def _kvproj(h, g, w, kn, pavg, t64, s_len, tm=512):
    m, d = h.shape
    assert m % tm == 0, (m, tm)
    row = lambda w_: pl.BlockSpec((tm, w_), lambda i: (i, 0))
    vec = lambda w_: pl.BlockSpec((1, w_), lambda i: (0, 0))
    tab = _tab_spec(tm, s_len, m - TAIL_ROWS)
    wide = B_KV_HEADS * 2 * LANES
    return pl.pallas_call(
        _kvproj_kernel,
        grid=(m // tm,),
        in_specs=[row(d), vec(d), _resident(w.shape), vec(LANES), _resident(pavg.shape), tab, tab, tab],
        out_specs=[row(wide), row(wide)],
        out_shape=[jax.ShapeDtypeStruct((m, wide), BF16)] * 2,
        compiler_params=_cparams(("parallel",)),
        name="kvproj",
    )(h, g, w, kn, pavg, *t64)


def _bqproj_kernel(h_ref, g_ref, w_ref, qn_ref, pavg_ref, c64_ref, a64_ref, b64_ref, q_ref):
    hn = _rms(h_ref[...], g_ref[...]).astype(BF16)
    qf = jnp.dot(hn, w_ref[...], preferred_element_type=F32)
    half = B_HEAD_DIM // ROPE_FRACTION // 2
    scale = B_HEAD_DIM ** -0.5
    for s in range(B_HEADS * B_HEAD_DIM // (2 * LANES)):
        tiles = _rms64_tiles(qf[:, 2 * s * LANES:(2 * s + 2) * LANES], qn_ref[...], pavg_ref[...])
        for t in range(2):
            sl = slice((2 * s + t) * LANES, (2 * s + t + 1) * LANES)
            y = _rope(tiles[t], c64_ref[...], a64_ref[...], b64_ref[...], half)
            q_ref[:, sl] = (y * scale).astype(BF16)


def _bqproj(h, g, w, qn, pavg, t64, s_len, tm=256):
    m, d = h.shape
    assert m % tm == 0, (m, tm)
    row = lambda w_: pl.BlockSpec((tm, w_), lambda i: (i, 0))
    vec = lambda w_: pl.BlockSpec((1, w_), lambda i: (0, 0))
    tab = _tab_spec(tm, s_len, m - TAIL_ROWS)
    return pl.pallas_call(
        _bqproj_kernel,
        grid=(m // tm,),
        in_specs=[row(d), vec(d), _resident(w.shape), vec(LANES), _resident(pavg.shape), tab, tab, tab],
        out_specs=row(w.shape[1]),
        out_shape=jax.ShapeDtypeStruct((m, w.shape[1]), BF16),
        compiler_params=_cparams(("parallel",)),
        name="bqproj",
    )(h, g, w, qn, pavg, *t64)


def _swa_kernel(n_xblk, nq, sink_ref, q_ref, kp_ref, kc_ref, vp_ref, vc_ref, o_ref):
    qb = q_ref.shape[0]
    g = pl.program_id(0)
    per_kv = B_HEADS // B_KV_HEADS
    n_tile = per_kv // 2
    r = lax.broadcasted_iota(jnp.int32, (qb, 2 * qb), 0)
    col = lax.broadcasted_iota(jnp.int32, (qb, 2 * qb), 1)
    rel = qb + r - col
    first_x = jnp.where(lax.rem(g, nq) == 0, META_ROW0, 0)
    first_tail = jnp.where(g == n_xblk, qb + META_ROW0, qb)
    first_col = jnp.where(g >= n_xblk, first_tail, first_x)
    mask = (rel >= 0) & (rel < WINDOW) & (col >= first_col)
    for n in range(B_KV_HEADS):
        lo = slice(2 * n * LANES, (2 * n + 1) * LANES)
        hi = slice((2 * n + 1) * LANES, (2 * n + 2) * LANES)
        kcat = jnp.concatenate([kp_ref[:, lo], kc_ref[:, lo], kp_ref[:, hi], kc_ref[:, hi]], axis=0)
        vcat = jnp.concatenate([vp_ref[:, lo], vc_ref[:, lo], vp_ref[:, hi], vc_ref[:, hi]], axis=0)
        qst = jnp.concatenate([q_ref[:, (n * n_tile + s) * LANES:(n * n_tile + s + 1) * LANES] for s in range(n_tile)], axis=0)
        logits = _nt(qst, kcat)
        rows = []
        for s in range(n_tile):
            halves = []
            for par in range(2):
                sink = sink_ref[n * per_kv + 2 * s + par]
                lg = jnp.where(mask, logits[s * qb:(s + 1) * qb, par * 2 * qb:(par + 1) * 2 * qb], -jnp.inf)
                m = jnp.maximum(jnp.max(lg, axis=-1, keepdims=True), sink)
                e = jnp.exp(lg - m)
                p = e / (jnp.sum(e, axis=-1, keepdims=True) + jnp.exp(sink - m))
                halves.append(p.astype(BF16))
            rows.append(jnp.concatenate(halves, axis=1))
        p2 = jnp.concatenate(rows, axis=0)
        o = jnp.dot(p2, vcat, preferred_element_type=F32)
        for s in range(n_tile):
            t = n * n_tile + s
            o_ref[:, t * LANES:(t + 1) * LANES] = o[s * qb:(s + 1) * qb].astype(BF16)


def _swa(sinks, q, k2, v2, bsz, s_len):
    qb = LANES
    nq = s_len // qb
    n_xblk = bsz * nq
    m, width = q.shape
    wide = k2.shape[1]
    cur = lambda w: pl.BlockSpec((qb, w), lambda g: (g, 0))
    prev = lambda w: pl.BlockSpec(
        (qb, w), lambda g: (jnp.where((g >= n_xblk) | (lax.rem(g, nq) == 0), n_xblk, g - 1), 0))
    return pl.pallas_call(
        functools.partial(_swa_kernel, n_xblk, nq),
        grid=(m // qb,),
        in_specs=[pl.BlockSpec(memory_space=pltpu.SMEM), cur(width), prev(wide), cur(wide), prev(wide), cur(wide)],
        out_specs=cur(width),
        out_shape=jax.ShapeDtypeStruct((m, width), BF16),
        compiler_params=_cparams(("parallel",)),
        name="swa",
    )(sinks, q, k2, k2, v2, v2)


def _rope_tables(pos, head_dim):
    half = head_dim // ROPE_FRACTION // 2
    inv = ROPE_THETA ** (-jnp.arange(half, dtype=F32) / half)
    ang = pos.astype(F32)[:, None] * inv[None, :]
    cos, sin = jnp.cos(ang), jnp.sin(ang)
    lp = jnp.arange(LANES) % head_dim
    idx = lp % half
    c = jnp.where(lp < 2 * half, cos[:, idx], 1.0)
    a = jnp.where(lp < half, -sin[:, idx], 0.0)
    b = jnp.where((lp >= half) & (lp < 2 * half), sin[:, idx], 0.0)
    return c, a, b


def _pad_cols(w, mult):
    pad = (-w.shape[-1]) % mult
    return jnp.pad(w, ((0, 0), (0, pad))) if pad else w


def kernel(x, meta_tokens, ffn1_norm, ffn1_w_gate, ffn1_w_up, ffn1_w_down, ffn2_norm, ffn2_w_gate, ffn2_w_up, ffn2_w_down, a_norm, a_w_in, a_q_norm, a_k_norm, a_idx_k_norm, a_w_out, kv_norm, kv_w, kv_k_norm, b_norm, b_w_q, b_q_norm, b_sinks, b_w_out):
    bsz, s_len, d = x.shape
    depth = ffn1_norm.shape[0]
    n_a = a_norm.shape[0]
    topk = min(TOPK_MAX, s_len // 4)
    n_x = bsz * s_len
    assert n_x % TAIL_ROWS == 0 and meta_tokens.shape[0] == N_META

    zeros = lambda n: jnp.zeros((n, d), x.dtype)
    tail_rows = jnp.concatenate([zeros(META_ROW0), meta_tokens.astype(x.dtype), zeros(TAIL_ROWS - LANES)], axis=0)
    h = x.reshape(n_x, d)
    seq_pos = N_META + jnp.arange(s_len, dtype=jnp.int32)
    tail_pos = jnp.clip(jnp.arange(TAIL_ROWS, dtype=jnp.int32) - META_ROW0, 0, N_META - 1)
    tail_pos = jnp.where(jnp.arange(TAIL_ROWS) < LANES, tail_pos, 0)
    pos = jnp.concatenate([seq_pos, tail_pos])
    t128 = _rope_tables(pos, A_HEAD_DIM)
    t64 = _rope_tables(pos, B_HEAD_DIM)
    seg = jnp.arange(2 * LANES) // B_HEAD_DIM
    pavg = jnp.where(seg[:, None] == seg[None, :], 1.0 / B_HEAD_DIM, 0.0).astype(BF16)
    tile2 = lambda g: jnp.tile(g.astype(F32), 2)[None, :]

    ffn_w = []
    for layer in range(depth):
        ffn_w += [((ffn1_w_gate, ffn1_w_up, ffn1_w_down), layer), ((ffn2_w_gate, ffn2_w_up, ffn2_w_down), layer)]
    state = {"w": tuple(w[0].astype(BF16) for w in ffn_w[0][0]), "k": 0}

    def ffn(h, g, **kw):
        k = state["k"]
        nxt = ffn_w[k + 1] if k + 1 < len(ffn_w) else None
        h, w_next = _ffn(h, g[None, :], *state["w"], convert=nxt, **kw)
        state["w"], state["k"] = w_next, k + 1
        return h

    q_dim, kv_dim, qi_dim = A_HEADS * A_HEAD_DIM, A_KV_HEADS * A_HEAD_DIM, IDX_HEADS * IDX_DIM
    k2 = v2 = None
    for layer in range(depth):
        if layer == n_a:
            k2, v2 = _kvproj(h, kv_norm[None, :], kv_w.astype(BF16), tile2(kv_k_norm), pavg, t64, s_len)
        h = ffn(h, ffn1_norm[layer], tail_rows=tail_rows if layer == 0 else None, short_last=True)
        if layer < n_a:
            i = layer
            w = a_w_in[i].astype(BF16)
            o0, o1, o2, o3 = q_dim, q_dim + kv_dim, q_dim + 2 * kv_dim, q_dim + 2 * kv_dim + qi_dim
            wkw = _pad_cols(w[:, o3:], LANES)
            ikn = _pad_cols(a_idx_k_norm[i][None, :], LANES)
            q, k, vt, qi, ki2, wit = _aproj(
                h, a_norm[i][None, :], w[:, :o0], w[:, o0:o1], w[:, o1:o2].T, w[:, o2:o3], wkw,
                a_q_norm[i][None, :], a_k_norm[i][None, :], ikn, t128, t64, s_len)
            o = _dsa(qi, wit, q, ki2, k, vt, bsz, s_len, topk)
            h = _oproj(h, o, a_w_out[i].astype(BF16))
        else:
            i = layer - n_a
            q = _bqproj(h, b_norm[i][None, :], b_w_q[i].astype(BF16), tile2(b_q_norm[i]), pavg, t64, s_len)
            o = _swa(b_sinks[i].astype(F32), q, k2, v2, bsz, s_len)
            h = _oproj(h, o, b_w_out[i].astype(BF16))
        final = layer == depth - 1
        h = ffn(h, ffn2_norm[layer], rows=n_x if final else None, short_last=not final)
    return h.reshape(bsz, s_len, d)
```

```python
import functools
import math

import jax
import jax.numpy as jnp
from jax import lax
from jax.experimental import pallas as pl
from jax.experimental.pallas import tpu as pltpu

F32 = jnp.float32
BF16 = jnp.bfloat16

LANES = 128
N_META = 16
META_ROW0 = LANES - N_META
TAIL_ROWS = 512
EPS = 1e-6
ROPE_THETA = 500000.0
ROPE_FRACTION = 4
A_HEADS, A_KV_HEADS, A_HEAD_DIM = 16, 4, 128
IDX_HEADS, IDX_DIM = 16, 64
TOPK_MAX = 256
B_HEADS, B_KV_HEADS, B_HEAD_DIM = 32, 4, 64
WINDOW = 128
VMEM_LIMIT = 60 * 1024 * 1024
NEG_INIT = -1e30
N_BISECT = 22
CVT_ROWS_WIDE = 32
KEY_CHUNKS = 4
VT_ROWS = A_HEAD_DIM + 16


def _cparams(sem):
    return pltpu.CompilerParams(dimension_semantics=sem, vmem_limit_bytes=VMEM_LIMIT)


def _resident(shape):
    nd = len(shape)
    return pl.BlockSpec(shape, lambda *_: (0,) * nd, pipeline_mode=pl.Buffered(1))


def _rms(x, g):
    ms = jnp.mean(x * x, axis=-1, keepdims=True)
    return x * lax.rsqrt(ms + EPS) * g


def _rope(y, c, a, b, half):
    return y * c + pltpu.roll(y, LANES - half, 1) * a + pltpu.roll(y, half, 1) * b


def _nt(a, b):
    return lax.dot_general(a, b, (((1,), (1,)), ((), ())), preferred_element_type=F32)


def _ffn_block(i, f, last):
    return jnp.where(i % 2 == 0, f, last - f)


def _cvt_slab(i, f, nf, start, count):
    return jnp.clip(i * nf + f - start, 0, count - 1)


def _ffn_kernel(tail, short_tile, has_tail_src, turns, *refs):
    refs = list(refs)
    h_ref = refs.pop(0)
    t_ref = refs.pop(0) if has_tail_src else None
    g_ref, wg_ref, wu_ref, wd_ref = refs[:4]
    n_cvt = len(turns)
    src = refs[4:4 + n_cvt]
    rest = refs[4 + n_cvt:]
    o_ref = rest[0]
    dst = rest[1:1 + n_cvt]
    n_ref = rest[-1]
    i = pl.program_id(0)
    f = pl.program_id(1)
    last = pl.num_programs(1) - 1
    tf = wg_ref.shape[-1]
    tm = o_ref.shape[0]

    def hidden(n, width):
        gate = jnp.dot(n, wg_ref[:, :width], preferred_element_type=F32)
        up = jnp.dot(n, wu_ref[:, :width], preferred_element_type=F32)
        act = (0.5 * gate) * jax.nn.sigmoid(gate) * up
        return jnp.dot(act.astype(BF16), wd_ref[:width, :], preferred_element_type=F32)

    def first(src_ref, rows, width):
        x = src_ref[:rows]
        n = _rms(x, g_ref[...]).astype(BF16)
        n_ref[:rows] = n
        o_ref[:rows] = x + hidden(n, width)
        if rows < tm:
            o_ref[rows:] = jnp.zeros((tm - rows, o_ref.shape[1]), F32)

    def step(rows, width):
        o_ref[:rows] += hidden(n_ref[:rows], width)

    partial = _ffn_block(i, f, last) == last

    def row_tile(on, src_ref, rows):
        if tail == tf:
            pl.when(on & (f == 0))(lambda: first(src_ref, rows, tf))
            pl.when(on & (f > 0))(lambda: step(rows, tf))
        else:
            pl.when(on & (f == 0) & ~partial)(lambda: first(src_ref, rows, tf))
            pl.when(on & (f == 0) & partial)(lambda: first(src_ref, rows, tail))
            pl.when(on & (f > 0) & ~partial)(lambda: step(rows, tf))
            pl.when(on & (f > 0) & partial)(lambda: step(rows, tail))

    if short_tile is None:
        row_tile(i >= 0, h_ref, tm)
    else:
        row_tile(i != short_tile, h_ref, tm)
        row_tile(i == short_tile, h_ref if t_ref is None else t_ref, LANES)

    for which, (start, count) in enumerate(turns):
        turn = i * (last + 1) + f - start

        @pl.when((turn >= 0) & (turn < count))
        def _(which=which):
            dst[which][...] = src[which][...].astype(BF16)


def _ffn(h, g, wg, wu, wd, rows=None, tail_rows=None, short_last=False, convert=None, tm=512, tf=1024):
    m, d = h.shape
    m = m if rows is None else rows
    assert m % tm == 0, (m, tm)
    n_main = m // tm
    n_tile = n_main + (tail_rows is not None)
    ff = wg.shape[-1]
    nf = pl.cdiv(ff, tf)
    tail = ff - (nf - 1) * tf
    assert tail % LANES == 0 and nf >= 2, ff
    assert short_last or tail_rows is None
    blk = lambda i, f: _ffn_block(i, f, nf - 1)
    in_specs = [pl.BlockSpec((tm, d), lambda i, f: (jnp.minimum(i, n_main - 1), 0))]
    args = [h]
    if tail_rows is not None:
        assert tail_rows.shape == (tm, d)
        in_specs.append(_resident((tm, d)))
        args.append(tail_rows)
    in_specs += [pl.BlockSpec((1, d), lambda i, f: (0, 0)),
                 pl.BlockSpec((d, tf), lambda i, f: (0, blk(i, f))),
                 pl.BlockSpec((d, tf), lambda i, f: (0, blk(i, f))),
                 pl.BlockSpec((tf, d), lambda i, f: (blk(i, f), 0))]
    args += [g, wg, wu, wd]
    out_specs = [pl.BlockSpec((tm, d), lambda i, f: (i, 0))]
    out_shape = [jax.ShapeDtypeStruct((n_tile * tm, d), F32)]
    turns = []
    cast_outside = None
    if convert is not None:
        assert d % CVT_ROWS_WIDE == 0 and ff % LANES == 0
        jobs = [(w, CVT_ROWS_WIDE if w.shape[2] == ff else LANES) for w in convert[0]]
        if sum(w.shape[1] // r for w, r in jobs) > n_tile * nf:
            cast_outside = tuple(w[convert[1]].astype(BF16) for w in convert[0])
            convert = None
    if convert is not None:
        layer = convert[1]
        start = 0
        for w, r in jobs:
            count, width = w.shape[1] // r, w.shape[2]
            s = lambda i, f, start=start, count=count: _cvt_slab(i, f, nf, start, count)
            in_specs.append(pl.BlockSpec((None, r, width), lambda i, f, s=s: (layer, s(i, f), 0)))
            out_specs.append(pl.BlockSpec((r, width), lambda i, f, s=s: (s(i, f), 0)))
            out_shape.append(jax.ShapeDtypeStruct(w.shape[1:], BF16))
            args.append(w)
            turns.append((start, count))
            start += count
    outs = pl.pallas_call(
        functools.partial(_ffn_kernel, tail, n_tile - 1 if short_last else None, tail_rows is not None, tuple(turns)),
        grid=(n_tile, nf),
        in_specs=in_specs,
        out_specs=out_specs,
        out_shape=out_shape,
        scratch_shapes=[pltpu.VMEM((tm, d), BF16)],
        compiler_params=_cparams(("arbitrary", "arbitrary")),
        name="ffn",
    )(*args)
    return outs[0], (tuple(outs[1:]) if convert is not None else cast_outside)


def _oproj_kernel(h_ref, o_ref, w_ref, out_ref):
    out_ref[...] = h_ref[...] + jnp.dot(o_ref[...], w_ref[...], preferred_element_type=F32)


def _oproj(h, o, w, tm=512):
    m, d = h.shape
    assert m % tm == 0, (m, tm)
    return pl.pallas_call(
        _oproj_kernel,
        grid=(m // tm,),
        in_specs=[
            pl.BlockSpec((tm, d), lambda i: (i, 0)),
            pl.BlockSpec((tm, o.shape[1]), lambda i: (i, 0)),
            _resident(w.shape),
        ],
        out_specs=pl.BlockSpec((tm, d), lambda i: (i, 0)),
        out_shape=jax.ShapeDtypeStruct((m, d), F32),
        compiler_params=_cparams(("parallel",)),
        name="oproj",
    )(h, o, w)


def _aproj_kernel(h_ref, g_ref, wq_ref, wk_ref, wvt_ref, wqi_ref, wkw_ref, qn_ref, kn_ref, ikn_ref,
                  c128_ref, a128_ref, b128_ref, c64_ref, a64_ref, b64_ref,
                  q_ref, k_ref, vt_ref, qi_ref, ki2_ref, wit_ref):
    tm = h_ref.shape[0]
    hn = _rms(h_ref[...], g_ref[...]).astype(BF16)
    c128, a128, b128 = c128_ref[...], a128_ref[...], b128_ref[...]
    c64, a64, b64 = c64_ref[...], a64_ref[...], b64_ref[...]
    half128 = A_HEAD_DIM // ROPE_FRACTION // 2
    half64 = IDX_DIM // ROPE_FRACTION // 2

    qf = jnp.dot(hn, wq_ref[...], preferred_element_type=F32)
    for hd in range(A_HEADS):
        sl = slice(hd * LANES, (hd + 1) * LANES)
        q_ref[:, sl] = _rope(_rms(qf[:, sl], qn_ref[...]), c128, a128, b128, half128).astype(BF16)

    kf = jnp.dot(hn, wk_ref[...], preferred_element_type=F32)
    for hd in range(A_KV_HEADS):
        sl = slice(hd * LANES, (hd + 1) * LANES)
        k_ref[:, sl] = _rope(_rms(kf[:, sl], kn_ref[...]), c128, a128, b128, half128).astype(BF16)

    vt = _nt(wvt_ref[...], hn)
    tail = jnp.where(lax.broadcasted_iota(jnp.int32, (VT_ROWS - A_HEAD_DIM, LANES), 0) == 0, 1.0, 0.0).astype(BF16)
    for c in range(tm // LANES):
        for n in range(A_KV_HEADS):
            vt_ref[c, n * VT_ROWS:n * VT_ROWS + A_HEAD_DIM, :] = (
                vt[n * A_HEAD_DIM:(n + 1) * A_HEAD_DIM, c * LANES:(c + 1) * LANES].astype(BF16))
            vt_ref[c, n * VT_ROWS + A_HEAD_DIM:(n + 1) * VT_ROWS, :] = tail

    qif = jnp.dot(hn, wqi_ref[...], preferred_element_type=F32)
    for s in range(IDX_HEADS * IDX_DIM // LANES):
        sl = slice(s * LANES, (s + 1) * LANES)
        qi_ref[:, sl] = _rope(qif[:, sl], c64, a64, b64, half64).astype(BF16)

    kw = jnp.dot(hn, wkw_ref[...], preferred_element_type=F32)
    lane = lax.broadcasted_iota(jnp.int32, kw.shape, 1)
    ms = jnp.sum(jnp.where(lane < IDX_DIM, kw * kw, 0.0), axis=-1, keepdims=True) * (1.0 / IDX_DIM)
    ki = _rope(kw * lax.rsqrt(ms + EPS) * ikn_ref[...], c64, a64, b64, half64)
    ki2_ref[:, :LANES] = ki.astype(BF16)
    ki2_ref[:, LANES:] = pltpu.roll(ki, IDX_DIM, 1).astype(BF16)
    wit_ref[...] = kw.T[IDX_DIM:IDX_DIM + IDX_HEADS, :] * ((IDX_HEADS * IDX_DIM) ** -0.5)


def _tab_spec(tm, s_len, n_x):
    per, n_main = s_len // tm, n_x // tm
    assert s_len % tm == 0 and n_x % s_len == 0
    return pl.BlockSpec((tm, LANES), lambda i: (jnp.where(i < n_main, i % per, per + i - n_main), 0))


def _aproj(h, g, wq, wk, wvt, wqi, wkw, qn, kn, ikn, t128, t64, s_len, tm=256):
    m, d = h.shape
    assert m % tm == 0, (m, tm)
    row = lambda w: pl.BlockSpec((tm, w), lambda i: (i, 0))
    tab = _tab_spec(tm, s_len, m - TAIL_ROWS)
    vec = lambda w: pl.BlockSpec((1, w), lambda i: (0, 0))
    kvd = wk.shape[1]
    return pl.pallas_call(
        _aproj_kernel,
        grid=(m // tm,),
        in_specs=[row(d), vec(d), _resident(wq.shape), _resident(wk.shape), _resident(wvt.shape),
                  _resident(wqi.shape), _resident(wkw.shape), vec(LANES), vec(LANES), vec(LANES),
                  tab, tab, tab, tab, tab, tab],
        out_specs=[row(wq.shape[1]), row(kvd),
                   pl.BlockSpec((tm // LANES, A_KV_HEADS * VT_ROWS, LANES), lambda i: (i, 0, 0)),
                   row(wqi.shape[1]), row(2 * LANES),
                   pl.BlockSpec((IDX_HEADS, tm), lambda i: (0, i))],
        out_shape=[jax.ShapeDtypeStruct((m, wq.shape[1]), BF16),
                   jax.ShapeDtypeStruct((m, kvd), BF16),
                   jax.ShapeDtypeStruct((m // LANES, A_KV_HEADS * VT_ROWS, LANES), BF16),
                   jax.ShapeDtypeStruct((m, wqi.shape[1]), BF16),
                   jax.ShapeDtypeStruct((m, 2 * LANES), BF16),
                   jax.ShapeDtypeStruct((IDX_HEADS, m), F32)],
        compiler_params=_cparams(("parallel",)),
        name="aproj",
    )(h, g, wq, wk, wvt, wqi, wkw, qn, kn, ikn, *t128, *t64)


def _dsa_kernel(topk, n_xblk, nq, qi_ref, wit_ref, q_ref, kix_ref, kim_ref, kx_ref, km_ref, vtx_ref, vtm_ref,
                o_ref, score_ref, acc_ref):
    qb = q_ref.shape[0]
    ck = KEY_CHUNKS * LANES
    s_len = kx_ref.shape[0]
    g = pl.program_id(0)
    is_tail = g >= n_xblk
    j = lax.rem(g, nq)
    npair = jnp.where(is_tail, 0, (j + KEY_CHUNKS) // KEY_CHUNKS)
    kf = jnp.float32(topk)
    group = A_HEADS // A_KV_HEADS
    n_slab = IDX_HEADS * IDX_DIM // LANES
    n_pos = s_len + N_META

    lane = lax.broadcasted_iota(jnp.int32, (1, qb), 1)
    tail_pos = jnp.where(g == n_xblk, jnp.maximum(lane - META_ROW0, 0), 0)
    qpos = jnp.where(is_tail, tail_pos, N_META + j * qb + lane)
    kpos_m = lax.broadcasted_iota(jnp.int32, (N_META, qb), 0)
    meta_rows = slice(META_ROW0, LANES)
    kpos_x = lax.broadcasted_iota(jnp.int32, (ck, qb), 0) + N_META

    def pair_row(i):
        return pl.multiple_of(i * ck, ck)

    qi = qi_ref[...]
    qs = jnp.concatenate([qi[:, s * LANES:(s + 1) * LANES] for s in range(n_slab)], axis=0)

    def scores(kc, kpos):
        n = kc.shape[0]
        kcat = jnp.concatenate([kc[:, :LANES], kc[:, LANES:]], axis=0)
        s2 = _nt(kcat, qs)
        acc = jnp.zeros((n, qb), F32)
        for s in range(n_slab):
            for par in range(2):
                blk = s2[par * n:(par + 1) * n, s * qb:(s + 1) * qb]
                acc = acc + jnp.maximum(blk, 0.0) * wit_ref[2 * s + par:2 * s + par + 1, :]
        vis = kpos <= qpos
        lo = jnp.min(jnp.where(vis, acc, jnp.inf), axis=0, keepdims=True)
        hi = jnp.max(jnp.where(vis, acc, -jnp.inf), axis=0, keepdims=True)
        return jnp.where(vis, acc, -jnp.inf), lo, hi

    sc, mn, mx = scores(kim_ref[meta_rows, :], kpos_m)
    score_ref[pl.ds(s_len, N_META), :] = sc

    def score_pair(i, carry):
        mn, mx = carry
        r0 = pair_row(i)
        sc, lo, hi = scores(kix_ref[pl.ds(r0, ck), :], kpos_x + i * ck)
        score_ref[pl.ds(r0, ck), :] = sc
        return jnp.minimum(mn, lo), jnp.maximum(mx, hi)

    mn, mx = lax.fori_loop(0, npair, score_pair, (mn, mx))

    n_part = 8

    def reduce_keys(fn, init, combine):
        def part(r0, n, kpos):
            return fn(score_ref[pl.ds(r0, n), :], kpos.astype(F32))
        acc = jnp.concatenate([part(s_len, N_META, kpos_m), jnp.full((n_part * 8 - N_META, qb), init, F32)], axis=0)
        return lax.fori_loop(
            0, npair,
            lambda i, a: combine(a, part(pair_row(i), ck, kpos_x + i * ck).reshape(ck // (8 * n_part), n_part * 8, qb)),
            acc)

    def count(pred):
        acc = reduce_keys(lambda blk, kidx: jnp.where(pred(blk, kidx), 1.0, 0.0), 0.0,
                          lambda a, x: a + jnp.sum(x, axis=0))
        return jnp.sum(acc, axis=0, keepdims=True)

    def min_where(pred):
        acc = reduce_keys(lambda blk, kidx: jnp.where(pred(blk, kidx), blk, jnp.inf), jnp.inf,
                          lambda a, x: jnp.minimum(a, jnp.min(x, axis=0)))
        return jnp.min(acc, axis=0, keepdims=True)

    def bisect(_, lohi):
        lo, hi = lohi
        mid = 0.5 * lo + 0.5 * hi
        ok = count(lambda blk, kidx: blk >= mid) >= kf
        return jnp.where(ok, mid, lo), jnp.where(ok, hi, mid)

    lo, _ = lax.fori_loop(0, N_BISECT, bisect, (mn, mx))
    c_lo = count(lambda blk, kidx: blk >= lo)
    inexact = jnp.max(jnp.where(c_lo > kf, 1.0, 0.0)) > 0.0

    def exact_threshold(lo):
        def step(state):
            lo, _ = state
            m = min_where(lambda blk, kidx: blk >= lo)
            adv = count(lambda blk, kidx: blk > m) >= kf
            nxt = min_where(lambda blk, kidx: blk > m)
            return jnp.where(adv, nxt, lo), jnp.max(jnp.where(adv, 1.0, 0.0))

        lo, _ = lax.while_loop(lambda st: st[1] > 0.0, step, (lo, jnp.float32(1.0)))
        m = min_where(lambda blk, kidx: blk >= lo)
        r = kf - count(lambda blk, kidx: blk > m)
        tied = count(lambda blk, kidx: blk == m) > r

        def idx_bisect(_, ab):
            a, b = ab
            mid = jnp.floor(0.5 * (a + b))
            ok = count(lambda blk, kidx: (blk == m) & (kidx <= mid)) >= r
            return jnp.where(ok, a, mid), jnp.where(ok, mid, b)

        def tie_cut(_):
            _, b = lax.fori_loop(0, int(math.ceil(math.log2(n_pos))) + 1, idx_bisect,
                                 (jnp.full((1, qb), -1.0, F32), jnp.full((1, qb), float(n_pos - 1), F32)))
            return jnp.where(tied, b, float(n_pos))

        any_tied = jnp.max(jnp.where(tied, 1.0, 0.0)) > 0.0
        return m, lax.cond(any_tied, tie_cut, lambda _: jnp.full((1, qb), float(n_pos), F32), 0)

    thr, cidx = lax.cond(inexact, exact_threshold,
                         lambda lo: (lo, jnp.full((1, qb), float(n_pos), F32)), lo)

    scale2 = A_HEAD_DIM ** -0.5 * math.log2(math.e)
    qg = [jnp.concatenate([q_ref[:, (n * group + h) * LANES:(n * group + h + 1) * LANES] for h in range(group)], axis=0)
          for n in range(A_KV_HEADS)]

    def logits(kcs):
        return tuple(_nt(kcs[n], qg[n]) for n in range(A_KV_HEADS))

    def attend(ms, sts, blk, kpos, vtcs, first=False):
        sel = (blk > thr) | ((blk == thr) & (kpos.astype(F32) <= cidx))
        bias = jnp.where(sel, 0.0, -jnp.inf)
        bias = jnp.concatenate([bias] * group, axis=1)
        new_ms, ps, alphas = [], [], []
        for n in range(A_KV_HEADS):
            st = sts[n] * scale2 + bias
            m_new = jnp.maximum(ms[n], jnp.max(st, axis=0, keepdims=True))
            ps.append(jnp.exp2(st - m_new).astype(BF16))
            alphas.append(jnp.exp2(ms[n] - m_new))
            new_ms.append(m_new)
        for n in range(A_KV_HEADS):
            p = ps[n]
            if p.shape[0] < vtcs[n].shape[1]:
                p = jnp.concatenate([jnp.zeros((vtcs[n].shape[1] - p.shape[0], p.shape[1]), BF16), p], axis=0)
            pv = jnp.dot(vtcs[n], p, preferred_element_type=F32)
            acc_ref[n] = pv if first else alphas[n] * acc_ref[n] + pv
        return tuple(new_ms)

    head = lambda n: slice(n * LANES, (n + 1) * LANES)
    vrow = lambda n: slice(n * VT_ROWS, (n + 1) * VT_ROWS)
    ms = tuple(jnp.full((1, group * qb), NEG_INIT, F32) for _ in range(A_KV_HEADS))
    ms = attend(ms, logits([km_ref[meta_rows, head(n)] for n in range(A_KV_HEADS)]),
                score_ref[pl.ds(s_len, N_META), :], kpos_m,
                [vtm_ref[0, vrow(n), :] for n in range(A_KV_HEADS)], first=True)

    def attend_pair(i, ms):
        r0 = pair_row(i)
        return attend(ms, logits([kx_ref[pl.ds(r0, ck), head(n)] for n in range(A_KV_HEADS)]),
                      score_ref[pl.ds(r0, ck), :], kpos_x + i * ck,
                      [jnp.concatenate([vtx_ref[KEY_CHUNKS * i + c, vrow(n), :] for c in range(KEY_CHUNKS)], axis=1)
                       for n in range(A_KV_HEADS)])

    lax.fori_loop(0, npair, attend_pair, ms)

    for n in range(A_KV_HEADS):
        a = acc_ref[n]
        ot = a[:A_HEAD_DIM] / a[A_HEAD_DIM:A_HEAD_DIM + 1]
        for h in range(group):
            hd = n * group + h
            o_ref[:, hd * LANES:(hd + 1) * LANES] = ot[:, h * qb:(h + 1) * qb].T.astype(BF16)


def _dsa(qi, wit, q, ki2, k, vt, bsz, s_len, topk):
    qb = LANES
    assert s_len % (KEY_CHUNKS * qb) == 0, s_len
    nq = s_len // qb
    n_xblk = bsz * nq
    m = q.shape[0]
    n_blk = m // qb
    kvd = k.shape[1]
    group = A_HEADS // A_KV_HEADS
    qrow = lambda w: pl.BlockSpec((qb, w), lambda g: (g, 0))
    batch = lambda g: jnp.minimum(g // nq, bsz - 1)
    return pl.pallas_call(
        functools.partial(_dsa_kernel, topk, n_xblk, nq),
        grid=(n_blk,),
        in_specs=[qrow(qi.shape[1]),
                  pl.BlockSpec((IDX_HEADS, qb), lambda g: (0, g)),
                  qrow(q.shape[1]),
                  pl.BlockSpec((s_len, 2 * LANES), lambda g: (batch(g), 0)),
                  pl.BlockSpec((qb, 2 * LANES), lambda g: (n_xblk, 0)),
                  pl.BlockSpec((s_len, kvd), lambda g: (batch(g), 0)),
                  pl.BlockSpec((qb, kvd), lambda g: (n_xblk, 0)),
                  pl.BlockSpec((nq, vt.shape[1], LANES), lambda g: (batch(g), 0, 0)),
                  pl.BlockSpec((1, vt.shape[1], LANES), lambda g: (n_xblk, 0, 0))],
        out_specs=qrow(q.shape[1]),
        out_shape=jax.ShapeDtypeStruct((m, q.shape[1]), BF16),
        scratch_shapes=[pltpu.VMEM((s_len + qb, qb), F32),
                        pltpu.VMEM((A_KV_HEADS, VT_ROWS, group * qb), F32)],
        compiler_params=_cparams(("arbitrary",)),
        name="dsa",
    )(qi, wit, q, ki2, ki2, k, k, vt, vt)


def _rms64_tiles(x, g, pavg):
    x2 = x * x
    hi = x2.astype(BF16)
    lo = (x2 - hi.astype(F32)).astype(BF16)
    ms = jnp.dot(hi, pavg, preferred_element_type=F32) + jnp.dot(lo, pavg, preferred_element_type=F32)
    y = x * lax.rsqrt(ms + EPS)
    return [y[:, t * LANES:(t + 1) * LANES] * g for t in range(2)]


def _split_halves(x, par):
    lane = lax.broadcasted_iota(jnp.int32, x.shape, 1)
    keep = jnp.where(lane < B_HEAD_DIM if par == 0 else lane >= B_HEAD_DIM, x, 0.0)
    moved = pltpu.roll(keep, B_HEAD_DIM, 1)
    return (keep, moved) if par == 0 else (moved, keep)


def _kvproj_kernel(h_ref, g_ref, w_ref, kn_ref, pavg_ref, c64_ref, a64_ref, b64_ref, k2_ref, v2_ref):
    hn = _rms(h_ref[...], g_ref[...]).astype(BF16)
    kv = jnp.dot(hn, w_ref[...], preferred_element_type=F32)
    half = B_HEAD_DIM // ROPE_FRACTION // 2
    kvd = B_KV_HEADS * B_HEAD_DIM
    assert kvd == 2 * LANES
    k_tiles = _rms64_tiles(kv[:, :kvd], kn_ref[...], pavg_ref[...])
    for s in range(kvd // LANES):
        ks = _rope(k_tiles[s], c64_ref[...], a64_ref[...], b64_ref[...], half)
        vs = kv[:, kvd + s * LANES:kvd + (s + 1) * LANES]
        for par in range(2):
            n = 2 * s + par
            for src, dst in ((ks, k2_ref), (vs, v2_ref)):
                low, high = _split_halves(src, par)
                dst[:, (2 * n) * LANES:(2 * n + 1) * LANES] = low.astype(BF16)
                dst[:, (2 * n + 1) * LANES:(2 * n + 2) * LANES] = high.astype(BF16)


def _kvproj(h, g, w, kn, pavg, t64, s_len, tm=512):
    m, d = h.shape
    assert m % tm == 0, (m, tm)
    row = lambda w_: pl.BlockSpec((tm, w_), lambda i: (i, 0))
    vec = lambda w_: pl.BlockSpec((1, w_), lambda i: (0, 0))
    tab = _tab_spec(tm, s_len, m - TAIL_ROWS)
    wide = B_KV_HEADS * 2 * LANES
    return pl.pallas_call(
        _kvproj_kernel,
        grid=(m // tm,),
        in_specs=[row(d), vec(d), _resident(w.shape), vec(LANES), _resident(pavg.shape), tab, tab, tab],
        out_specs=[row(wide), row(wide)],
        out_shape=[jax.ShapeDtypeStruct((m, wide), BF16)] * 2,
        compiler_params=_cparams(("parallel",)),
        name="kvproj",
    )(h, g, w, kn, pavg, *t64)


def _bqproj_kernel(h_ref, g_ref, w_ref, qn_ref, pavg_ref, c64_ref, a64_ref, b64_ref, q_ref):
    hn = _rms(h_ref[...], g_ref[...]).astype(BF16)
    qf = jnp.dot(hn, w_ref[...], preferred_element_type=F32)
    half = B_HEAD_DIM // ROPE_FRACTION // 2
    scale = B_HEAD_DIM ** -0.5
    for s in range(B_HEADS * B_HEAD_DIM // (2 * LANES)):
        tiles = _rms64_tiles(qf[:, 2 * s * LANES:(2 * s + 2) * LANES], qn_ref[...], pavg_ref[...])
        for t in range(2):
            sl = slice((2 * s + t) * LANES, (2 * s + t + 1) * LANES)
            y = _rope(tiles[t], c64_ref[...], a64_ref[...], b64_ref[...], half)
            q_ref[:, sl] = (y * scale).astype(BF16)


def _bqproj(h, g, w, qn, pavg, t64, s_len, tm=512):
    m, d = h.shape
    assert m % tm == 0, (m, tm)
    row = lambda w_: pl.BlockSpec((tm, w_), lambda i: (i, 0))
    vec = lambda w_: pl.BlockSpec((1, w_), lambda i: (0, 0))
    tab = _tab_spec(tm, s_len, m - TAIL_ROWS)
    return pl.pallas_call(
        _bqproj_kernel,
        grid=(m // tm,),
        in_specs=[row(d), vec(d), _resident(w.shape), vec(LANES), _resident(pavg.shape), tab, tab, tab],
        out_specs=row(w.shape[1]),
        out_shape=jax.ShapeDtypeStruct((m, w.shape[1]), BF16),
        compiler_params=_cparams(("parallel",)),
        name="bqproj",
    )(h, g, w, qn, pavg, *t64)


def _swa_kernel(n_xblk, nq, sink_ref, q_ref, kp_ref, kc_ref, vp_ref, vc_ref, o_ref):
    qb = q_ref.shape[0]
    g = pl.program_id(0)
    per_kv = B_HEADS // B_KV_HEADS
    n_tile = per_kv // 2
    r = lax.broadcasted_iota(jnp.int32, (qb, 2 * qb), 0)
    col = lax.broadcasted_iota(jnp.int32, (qb, 2 * qb), 1)
    rel = qb + r - col
    first_x = jnp.where(lax.rem(g, nq) == 0, META_ROW0, 0)
    first_tail = jnp.where(g == n_xblk, qb + META_ROW0, qb)
    first_col = jnp.where(g >= n_xblk, first_tail, first_x)
    mask = (rel >= 0) & (rel < WINDOW) & (col >= first_col)
    for n in range(B_KV_HEADS):
        lo = slice(2 * n * LANES, (2 * n + 1) * LANES)
        hi = slice((2 * n + 1) * LANES, (2 * n + 2) * LANES)
        kcat = jnp.concatenate([kp_ref[:, lo], kc_ref[:, lo], kp_ref[:, hi], kc_ref[:, hi]], axis=0)
        vcat = jnp.concatenate([vp_ref[:, lo], vc_ref[:, lo], vp_ref[:, hi], vc_ref[:, hi]], axis=0)
        qst = jnp.concatenate([q_ref[:, (n * n_tile + s) * LANES:(n * n_tile + s + 1) * LANES] for s in range(n_tile)], axis=0)
        logits = _nt(qst, kcat)
        rows = []
        for s in range(n_tile):
            halves = []
            for par in range(2):
                sink = sink_ref[n * per_kv + 2 * s + par]
                lg = jnp.where(mask, logits[s * qb:(s + 1) * qb, par * 2 * qb:(par + 1) * 2 * qb], -jnp.inf)
                m = jnp.maximum(jnp.max(lg, axis=-1, keepdims=True), sink)
                e = jnp.exp(lg - m)
                p = e / (jnp.sum(e, axis=-1, keepdims=True) + jnp.exp(sink - m))
                halves.append(p.astype(BF16))
            rows.append(jnp.concatenate(halves, axis=1))
        p2 = jnp.concatenate(rows, axis=0)
        o = jnp.dot(p2, vcat, preferred_element_type=F32)
        for s in range(n_tile):
            t = n * n_tile + s
            o_ref[:, t * LANES:(t + 1) * LANES] = o[s * qb:(s + 1) * qb].astype(BF16)


def _swa(sinks, q, k2, v2, bsz, s_len):
    qb = LANES
    nq = s_len // qb
    n_xblk = bsz * nq
    m, width = q.shape
    wide = k2.shape[1]
    cur = lambda w: pl.BlockSpec((qb, w), lambda g: (g, 0))
    prev = lambda w: pl.BlockSpec(
        (qb, w), lambda g: (jnp.where((g >= n_xblk) | (lax.rem(g, nq) == 0), n_xblk, g - 1), 0))
    return pl.pallas_call(
        functools.partial(_swa_kernel, n_xblk, nq),
        grid=(m // qb,),
        in_specs=[pl.BlockSpec(memory_space=pltpu.SMEM), cur(width), prev(wide), cur(wide), prev(wide), cur(wide)],
        out_specs=cur(width),
        out_shape=jax.ShapeDtypeStruct((m, width), BF16),
        compiler_params=_cparams(("parallel",)),
        name="swa",
    )(sinks, q, k2, k2, v2, v2)


def _rope_tables(pos, head_dim):
    half = head_dim // ROPE_FRACTION // 2
    inv = ROPE_THETA ** (-jnp.arange(half, dtype=F32) / half)
    ang = pos.astype(F32)[:, None] * inv[None, :]
    cos, sin = jnp.cos(ang), jnp.sin(ang)
    lp = jnp.arange(LANES) % head_dim
    idx = lp % half
    c = jnp.where(lp < 2 * half, cos[:, idx], 1.0)
    a = jnp.where(lp < half, -sin[:, idx], 0.0)
    b = jnp.where((lp >= half) & (lp < 2 * half), sin[:, idx], 0.0)
    return c, a, b


def _pad_cols(w, mult):
    pad = (-w.shape[-1]) % mult
    return jnp.pad(w, ((0, 0), (0, pad))) if pad else w


def kernel(x, meta_tokens, ffn1_norm, ffn1_w_gate, ffn1_w_up, ffn1_w_down, ffn2_norm, ffn2_w_gate, ffn2_w_up, ffn2_w_down, a_norm, a_w_in, a_q_norm, a_k_norm, a_idx_k_norm, a_w_out, kv_norm, kv_w, kv_k_norm, b_norm, b_w_q, b_q_norm, b_sinks, b_w_out):
    bsz, s_len, d = x.shape
    depth = ffn1_norm.shape[0]
    n_a = a_norm.shape[0]
    topk = min(TOPK_MAX, s_len // 4)
    n_x = bsz * s_len
    assert n_x % TAIL_ROWS == 0 and meta_tokens.shape[0] == N_META

    zeros = lambda n: jnp.zeros((n, d), x.dtype)
    tail_rows = jnp.concatenate([zeros(META_ROW0), meta_tokens.astype(x.dtype), zeros(TAIL_ROWS - LANES)], axis=0)
    h = x.reshape(n_x, d)
    seq_pos = N_META + jnp.arange(s_len, dtype=jnp.int32)
    tail_pos = jnp.clip(jnp.arange(TAIL_ROWS, dtype=jnp.int32) - META_ROW0, 0, N_META - 1)
    tail_pos = jnp.where(jnp.arange(TAIL_ROWS) < LANES, tail_pos, 0)
    pos = jnp.concatenate([seq_pos, tail_pos])
    t128 = _rope_tables(pos, A_HEAD_DIM)
    t64 = _rope_tables(pos, B_HEAD_DIM)
    seg = jnp.arange(2 * LANES) // B_HEAD_DIM
    pavg = jnp.where(seg[:, None] == seg[None, :], 1.0 / B_HEAD_DIM, 0.0).astype(BF16)
    tile2 = lambda g: jnp.tile(g.astype(F32), 2)[None, :]

    ffn_w = []
    for layer in range(depth):
        ffn_w += [((ffn1_w_gate, ffn1_w_up, ffn1_w_down), layer), ((ffn2_w_gate, ffn2_w_up, ffn2_w_down), layer)]
    state = {"w": tuple(w[0].astype(BF16) for w in ffn_w[0][0]), "k": 0}

    def ffn(h, g, **kw):
        k = state["k"]
        nxt = ffn_w[k + 1] if k + 1 < len(ffn_w) else None
        h, w_next = _ffn(h, g[None, :], *state["w"], convert=nxt, **kw)
        state["w"], state["k"] = w_next, k + 1
        return h

    q_dim, kv_dim, qi_dim = A_HEADS * A_HEAD_DIM, A_KV_HEADS * A_HEAD_DIM, IDX_HEADS * IDX_DIM
    k2 = v2 = None
    for layer in range(depth):
        if layer == n_a:
            k2, v2 = _kvproj(h, kv_norm[None, :], kv_w.astype(BF16), tile2(kv_k_norm), pavg, t64, s_len)
        h = ffn(h, ffn1_norm[layer], tail_rows=tail_rows if layer == 0 else None, short_last=True)
        if layer < n_a:
            i = layer
            w = a_w_in[i].astype(BF16)
            o0, o1, o2, o3 = q_dim, q_dim + kv_dim, q_dim + 2 * kv_dim, q_dim + 2 * kv_dim + qi_dim
            wkw = _pad_cols(w[:, o3:], LANES)
            ikn = _pad_cols(a_idx_k_norm[i][None, :], LANES)
            q, k, vt, qi, ki2, wit = _aproj(
                h, a_norm[i][None, :], w[:, :o0], w[:, o0:o1], w[:, o1:o2].T, w[:, o2:o3], wkw,
                a_q_norm[i][None, :], a_k_norm[i][None, :], ikn, t128, t64, s_len)
            o = _dsa(qi, wit, q, ki2, k, vt, bsz, s_len, topk)
            h = _oproj(h, o, a_w_out[i].astype(BF16))
        else:
            i = layer - n_a
            q = _bqproj(h, b_norm[i][None, :], b_w_q[i].astype(BF16), tile2(b_q_norm[i]), pavg, t64, s_len)
            o = _swa(b_sinks[i].astype(F32), q, k2, v2, bsz, s_len)
            h = _oproj(h, o, b_w_out[i].astype(BF16))
        final = layer == depth - 1
        h = ffn(h, ffn2_norm[layer], rows=n_x if final else None, short_last=not final)
    return h.reshape(bsz, s_len, d)
```

```python
import functools
import math

import jax
import jax.numpy as jnp
from jax import lax
from jax.experimental import pallas as pl
from jax.experimental.pallas import tpu as pltpu

F32 = jnp.float32
BF16 = jnp.bfloat16

LANES = 128
N_META = 16
META_ROW0 = LANES - N_META
TAIL_ROWS = 512
EPS = 1e-6
ROPE_THETA = 500000.0
ROPE_FRACTION = 4
A_HEADS, A_KV_HEADS, A_HEAD_DIM = 16, 4, 128
IDX_HEADS, IDX_DIM = 16, 64
TOPK_MAX = 256
B_HEADS, B_KV_HEADS, B_HEAD_DIM = 32, 4, 64
WINDOW = 128
VMEM_LIMIT = 60 * 1024 * 1024
NEG_INIT = -1e30
N_BISECT = 22
CVT_ROWS_WIDE = 32
KEY_CHUNKS = 4
VT_ROWS = A_HEAD_DIM + 16


def _cparams(sem):
    return pltpu.CompilerParams(dimension_semantics=sem, vmem_limit_bytes=VMEM_LIMIT)


def _resident(shape):
    nd = len(shape)
    return pl.BlockSpec(shape, lambda *_: (0,) * nd, pipeline_mode=pl.Buffered(1))


def _rms(x, g):
    ms = jnp.mean(x * x, axis=-1, keepdims=True)
    return x * lax.rsqrt(ms + EPS) * g


def _rope(y, c, a, b, half):
    return y * c + pltpu.roll(y, LANES - half, 1) * a + pltpu.roll(y, half, 1) * b


def _nt(a, b):
    return lax.dot_general(a, b, (((1,), (1,)), ((), ())), preferred_element_type=F32)


def _ffn_block(i, f, last):
    return jnp.where(i % 2 == 0, f, last - f)


def _cvt_slab(i, f, nf, start, count):
    return jnp.clip(i * nf + f - start, 0, count - 1)


def _ffn_kernel(tail, short_tile, has_tail_src, turns, *refs):
    refs = list(refs)
    h_ref = refs.pop(0)
    t_ref = refs.pop(0) if has_tail_src else None
    g_ref, wg_ref, wu_ref, wd_ref = refs[:4]
    n_cvt = len(turns)
    src = refs[4:4 + n_cvt]
    rest = refs[4 + n_cvt:]
    o_ref = rest[0]
    dst = rest[1:1 + n_cvt]
    n_ref = rest[-1]
    i = pl.program_id(0)
    f = pl.program_id(1)
    last = pl.num_programs(1) - 1
    tf = wg_ref.shape[-1]
    tm = o_ref.shape[0]

    def hidden(n, width):
        gate = jnp.dot(n, wg_ref[:, :width], preferred_element_type=F32)
        up = jnp.dot(n, wu_ref[:, :width], preferred_element_type=F32)
        act = (0.5 * gate) * jax.nn.sigmoid(gate) * up
        return jnp.dot(act.astype(BF16), wd_ref[:width, :], preferred_element_type=F32)

    def first(src_ref, rows, width):
        x = src_ref[:rows]
        n = _rms(x, g_ref[...]).astype(BF16)
        n_ref[:rows] = n
        o_ref[:rows] = x + hidden(n, width)
        if rows < tm:
            o_ref[rows:] = jnp.zeros((tm - rows, o_ref.shape[1]), F32)

    def step(rows, width):
        o_ref[:rows] += hidden(n_ref[:rows], width)

    partial = _ffn_block(i, f, last) == last

    def row_tile(on, src_ref, rows):
        if tail == tf:
            pl.when(on & (f == 0))(lambda: first(src_ref, rows, tf))
            pl.when(on & (f > 0))(lambda: step(rows, tf))
        else:
            pl.when(on & (f == 0) & ~partial)(lambda: first(src_ref, rows, tf))
            pl.when(on & (f == 0) & partial)(lambda: first(src_ref, rows, tail))
            pl.when(on & (f > 0) & ~partial)(lambda: step(rows, tf))
            pl.when(on & (f > 0) & partial)(lambda: step(rows, tail))

    if short_tile is None:
        row_tile(i >= 0, h_ref, tm)
    else:
        row_tile(i != short_tile, h_ref, tm)
        row_tile(i == short_tile, h_ref if t_ref is None else t_ref, LANES)

    for which, (start, count) in enumerate(turns):
        turn = i * (last + 1) + f - start

        @pl.when((turn >= 0) & (turn < count))
        def _(which=which):
            dst[which][...] = src[which][...].astype(BF16)


def _ffn(h, g, wg, wu, wd, rows=None, tail_rows=None, short_last=False, convert=None, tm=512, tf=1024):
    m, d = h.shape
    m = m if rows is None else rows
    assert m % tm == 0, (m, tm)
    n_main = m // tm
    n_tile = n_main + (tail_rows is not None)
    ff = wg.shape[-1]
    nf = pl.cdiv(ff, tf)
    tail = ff - (nf - 1) * tf
    assert tail % LANES == 0 and nf >= 2, ff
    assert short_last or tail_rows is None
    blk = lambda i, f: _ffn_block(i, f, nf - 1)
    in_specs = [pl.BlockSpec((tm, d), lambda i, f: (jnp.minimum(i, n_main - 1), 0))]
    args = [h]
    if tail_rows is not None:
        assert tail_rows.shape == (tm, d)
        in_specs.append(_resident((tm, d)))
        args.append(tail_rows)
    in_specs += [pl.BlockSpec((1, d), lambda i, f: (0, 0)),
                 pl.BlockSpec((d, tf), lambda i, f: (0, blk(i, f))),
                 pl.BlockSpec((d, tf), lambda i, f: (0, blk(i, f))),
                 pl.BlockSpec((tf, d), lambda i, f: (blk(i, f), 0))]
    args += [g, wg, wu, wd]
    out_specs = [pl.BlockSpec((tm, d), lambda i, f: (i, 0))]
    out_shape = [jax.ShapeDtypeStruct((n_tile * tm, d), F32)]
    turns = []
    cast_outside = None
    if convert is not None:
        assert d % CVT_ROWS_WIDE == 0 and ff % LANES == 0
        jobs = [(w, CVT_ROWS_WIDE if w.shape[2] == ff else LANES) for w in convert[0]]
        if sum(w.shape[1] // r for w, r in jobs) > n_tile * nf:
            cast_outside = tuple(w[convert[1]].astype(BF16) for w in convert[0])
            convert = None
    if convert is not None:
        layer = convert[1]
        start = 0
        for w, r in jobs:
            count, width = w.shape[1] // r, w.shape[2]
            s = lambda i, f, start=start, count=count: _cvt_slab(i, f, nf, start, count)
            in_specs.append(pl.BlockSpec((None, r, width), lambda i, f, s=s: (layer, s(i, f), 0)))
            out_specs.append(pl.BlockSpec((r, width), lambda i, f, s=s: (s(i, f), 0)))
            out_shape.append(jax.ShapeDtypeStruct(w.shape[1:], BF16))
            args.append(w)
            turns.append((start, count))
            start += count
    outs = pl.pallas_call(
        functools.partial(_ffn_kernel, tail, n_tile - 1 if short_last else None, tail_rows is not None, tuple(turns)),
        grid=(n_tile, nf),
        in_specs=in_specs,
        out_specs=out_specs,
        out_shape=out_shape,
        scratch_shapes=[pltpu.VMEM((tm, d), BF16)],
        compiler_params=_cparams(("arbitrary", "arbitrary")),
        name="ffn",
    )(*args)
    return outs[0], (tuple(outs[1:]) if convert is not None else cast_outside)


def _oproj_kernel(h_ref, o_ref, w_ref, out_ref):
    out_ref[...] = h_ref[...] + jnp.dot(o_ref[...], w_ref[...], preferred_element_type=F32)


def _oproj(h, o, w, tm=512):
    m, d = h.shape
    assert m % tm == 0, (m, tm)
    return pl.pallas_call(
        _oproj_kernel,
        grid=(m // tm,),
        in_specs=[
            pl.BlockSpec((tm, d), lambda i: (i, 0)),
            pl.BlockSpec((tm, o.shape[1]), lambda i: (i, 0)),
            _resident(w.shape),
        ],
        out_specs=pl.BlockSpec((tm, d), lambda i: (i, 0)),
        out_shape=jax.ShapeDtypeStruct((m, d), F32),
        compiler_params=_cparams(("parallel",)),
        name="oproj",
    )(h, o, w)


def _aproj_kernel(h_ref, g_ref, wq_ref, wk_ref, wvt_ref, wqi_ref, wkw_ref, qn_ref, kn_ref, ikn_ref,
                  c128_ref, a128_ref, b128_ref, c64_ref, a64_ref, b64_ref,
                  q_ref, k_ref, vt_ref, qi_ref, ki2_ref, wit_ref):
    tm = h_ref.shape[0]
    hn = _rms(h_ref[...], g_ref[...]).astype(BF16)
    c128, a128, b128 = c128_ref[...], a128_ref[...], b128_ref[...]
    c64, a64, b64 = c64_ref[...], a64_ref[...], b64_ref[...]
    half128 = A_HEAD_DIM // ROPE_FRACTION // 2
    half64 = IDX_DIM // ROPE_FRACTION // 2

    qf = jnp.dot(hn, wq_ref[...], preferred_element_type=F32)
    for hd in range(A_HEADS):
        sl = slice(hd * LANES, (hd + 1) * LANES)
        q_ref[:, sl] = _rope(_rms(qf[:, sl], qn_ref[...]), c128, a128, b128, half128).astype(BF16)

    kf = jnp.dot(hn, wk_ref[...], preferred_element_type=F32)
    for hd in range(A_KV_HEADS):
        sl = slice(hd * LANES, (hd + 1) * LANES)
        k_ref[:, sl] = _rope(_rms(kf[:, sl], kn_ref[...]), c128, a128, b128, half128).astype(BF16)

    vt = _nt(wvt_ref[...], hn)
    tail = jnp.where(lax.broadcasted_iota(jnp.int32, (VT_ROWS - A_HEAD_DIM, LANES), 0) == 0, 1.0, 0.0).astype(BF16)
    for c in range(tm // LANES):
        for n in range(A_KV_HEADS):
            vt_ref[c, n * VT_ROWS:n * VT_ROWS + A_HEAD_DIM, :] = (
                vt[n * A_HEAD_DIM:(n + 1) * A_HEAD_DIM, c * LANES:(c + 1) * LANES].astype(BF16))
            vt_ref[c, n * VT_ROWS + A_HEAD_DIM:(n + 1) * VT_ROWS, :] = tail

    qif = jnp.dot(hn, wqi_ref[...], preferred_element_type=F32)
    for s in range(IDX_HEADS * IDX_DIM // LANES):
        sl = slice(s * LANES, (s + 1) * LANES)
        qi_ref[:, sl] = _rope(qif[:, sl], c64, a64, b64, half64).astype(BF16)

    kw = jnp.dot(hn, wkw_ref[...], preferred_element_type=F32)
    lane = lax.broadcasted_iota(jnp.int32, kw.shape, 1)
    ms = jnp.sum(jnp.where(lane < IDX_DIM, kw * kw, 0.0), axis=-1, keepdims=True) * (1.0 / IDX_DIM)
    ki = _rope(kw * lax.rsqrt(ms + EPS) * ikn_ref[...], c64, a64, b64, half64)
    ki2_ref[:, :LANES] = ki.astype(BF16)
    ki2_ref[:, LANES:] = pltpu.roll(ki, IDX_DIM, 1).astype(BF16)
    wit_ref[...] = kw.T[IDX_DIM:IDX_DIM + IDX_HEADS, :] * ((IDX_HEADS * IDX_DIM) ** -0.5)


def _tab_spec(tm, s_len, n_x):
    per, n_main = s_len // tm, n_x // tm
    assert s_len % tm == 0 and n_x % s_len == 0
    return pl.BlockSpec((tm, LANES), lambda i: (jnp.where(i < n_main, i % per, per + i - n_main), 0))


def _aproj(h, g, wq, wk, wvt, wqi, wkw, qn, kn, ikn, t128, t64, s_len, tm=256):
    m, d = h.shape
    assert m % tm == 0, (m, tm)
    row = lambda w: pl.BlockSpec((tm, w), lambda i: (i, 0))
    tab = _tab_spec(tm, s_len, m - TAIL_ROWS)
    vec = lambda w: pl.BlockSpec((1, w), lambda i: (0, 0))
    kvd = wk.shape[1]
    return pl.pallas_call(
        _aproj_kernel,
        grid=(m // tm,),
        in_specs=[row(d), vec(d), _resident(wq.shape), _resident(wk.shape), _resident(wvt.shape),
                  _resident(wqi.shape), _resident(wkw.shape), vec(LANES), vec(LANES), vec(LANES),
                  tab, tab, tab, tab, tab, tab],
        out_specs=[row(wq.shape[1]), row(kvd),
                   pl.BlockSpec((tm // LANES, A_KV_HEADS * VT_ROWS, LANES), lambda i: (i, 0, 0)),
                   row(wqi.shape[1]), row(2 * LANES),
                   pl.BlockSpec((IDX_HEADS, tm), lambda i: (0, i))],
        out_shape=[jax.ShapeDtypeStruct((m, wq.shape[1]), BF16),
                   jax.ShapeDtypeStruct((m, kvd), BF16),
                   jax.ShapeDtypeStruct((m // LANES, A_KV_HEADS * VT_ROWS, LANES), BF16),
                   jax.ShapeDtypeStruct((m, wqi.shape[1]), BF16),
                   jax.ShapeDtypeStruct((m, 2 * LANES), BF16),
                   jax.ShapeDtypeStruct((IDX_HEADS, m), F32)],
        compiler_params=_cparams(("parallel",)),
        name="aproj",
    )(h, g, wq, wk, wvt, wqi, wkw, qn, kn, ikn, *t128, *t64)


def _dsa_kernel(topk, n_xblk, nq, qi_ref, wit_ref, q_ref, kix_ref, kim_ref, kx_ref, km_ref, vtx_ref, vtm_ref,
                o_ref, score_ref, acc_ref):
    qb = q_ref.shape[0]
    ck = KEY_CHUNKS * LANES
    s_len = kx_ref.shape[0]
    g = pl.program_id(0)
    is_tail = g >= n_xblk
    j = lax.rem(g, nq)
    npair = jnp.where(is_tail, 0, (j + KEY_CHUNKS) // KEY_CHUNKS)
    kf = jnp.float32(topk)
    group = A_HEADS // A_KV_HEADS
    n_slab = IDX_HEADS * IDX_DIM // LANES
    n_pos = s_len + N_META

    lane = lax.broadcasted_iota(jnp.int32, (1, qb), 1)
    tail_pos = jnp.where(g == n_xblk, jnp.maximum(lane - META_ROW0, 0), 0)
    qpos = jnp.where(is_tail, tail_pos, N_META + j * qb + lane)
    kpos_m = lax.broadcasted_iota(jnp.int32, (N_META, qb), 0)
    meta_rows = slice(META_ROW0, LANES)
    kpos_x = lax.broadcasted_iota(jnp.int32, (ck, qb), 0) + N_META

    def pair_row(i):
        return pl.multiple_of(i * ck, ck)

    qi = qi_ref[...]
    qs = jnp.concatenate([qi[:, s * LANES:(s + 1) * LANES] for s in range(n_slab)], axis=0)

    def scores(kc, kpos):
        n = kc.shape[0]
        kcat = jnp.concatenate([kc[:, :LANES], kc[:, LANES:]], axis=0)
        s2 = _nt(kcat, qs)
        acc = jnp.zeros((n, qb), F32)
        for s in range(n_slab):
            for par in range(2):
                blk = s2[par * n:(par + 1) * n, s * qb:(s + 1) * qb]
                acc = acc + jnp.maximum(blk, 0.0) * wit_ref[2 * s + par:2 * s + par + 1, :]
        vis = kpos <= qpos
        lo = jnp.min(jnp.where(vis, acc, jnp.inf), axis=0, keepdims=True)
        hi = jnp.max(jnp.where(vis, acc, -jnp.inf), axis=0, keepdims=True)
        return jnp.where(vis, acc, -jnp.inf), lo, hi

    sc, mn, mx = scores(kim_ref[meta_rows, :], kpos_m)
    score_ref[pl.ds(s_len, N_META), :] = sc

    def score_pair(i, carry):
        mn, mx = carry
        r0 = pair_row(i)
        sc, lo, hi = scores(kix_ref[pl.ds(r0, ck), :], kpos_x + i * ck)
        score_ref[pl.ds(r0, ck), :] = sc
        return jnp.minimum(mn, lo), jnp.maximum(mx, hi)

    mn, mx = lax.fori_loop(0, npair, score_pair, (mn, mx))

    n_part = 8

    def reduce_keys(fn, init, combine):
        def part(r0, n, kpos):
            return fn(score_ref[pl.ds(r0, n), :], kpos.astype(F32))
        acc = jnp.concatenate([part(s_len, N_META, kpos_m), jnp.full((n_part * 8 - N_META, qb), init, F32)], axis=0)
        return lax.fori_loop(
            0, npair,
            lambda i, a: combine(a, part(pair_row(i), ck, kpos_x + i * ck).reshape(ck // (8 * n_part), n_part * 8, qb)),
            acc)

    def count(pred):
        acc = reduce_keys(lambda blk, kidx: jnp.where(pred(blk, kidx), 1.0, 0.0), 0.0,
                          lambda a, x: a + jnp.sum(x, axis=0))
        return jnp.sum(acc, axis=0, keepdims=True)

    def min_where(pred):
        acc = reduce_keys(lambda blk, kidx: jnp.where(pred(blk, kidx), blk, jnp.inf), jnp.inf,
                          lambda a, x: jnp.minimum(a, jnp.min(x, axis=0)))
        return jnp.min(acc, axis=0, keepdims=True)

    def bisect(_, lohi):
        lo, hi = lohi
        mid = 0.5 * lo + 0.5 * hi
        ok = count(lambda blk, kidx: blk >= mid) >= kf
        return jnp.where(ok, mid, lo), jnp.where(ok, hi, mid)

    lo, _ = lax.fori_loop(0, N_BISECT, bisect, (mn, mx))
    c_lo = count(lambda blk, kidx: blk >= lo)
    inexact = jnp.max(jnp.where(c_lo > kf, 1.0, 0.0)) > 0.0

    def exact_threshold(lo):
        def step(state):
            lo, _ = state
            m = min_where(lambda blk, kidx: blk >= lo)
            adv = count(lambda blk, kidx: blk > m) >= kf
            nxt = min_where(lambda blk, kidx: blk > m)
            return jnp.where(adv, nxt, lo), jnp.max(jnp.where(adv, 1.0, 0.0))

        lo, _ = lax.while_loop(lambda st: st[1] > 0.0, step, (lo, jnp.float32(1.0)))
        m = min_where(lambda blk, kidx: blk >= lo)
        r = kf - count(lambda blk, kidx: blk > m)
        tied = count(lambda blk, kidx: blk == m) > r

        def idx_bisect(_, ab):
            a, b = ab
            mid = jnp.floor(0.5 * (a + b))
            ok = count(lambda blk, kidx: (blk == m) & (kidx <= mid)) >= r
            return jnp.where(ok, a, mid), jnp.where(ok, mid, b)

        def tie_cut(_):
            _, b = lax.fori_loop(0, int(math.ceil(math.log2(n_pos))) + 1, idx_bisect,
                                 (jnp.full((1, qb), -1.0, F32), jnp.full((1, qb), float(n_pos - 1), F32)))
            return jnp.where(tied, b, float(n_pos))

        any_tied = jnp.max(jnp.where(tied, 1.0, 0.0)) > 0.0
        return m, lax.cond(any_tied, tie_cut, lambda _: jnp.full((1, qb), float(n_pos), F32), 0)

    thr, cidx = lax.cond(inexact, exact_threshold,
                         lambda lo: (lo, jnp.full((1, qb), float(n_pos), F32)), lo)

    scale2 = A_HEAD_DIM ** -0.5 * math.log2(math.e)
    qg = [jnp.concatenate([q_ref[:, (n * group + h) * LANES:(n * group + h + 1) * LANES] for h in range(group)], axis=0)
          for n in range(A_KV_HEADS)]

    def logits(kcs):
        return tuple(_nt(kcs[n], qg[n]) for n in range(A_KV_HEADS))

    def attend(ms, sts, blk, kpos, vtcs, first=False):
        sel = (blk > thr) | ((blk == thr) & (kpos.astype(F32) <= cidx))
        bias = jnp.where(sel, 0.0, -jnp.inf)
        bias = jnp.concatenate([bias] * group, axis=1)
        new_ms, ps, alphas = [], [], []
        for n in range(A_KV_HEADS):
            st = sts[n] * scale2 + bias
            m_new = jnp.maximum(ms[n], jnp.max(st, axis=0, keepdims=True))
            ps.append(jnp.exp2(st - m_new).astype(BF16))
            alphas.append(jnp.exp2(ms[n] - m_new))
            new_ms.append(m_new)
        for n in range(A_KV_HEADS):
            p = ps[n]
            if p.shape[0] < vtcs[n].shape[1]:
                p = jnp.concatenate([jnp.zeros((vtcs[n].shape[1] - p.shape[0], p.shape[1]), BF16), p], axis=0)
            pv = jnp.dot(vtcs[n], p, preferred_element_type=F32)
            acc_ref[n] = pv if first else alphas[n] * acc_ref[n] + pv
        return tuple(new_ms)

    head = lambda n: slice(n * LANES, (n + 1) * LANES)
    vrow = lambda n: slice(n * VT_ROWS, (n + 1) * VT_ROWS)
    ms = tuple(jnp.full((1, group * qb), NEG_INIT, F32) for _ in range(A_KV_HEADS))
    ms = attend(ms, logits([km_ref[meta_rows, head(n)] for n in range(A_KV_HEADS)]),
                score_ref[pl.ds(s_len, N_META), :], kpos_m,
                [vtm_ref[0, vrow(n), :] for n in range(A_KV_HEADS)], first=True)

    def attend_pair(i, ms):
        r0 = pair_row(i)
        return attend(ms, logits([kx_ref[pl.ds(r0, ck), head(n)] for n in range(A_KV_HEADS)]),
                      score_ref[pl.ds(r0, ck), :], kpos_x + i * ck,
                      [jnp.concatenate([vtx_ref[KEY_CHUNKS * i + c, vrow(n), :] for c in range(KEY_CHUNKS)], axis=1)
                       for n in range(A_KV_HEADS)])

    lax.fori_loop(0, npair, attend_pair, ms)

    for n in range(A_KV_HEADS):
        a = acc_ref[n]
        ot = a[:A_HEAD_DIM] / a[A_HEAD_DIM:A_HEAD_DIM + 1]
        for h in range(group):
            hd = n * group + h
            o_ref[:, hd * LANES:(hd + 1) * LANES] = ot[:, h * qb:(h + 1) * qb].T.astype(BF16)


def _dsa(qi, wit, q, ki2, k, vt, bsz, s_len, topk):
    qb = LANES
    assert s_len % (KEY_CHUNKS * qb) == 0, s_len
    nq = s_len // qb
    n_xblk = bsz * nq
    m = q.shape[0]
    n_blk = m // qb
    kvd = k.shape[1]
    group = A_HEADS // A_KV_HEADS
    qrow = lambda w: pl.BlockSpec((qb, w), lambda g: (g, 0))
    batch = lambda g: jnp.minimum(g // nq, bsz - 1)
    return pl.pallas_call(
        functools.partial(_dsa_kernel, topk, n_xblk, nq),
        grid=(n_blk,),
        in_specs=[qrow(qi.shape[1]),
                  pl.BlockSpec((IDX_HEADS, qb), lambda g: (0, g)),
                  qrow(q.shape[1]),
                  pl.BlockSpec((s_len, 2 * LANES), lambda g: (batch(g), 0)),
                  pl.BlockSpec((qb, 2 * LANES), lambda g: (n_xblk, 0)),
                  pl.BlockSpec((s_len, kvd), lambda g: (batch(g), 0)),
                  pl.BlockSpec((qb, kvd), lambda g: (n_xblk, 0)),
                  pl.BlockSpec((nq, vt.shape[1], LANES), lambda g: (batch(g), 0, 0)),
                  pl.BlockSpec((1, vt.shape[1], LANES), lambda g: (n_xblk, 0, 0))],
        out_specs=qrow(q.shape[1]),
        out_shape=jax.ShapeDtypeStruct((m, q.shape[1]), BF16),
        scratch_shapes=[pltpu.VMEM((s_len + qb, qb), F32),
                        pltpu.VMEM((A_KV_HEADS, VT_ROWS, group * qb), F32)],
        compiler_params=_cparams(("arbitrary",)),
        name="dsa",
    )(qi, wit, q, ki2, ki2, k, k, vt, vt)


def _rms64_tiles(x, g, pavg):
    x2 = x * x
    hi = x2.astype(BF16)
    lo = (x2 - hi.astype(F32)).astype(BF16)
    ms = jnp.dot(hi, pavg, preferred_element_type=F32) + jnp.dot(lo, pavg, preferred_element_type=F32)
    y = x * lax.rsqrt(ms + EPS)
    return [y[:, t * LANES:(t + 1) * LANES] * g for t in range(2)]


def _split_halves(x, par):
    lane = lax.broadcasted_iota(jnp.int32, x.shape, 1)
    keep = jnp.where(lane < B_HEAD_DIM if par == 0 else lane >= B_HEAD_DIM, x, 0.0)
    moved = pltpu.roll(keep, B_HEAD_DIM, 1)
    return (keep, moved) if par == 0 else (moved, keep)


def _kvproj_kernel(h_ref, g_ref, w_ref, kn_ref, pavg_ref, c64_ref, a64_ref, b64_ref, k2_ref, v2_ref):
    hn = _rms(h_ref[...], g_ref[...]).astype(BF16)
    kv = jnp.dot(hn, w_ref[...], preferred_element_type=F32)
    half = B_HEAD_DIM // ROPE_FRACTION // 2
    kvd = B_KV_HEADS * B_HEAD_DIM
    assert kvd == 2 * LANES
    k_tiles = _rms64_tiles(kv[:, :kvd], kn_ref[...], pavg_ref[...])
    for s in range(kvd // LANES):
        ks = _rope(k_tiles[s], c64_ref[...], a64_ref[...], b64_ref[...], half)
        vs = kv[:, kvd + s * LANES:kvd + (s + 1) * LANES]
        for par in range(2):
            n = 2 * s + par
            for src, dst in ((ks, k2_ref), (vs, v2_ref)):
                low, high = _split_halves(src, par)
                dst[:, (2 * n) * LANES:(2 * n + 1) * LANES] = low.astype(BF16)
                dst[:, (2 * n + 1) * LANES:(2 * n + 2) * LANES] = high.astype(BF16)


def _kvproj(h, g, w, kn, pavg, t64, s_len, tm=512):
    m, d = h.shape
    assert m % tm == 0, (m, tm)
    row = lambda w_: pl.BlockSpec((tm, w_), lambda i: (i, 0))
    vec = lambda w_: pl.BlockSpec((1, w_), lambda i: (0, 0))
    tab = _tab_spec(tm, s_len, m - TAIL_ROWS)
    wide = B_KV_HEADS * 2 * LANES
    return pl.pallas_call(
        _kvproj_kernel,
        grid=(m // tm,),
        in_specs=[row(d), vec(d), _resident(w.shape), vec(LANES), _resident(pavg.shape), tab, tab, tab],
        out_specs=[row(wide), row(wide)],
        out_shape=[jax.ShapeDtypeStruct((m, wide), BF16)] * 2,
        compiler_params=_cparams(("parallel",)),
        name="kvproj",
    )(h, g, w, kn, pavg, *t64)


def _bqproj_kernel(h_ref, g_ref, w_ref, qn_ref, pavg_ref, c64_ref, a64_ref, b64_ref, q_ref):
    hn = _rms(h_ref[...], g_ref[...]).astype(BF16)
    qf = jnp.dot(hn, w_ref[...], preferred_element_type=F32)
    half = B_HEAD_DIM // ROPE_FRACTION // 2
    scale = B_HEAD_DIM ** -0.5
    for s in range(B_HEADS * B_HEAD_DIM // (2 * LANES)):
        tiles = _rms64_tiles(qf[:, 2 * s * LANES:(2 * s + 2) * LANES], qn_ref[...], pavg_ref[...])
        for t in range(2):
            sl = slice((2 * s + t) * LANES, (2 * s + t + 1) * LANES)
            y = _rope(tiles[t], c64_ref[...], a64_ref[...], b64_ref[...], half)
            q_ref[:, sl] = (y * scale).astype(BF16)


def _bqproj(h, g, w, qn, pavg, t64, s_len, tm=512):
    m, d = h.shape
    assert m % tm == 0, (m, tm)
    row = lambda w_: pl.BlockSpec((tm, w_), lambda i: (i, 0))
    vec = lambda w_: pl.BlockSpec((1, w_), lambda i: (0, 0))
    tab = _tab_spec(tm, s_len, m - TAIL_ROWS)
    return pl.pallas_call(
        _bqproj_kernel,
        grid=(m // tm,),
        in_specs=[row(d), vec(d), _resident(w.shape), vec(LANES), _resident(pavg.shape), tab, tab, tab],
        out_specs=row(w.shape[1]),
        out_shape=jax.ShapeDtypeStruct((m, w.shape[1]), BF16),
        compiler_params=_cparams(("parallel",)),
        name="bqproj",
    )(h, g, w, qn, pavg, *t64)


def _swa_kernel(n_xblk, nq, sink_ref, q_ref, kp_ref, kc_ref, vp_ref, vc_ref, o_ref):
    qb = q_ref.shape[0]
    g = pl.program_id(0)
    per_kv = B_HEADS // B_KV_HEADS
    n_tile = per_kv // 2
    r = lax.broadcasted_iota(jnp.int32, (qb, 2 * qb), 0)
    col = lax.broadcasted_iota(jnp.int32, (qb, 2 * qb), 1)
    rel = qb + r - col
    first_x = jnp.where(lax.rem(g, nq) == 0, META_ROW0, 0)
    first_tail = jnp.where(g == n_xblk, qb + META_ROW0, qb)
    first_col = jnp.where(g >= n_xblk, first_tail, first_x)
    mask = (rel >= 0) & (rel < WINDOW) & (col >= first_col)
    for n in range(B_KV_HEADS):
        lo = slice(2 * n * LANES, (2 * n + 1) * LANES)
        hi = slice((2 * n + 1) * LANES, (2 * n + 2) * LANES)
        kcat = jnp.concatenate([kp_ref[:, lo], kc_ref[:, lo], kp_ref[:, hi], kc_ref[:, hi]], axis=0)
        vcat = jnp.concatenate([vp_ref[:, lo], vc_ref[:, lo], vp_ref[:, hi], vc_ref[:, hi]], axis=0)
        qst = jnp.concatenate([q_ref[:, (n * n_tile + s) * LANES:(n * n_tile + s + 1) * LANES] for s in range(n_tile)], axis=0)
        logits = _nt(qst, kcat)
        rows, inv = [], []
        low_half = lax.broadcasted_iota(jnp.int32, (qb, LANES), 1) < B_HEAD_DIM
        for s in range(n_tile):
            halves, dens = [], []
            for par in range(2):
                sink = sink_ref[n * per_kv + 2 * s + par]
                lg = jnp.where(mask, logits[s * qb:(s + 1) * qb, par * 2 * qb:(par + 1) * 2 * qb], -jnp.inf)
                m = jnp.maximum(jnp.max(lg, axis=-1, keepdims=True), sink)
                e = jnp.exp(lg - m)
                dens.append(jnp.sum(e, axis=-1, keepdims=True) + jnp.exp(sink - m))
                halves.append(e.astype(BF16))
            rows.append(jnp.concatenate(halves, axis=1))
            inv.append(jnp.where(low_half, 1.0 / dens[0], 1.0 / dens[1]))
        p2 = jnp.concatenate(rows, axis=0)
        o = jnp.dot(p2, vcat, preferred_element_type=F32)
        for s in range(n_tile):
            t = n * n_tile + s
            o_ref[:, t * LANES:(t + 1) * LANES] = (o[s * qb:(s + 1) * qb] * inv[s]).astype(BF16)


def _swa(sinks, q, k2, v2, bsz, s_len):
    qb = LANES
    nq = s_len // qb
    n_xblk = bsz * nq
    m, width = q.shape
    wide = k2.shape[1]
    cur = lambda w: pl.BlockSpec((qb, w), lambda g: (g, 0))
    prev = lambda w: pl.BlockSpec(
        (qb, w), lambda g: (jnp.where((g >= n_xblk) | (lax.rem(g, nq) == 0), n_xblk, g - 1), 0))
    return pl.pallas_call(
        functools.partial(_swa_kernel, n_xblk, nq),
        grid=(m // qb,),
        in_specs=[pl.BlockSpec(memory_space=pltpu.SMEM), cur(width), prev(wide), cur(wide), prev(wide), cur(wide)],
        out_specs=cur(width),
        out_shape=jax.ShapeDtypeStruct((m, width), BF16),
        compiler_params=_cparams(("parallel",)),
        name="swa",
    )(sinks, q, k2, k2, v2, v2)


def _rope_tables(pos, head_dim):
    half = head_dim // ROPE_FRACTION // 2
    inv = ROPE_THETA ** (-jnp.arange(half, dtype=F32) / half)
    ang = pos.astype(F32)[:, None] * inv[None, :]
    cos, sin = jnp.cos(ang), jnp.sin(ang)
    lp = jnp.arange(LANES) % head_dim
    idx = lp % half
    c = jnp.where(lp < 2 * half, cos[:, idx], 1.0)
    a = jnp.where(lp < half, -sin[:, idx], 0.0)
    b = jnp.where((lp >= half) & (lp < 2 * half), sin[:, idx], 0.0)
    return c, a, b


def _pad_cols(w, mult):
    pad = (-w.shape[-1]) % mult
    return jnp.pad(w, ((0, 0), (0, pad))) if pad else w


def kernel(x, meta_tokens, ffn1_norm, ffn1_w_gate, ffn1_w_up, ffn1_w_down, ffn2_norm, ffn2_w_gate, ffn2_w_up, ffn2_w_down, a_norm, a_w_in, a_q_norm, a_k_norm, a_idx_k_norm, a_w_out, kv_norm, kv_w, kv_k_norm, b_norm, b_w_q, b_q_norm, b_sinks, b_w_out):
    bsz, s_len, d = x.shape
    depth = ffn1_norm.shape[0]
    n_a = a_norm.shape[0]
    topk = min(TOPK_MAX, s_len // 4)
    n_x = bsz * s_len
    assert n_x % TAIL_ROWS == 0 and meta_tokens.shape[0] == N_META

    zeros = lambda n: jnp.zeros((n, d), x.dtype)
    tail_rows = jnp.concatenate([zeros(META_ROW0), meta_tokens.astype(x.dtype), zeros(TAIL_ROWS - LANES)], axis=0)
    h = x.reshape(n_x, d)
    seq_pos = N_META + jnp.arange(s_len, dtype=jnp.int32)
    tail_pos = jnp.clip(jnp.arange(TAIL_ROWS, dtype=jnp.int32) - META_ROW0, 0, N_META - 1)
    tail_pos = jnp.where(jnp.arange(TAIL_ROWS) < LANES, tail_pos, 0)
    pos = jnp.concatenate([seq_pos, tail_pos])
    t128 = _rope_tables(pos, A_HEAD_DIM)
    t64 = _rope_tables(pos, B_HEAD_DIM)
    seg = jnp.arange(2 * LANES) // B_HEAD_DIM
    pavg = jnp.where(seg[:, None] == seg[None, :], 1.0 / B_HEAD_DIM, 0.0).astype(BF16)
    tile2 = lambda g: jnp.tile(g.astype(F32), 2)[None, :]

    ffn_w = []
    for layer in range(depth):
        ffn_w += [((ffn1_w_gate, ffn1_w_up, ffn1_w_down), layer), ((ffn2_w_gate, ffn2_w_up, ffn2_w_down), layer)]
    state = {"w": tuple(w[0].astype(BF16) for w in ffn_w[0][0]), "k": 0}

    def ffn(h, g, **kw):
        k = state["k"]
        nxt = ffn_w[k + 1] if k + 1 < len(ffn_w) else None
        h, w_next = _ffn(h, g[None, :], *state["w"], convert=nxt, **kw)
        state["w"], state["k"] = w_next, k + 1
        return h

    q_dim, kv_dim, qi_dim = A_HEADS * A_HEAD_DIM, A_KV_HEADS * A_HEAD_DIM, IDX_HEADS * IDX_DIM
    k2 = v2 = None
    for layer in range(depth):
        if layer == n_a:
            k2, v2 = _kvproj(h, kv_norm[None, :], kv_w.astype(BF16), tile2(kv_k_norm), pavg, t64, s_len)
        h = ffn(h, ffn1_norm[layer], tail_rows=tail_rows if layer == 0 else None, short_last=True)
        if layer < n_a:
            i = layer
            w = a_w_in[i].astype(BF16)
            o0, o1, o2, o3 = q_dim, q_dim + kv_dim, q_dim + 2 * kv_dim, q_dim + 2 * kv_dim + qi_dim
            wkw = _pad_cols(w[:, o3:], LANES)
            ikn = _pad_cols(a_idx_k_norm[i][None, :], LANES)
            q, k, vt, qi, ki2, wit = _aproj(
                h, a_norm[i][None, :], w[:, :o0], w[:, o0:o1], w[:, o1:o2].T, w[:, o2:o3], wkw,
                a_q_norm[i][None, :], a_k_norm[i][None, :], ikn, t128, t64, s_len)
            o = _dsa(qi, wit, q, ki2, k, vt, bsz, s_len, topk)
            h = _oproj(h, o, a_w_out[i].astype(BF16))
        else:
            i = layer - n_a
            q = _bqproj(h, b_norm[i][None, :], b_w_q[i].astype(BF16), tile2(b_q_norm[i]), pavg, t64, s_len)
            o = _swa(b_sinks[i].astype(F32), q, k2, v2, bsz, s_len)
            h = _oproj(h, o, b_w_out[i].astype(BF16))
        final = layer == depth - 1
        h = ffn(h, ffn2_norm[layer], rows=n_x if final else None, short_last=not final)
    return h.reshape(bsz, s_len, d)
```

```python
import functools
import math

import jax
import jax.numpy as jnp
from jax import lax
from jax.experimental import pallas as pl
from jax.experimental.pallas import tpu as pltpu

F32 = jnp.float32
BF16 = jnp.bfloat16

LANES = 128
N_META = 16
META_ROW0 = LANES - N_META
TAIL_ROWS = 512
EPS = 1e-6
ROPE_THETA = 500000.0
ROPE_FRACTION = 4
A_HEADS, A_KV_HEADS, A_HEAD_DIM = 16, 4, 128
IDX_HEADS, IDX_DIM = 16, 64
TOPK_MAX = 256
B_HEADS, B_KV_HEADS, B_HEAD_DIM = 32, 4, 64
WINDOW = 128
VMEM_LIMIT = 60 * 1024 * 1024
NEG_INIT = -1e30
A_LOGIT_SCALE2 = A_HEAD_DIM ** -0.5 * math.log2(math.e)
B_LOGIT_SCALE2 = B_HEAD_DIM ** -0.5 * math.log2(math.e)
N_BISECT = 22
CVT_ROWS_WIDE = 32
KEY_CHUNKS = 4
VT_ROWS = A_HEAD_DIM + 16


def _cparams(sem):
    return pltpu.CompilerParams(dimension_semantics=sem, vmem_limit_bytes=VMEM_LIMIT)


def _resident(shape):
    nd = len(shape)
    return pl.BlockSpec(shape, lambda *_: (0,) * nd, pipeline_mode=pl.Buffered(1))


def _rms(x, g):
    ms = jnp.mean(x * x, axis=-1, keepdims=True)
    return x * lax.rsqrt(ms + EPS) * g


def _rope(y, c, a, b, half):
    return y * c + pltpu.roll(y, LANES - half, 1) * a + pltpu.roll(y, half, 1) * b


def _nt(a, b):
    return lax.dot_general(a, b, (((1,), (1,)), ((), ())), preferred_element_type=F32)


def _ffn_block(i, f, last):
    return jnp.where(i % 2 == 0, f, last - f)


def _cvt_slab(i, f, nf, start, count):
    return jnp.clip(i * nf + f - start, 0, count - 1)


def _ffn_kernel(tail, short_tile, has_tail_src, turns, *refs):
    refs = list(refs)
    h_ref = refs.pop(0)
    t_ref = refs.pop(0) if has_tail_src else None
    g_ref, wg_ref, wu_ref, wd_ref = refs[:4]
    n_cvt = len(turns)
    src = refs[4:4 + n_cvt]
    rest = refs[4 + n_cvt:]
    o_ref = rest[0]
    dst = rest[1:1 + n_cvt]
    n_ref = rest[-1]
    i = pl.program_id(0)
    f = pl.program_id(1)
    last = pl.num_programs(1) - 1
    tf = wg_ref.shape[-1]
    tm = o_ref.shape[0]

    def hidden(n, width):
        gate = jnp.dot(n, wg_ref[:, :width], preferred_element_type=F32)
        up = jnp.dot(n, wu_ref[:, :width], preferred_element_type=F32)
        act = (0.5 * gate) * jax.nn.sigmoid(gate) * up
        return jnp.dot(act.astype(BF16), wd_ref[:width, :], preferred_element_type=F32)

    def first(src_ref, rows, width):
        x = src_ref[:rows]
        n = _rms(x, g_ref[...]).astype(BF16)
        n_ref[:rows] = n
        o_ref[:rows] = x + hidden(n, width)
        if rows < tm:
            o_ref[rows:] = jnp.zeros((tm - rows, o_ref.shape[1]), F32)

    def step(rows, width):
        o_ref[:rows] += hidden(n_ref[:rows], width)

    partial = _ffn_block(i, f, last) == last

    def row_tile(on, src_ref, rows):
        if tail == tf:
            pl.when(on & (f == 0))(lambda: first(src_ref, rows, tf))
            pl.when(on & (f > 0))(lambda: step(rows, tf))
        else:
            pl.when(on & (f == 0) & ~partial)(lambda: first(src_ref, rows, tf))
            pl.when(on & (f == 0) & partial)(lambda: first(src_ref, rows, tail))
            pl.when(on & (f > 0) & ~partial)(lambda: step(rows, tf))
            pl.when(on & (f > 0) & partial)(lambda: step(rows, tail))

    if short_tile is None:
        row_tile(i >= 0, h_ref, tm)
    else:
        row_tile(i != short_tile, h_ref, tm)
        row_tile(i == short_tile, h_ref if t_ref is None else t_ref, LANES)

    for which, (start, count) in enumerate(turns):
        turn = i * (last + 1) + f - start

        @pl.when((turn >= 0) & (turn < count))
        def _(which=which):
            dst[which][...] = src[which][...].astype(BF16)


def _ffn(h, g, wg, wu, wd, rows=None, tail_rows=None, short_last=False, convert=None, tm=512, tf=1024):
    m, d = h.shape
    m = m if rows is None else rows
    assert m % tm == 0, (m, tm)
    n_main = m // tm
    n_tile = n_main + (tail_rows is not None)
    ff = wg.shape[-1]
    nf = pl.cdiv(ff, tf)
    tail = ff - (nf - 1) * tf
    assert tail % LANES == 0 and nf >= 2, ff
    assert short_last or tail_rows is None
    blk = lambda i, f: _ffn_block(i, f, nf - 1)
    in_specs = [pl.BlockSpec((tm, d), lambda i, f: (jnp.minimum(i, n_main - 1), 0))]
    args = [h]
    if tail_rows is not None:
        assert tail_rows.shape == (tm, d)
        in_specs.append(_resident((tm, d)))
        args.append(tail_rows)
    in_specs += [pl.BlockSpec((1, d), lambda i, f: (0, 0)),
                 pl.BlockSpec((d, tf), lambda i, f: (0, blk(i, f))),
                 pl.BlockSpec((d, tf), lambda i, f: (0, blk(i, f))),
                 pl.BlockSpec((tf, d), lambda i, f: (blk(i, f), 0))]
    args += [g, wg, wu, wd]
    out_specs = [pl.BlockSpec((tm, d), lambda i, f: (i, 0))]
    out_shape = [jax.ShapeDtypeStruct((n_tile * tm, d), F32)]
    turns = []
    cast_outside = None
    if convert is not None:
        assert d % CVT_ROWS_WIDE == 0 and ff % LANES == 0
        jobs = [(w, CVT_ROWS_WIDE if w.shape[2] == ff else LANES) for w in convert[0]]
        if sum(w.shape[1] // r for w, r in jobs) > n_tile * nf:
            cast_outside = tuple(w[convert[1]].astype(BF16) for w in convert[0])
            convert = None
    if convert is not None:
        layer = convert[1]
        start = 0
        for w, r in jobs:
            count, width = w.shape[1] // r, w.shape[2]
            s = lambda i, f, start=start, count=count: _cvt_slab(i, f, nf, start, count)
            in_specs.append(pl.BlockSpec((None, r, width), lambda i, f, s=s: (layer, s(i, f), 0)))
            out_specs.append(pl.BlockSpec((r, width), lambda i, f, s=s: (s(i, f), 0)))
            out_shape.append(jax.ShapeDtypeStruct(w.shape[1:], BF16))
            args.append(w)
            turns.append((start, count))
            start += count
    outs = pl.pallas_call(
        functools.partial(_ffn_kernel, tail, n_tile - 1 if short_last else None, tail_rows is not None, tuple(turns)),
        grid=(n_tile, nf),
        in_specs=in_specs,
        out_specs=out_specs,
        out_shape=out_shape,
        scratch_shapes=[pltpu.VMEM((tm, d), BF16)],
        compiler_params=_cparams(("arbitrary", "arbitrary")),
        name="ffn",
    )(*args)
    return outs[0], (tuple(outs[1:]) if convert is not None else cast_outside)


def _oproj_kernel(h_ref, o_ref, w_ref, out_ref):
    out_ref[...] = h_ref[...] + jnp.dot(o_ref[...], w_ref[...], preferred_element_type=F32)


def _oproj(h, o, w, tm=512):
    m, d = h.shape
    assert m % tm == 0, (m, tm)
    return pl.pallas_call(
        _oproj_kernel,
        grid=(m // tm,),
        in_specs=[
            pl.BlockSpec((tm, d), lambda i: (i, 0)),
            pl.BlockSpec((tm, o.shape[1]), lambda i: (i, 0)),
            _resident(w.shape),
        ],
        out_specs=pl.BlockSpec((tm, d), lambda i: (i, 0)),
        out_shape=jax.ShapeDtypeStruct((m, d), F32),
        compiler_params=_cparams(("parallel",)),
        name="oproj",
    )(h, o, w)


def _aproj_kernel(h_ref, g_ref, wq_ref, wk_ref, wvt_ref, wqi_ref, wkw_ref, qn_ref, kn_ref, ikn_ref,
                  c128_ref, a128_ref, b128_ref, c64_ref, a64_ref, b64_ref,
                  q_ref, k_ref, vt_ref, qi_ref, ki2_ref, wit_ref):
    tm = h_ref.shape[0]
    hn = _rms(h_ref[...], g_ref[...]).astype(BF16)
    c128, a128, b128 = c128_ref[...], a128_ref[...], b128_ref[...]
    c64, a64, b64 = c64_ref[...], a64_ref[...], b64_ref[...]
    half128 = A_HEAD_DIM // ROPE_FRACTION // 2
    half64 = IDX_DIM // ROPE_FRACTION // 2

    qf = jnp.dot(hn, wq_ref[...], preferred_element_type=F32)
    for hd in range(A_HEADS):
        sl = slice(hd * LANES, (hd + 1) * LANES)
        q_ref[:, sl] = (_rope(_rms(qf[:, sl], qn_ref[...]), c128, a128, b128, half128) * A_LOGIT_SCALE2).astype(BF16)

    kf = jnp.dot(hn, wk_ref[...], preferred_element_type=F32)
    for hd in range(A_KV_HEADS):
        sl = slice(hd * LANES, (hd + 1) * LANES)
        k_ref[:, sl] = _rope(_rms(kf[:, sl], kn_ref[...]), c128, a128, b128, half128).astype(BF16)

    vt = _nt(wvt_ref[...], hn)
    tail = jnp.where(lax.broadcasted_iota(jnp.int32, (VT_ROWS - A_HEAD_DIM, LANES), 0) == 0, 1.0, 0.0).astype(BF16)
    for c in range(tm // LANES):
        for n in range(A_KV_HEADS):
            vt_ref[c, n * VT_ROWS:n * VT_ROWS + A_HEAD_DIM, :] = (
                vt[n * A_HEAD_DIM:(n + 1) * A_HEAD_DIM, c * LANES:(c + 1) * LANES].astype(BF16))
            vt_ref[c, n * VT_ROWS + A_HEAD_DIM:(n + 1) * VT_ROWS, :] = tail

    qif = jnp.dot(hn, wqi_ref[...], preferred_element_type=F32)
    for s in range(IDX_HEADS * IDX_DIM // LANES):
        sl = slice(s * LANES, (s + 1) * LANES)
        qi_ref[:, sl] = _rope(qif[:, sl], c64, a64, b64, half64).astype(BF16)

    kw = jnp.dot(hn, wkw_ref[...], preferred_element_type=F32)
    lane = lax.broadcasted_iota(jnp.int32, kw.shape, 1)
    ms = jnp.sum(jnp.where(lane < IDX_DIM, kw * kw, 0.0), axis=-1, keepdims=True) * (1.0 / IDX_DIM)
    ki = _rope(kw * lax.rsqrt(ms + EPS) * ikn_ref[...], c64, a64, b64, half64)
    ki2_ref[:, :LANES] = ki.astype(BF16)
    ki2_ref[:, LANES:] = pltpu.roll(ki, IDX_DIM, 1).astype(BF16)
    wit_ref[...] = kw.T[IDX_DIM:IDX_DIM + IDX_HEADS, :] * ((IDX_HEADS * IDX_DIM) ** -0.5)


def _tab_spec(tm, s_len, n_x):
    per, n_main = s_len // tm, n_x // tm
    assert s_len % tm == 0 and n_x % s_len == 0
    return pl.BlockSpec((tm, LANES), lambda i: (jnp.where(i < n_main, i % per, per + i - n_main), 0))


def _aproj(h, g, wq, wk, wvt, wqi, wkw, qn, kn, ikn, t128, t64, s_len, tm=256):
    m, d = h.shape
    assert m % tm == 0, (m, tm)
    row = lambda w: pl.BlockSpec((tm, w), lambda i: (i, 0))
    tab = _tab_spec(tm, s_len, m - TAIL_ROWS)
    vec = lambda w: pl.BlockSpec((1, w), lambda i: (0, 0))
    kvd = wk.shape[1]
    return pl.pallas_call(
        _aproj_kernel,
        grid=(m // tm,),
        in_specs=[row(d), vec(d), _resident(wq.shape), _resident(wk.shape), _resident(wvt.shape),
                  _resident(wqi.shape), _resident(wkw.shape), vec(LANES), vec(LANES), vec(LANES),
                  tab, tab, tab, tab, tab, tab],
        out_specs=[row(wq.shape[1]), row(kvd),
                   pl.BlockSpec((tm // LANES, A_KV_HEADS * VT_ROWS, LANES), lambda i: (i, 0, 0)),
                   row(wqi.shape[1]), row(2 * LANES),
                   pl.BlockSpec((IDX_HEADS, tm), lambda i: (0, i))],
        out_shape=[jax.ShapeDtypeStruct((m, wq.shape[1]), BF16),
                   jax.ShapeDtypeStruct((m, kvd), BF16),
                   jax.ShapeDtypeStruct((m // LANES, A_KV_HEADS * VT_ROWS, LANES), BF16),
                   jax.ShapeDtypeStruct((m, wqi.shape[1]), BF16),
                   jax.ShapeDtypeStruct((m, 2 * LANES), BF16),
                   jax.ShapeDtypeStruct((IDX_HEADS, m), F32)],
        compiler_params=_cparams(("parallel",)),
        name="aproj",
    )(h, g, wq, wk, wvt, wqi, wkw, qn, kn, ikn, *t128, *t64)


def _dsa_kernel(topk, n_xblk, nq, qi_ref, wit_ref, q_ref, kix_ref, kim_ref, kx_ref, km_ref, vtx_ref, vtm_ref,
                o_ref, score_ref, acc_ref):
    qb = q_ref.shape[0]
    ck = KEY_CHUNKS * LANES
    s_len = kx_ref.shape[0]
    g = pl.program_id(0)
    is_tail = g >= n_xblk
    j = lax.rem(g, nq)
    npair = jnp.where(is_tail, 0, (j + KEY_CHUNKS) // KEY_CHUNKS)
    kf = jnp.float32(topk)
    group = A_HEADS // A_KV_HEADS
    n_slab = IDX_HEADS * IDX_DIM // LANES
    n_pos = s_len + N_META

    lane = lax.broadcasted_iota(jnp.int32, (1, qb), 1)
    tail_pos = jnp.where(g == n_xblk, jnp.maximum(lane - META_ROW0, 0), 0)
    qpos = jnp.where(is_tail, tail_pos, N_META + j * qb + lane)
    kpos_m = lax.broadcasted_iota(jnp.int32, (N_META, qb), 0)
    meta_rows = slice(META_ROW0, LANES)
    kpos_x = lax.broadcasted_iota(jnp.int32, (ck, qb), 0) + N_META

    def pair_row(i):
        return pl.multiple_of(i * ck, ck)

    qi = qi_ref[...]
    qs = jnp.concatenate([qi[:, s * LANES:(s + 1) * LANES] for s in range(n_slab)], axis=0)

    def scores(kc, kpos):
        n = kc.shape[0]
        kcat = jnp.concatenate([kc[:, :LANES], kc[:, LANES:]], axis=0)
        s2 = _nt(kcat, qs)
        acc = jnp.zeros((n, qb), F32)
        for s in range(n_slab):
            for par in range(2):
                blk = s2[par * n:(par + 1) * n, s * qb:(s + 1) * qb]
                acc = acc + jnp.maximum(blk, 0.0) * wit_ref[2 * s + par:2 * s + par + 1, :]
        vis = kpos <= qpos
        lo = jnp.min(jnp.where(vis, acc, jnp.inf), axis=0, keepdims=True)
        hi = jnp.max(jnp.where(vis, acc, -jnp.inf), axis=0, keepdims=True)
        return jnp.where(vis, acc, -jnp.inf), lo, hi

    sc, mn, mx = scores(kim_ref[meta_rows, :], kpos_m)
    score_ref[pl.ds(s_len, N_META), :] = sc

    def score_pair(i, carry):
        mn, mx = carry
        r0 = pair_row(i)
        sc, lo, hi = scores(kix_ref[pl.ds(r0, ck), :], kpos_x + i * ck)
        score_ref[pl.ds(r0, ck), :] = sc
        return jnp.minimum(mn, lo), jnp.maximum(mx, hi)

    mn, mx = lax.fori_loop(0, npair, score_pair, (mn, mx))

    n_part = 8

    def reduce_keys(fn, init, combine):
        def part(r0, n, kpos):
            return fn(score_ref[pl.ds(r0, n), :], kpos.astype(F32))
        acc = jnp.concatenate([part(s_len, N_META, kpos_m), jnp.full((n_part * 8 - N_META, qb), init, F32)], axis=0)
        return lax.fori_loop(
            0, npair,
            lambda i, a: combine(a, part(pair_row(i), ck, kpos_x + i * ck).reshape(ck // (8 * n_part), n_part * 8, qb)),
            acc)

    def count(pred):
        acc = reduce_keys(lambda blk, kidx: jnp.where(pred(blk, kidx), 1.0, 0.0), 0.0,
                          lambda a, x: a + jnp.sum(x, axis=0))
        return jnp.sum(acc, axis=0, keepdims=True)

    def min_where(pred):
        acc = reduce_keys(lambda blk, kidx: jnp.where(pred(blk, kidx), blk, jnp.inf), jnp.inf,
                          lambda a, x: jnp.minimum(a, jnp.min(x, axis=0)))
        return jnp.min(acc, axis=0, keepdims=True)

    def bisect(_, lohi):
        lo, hi = lohi
        mid = 0.5 * lo + 0.5 * hi
        ok = count(lambda blk, kidx: blk >= mid) >= kf
        return jnp.where(ok, mid, lo), jnp.where(ok, hi, mid)

    lo, _ = lax.fori_loop(0, N_BISECT, bisect, (mn, mx))
    c_lo = count(lambda blk, kidx: blk >= lo)
    inexact = jnp.max(jnp.where(c_lo > kf, 1.0, 0.0)) > 0.0

    def exact_threshold(lo):
        def step(state):
            lo, _ = state
            m = min_where(lambda blk, kidx: blk >= lo)
            adv = count(lambda blk, kidx: blk > m) >= kf
            nxt = min_where(lambda blk, kidx: blk > m)
            return jnp.where(adv, nxt, lo), jnp.max(jnp.where(adv, 1.0, 0.0))

        lo, _ = lax.while_loop(lambda st: st[1] > 0.0, step, (lo, jnp.float32(1.0)))
        m = min_where(lambda blk, kidx: blk >= lo)
        r = kf - count(lambda blk, kidx: blk > m)
        tied = count(lambda blk, kidx: blk == m) > r

        def idx_bisect(_, ab):
            a, b = ab
            mid = jnp.floor(0.5 * (a + b))
            ok = count(lambda blk, kidx: (blk == m) & (kidx <= mid)) >= r
            return jnp.where(ok, a, mid), jnp.where(ok, mid, b)

        def tie_cut(_):
            _, b = lax.fori_loop(0, int(math.ceil(math.log2(n_pos))) + 1, idx_bisect,
                                 (jnp.full((1, qb), -1.0, F32), jnp.full((1, qb), float(n_pos - 1), F32)))
            return jnp.where(tied, b, float(n_pos))

        any_tied = jnp.max(jnp.where(tied, 1.0, 0.0)) > 0.0
        return m, lax.cond(any_tied, tie_cut, lambda _: jnp.full((1, qb), float(n_pos), F32), 0)

    thr, cidx = lax.cond(inexact, exact_threshold,
                         lambda lo: (lo, jnp.full((1, qb), float(n_pos), F32)), lo)

    qg = [jnp.concatenate([q_ref[:, (n * group + h) * LANES:(n * group + h + 1) * LANES] for h in range(group)], axis=0)
          for n in range(A_KV_HEADS)]

    def logits(kcs):
        return tuple(_nt(kcs[n], qg[n]) for n in range(A_KV_HEADS))

    def attend(ms, sts, blk, kpos, vtcs, first=False):
        sel = (blk > thr) | ((blk == thr) & (kpos.astype(F32) <= cidx))
        bias = jnp.where(sel, 0.0, -jnp.inf)
        bias = jnp.concatenate([bias] * group, axis=1)
        new_ms, ps, alphas = [], [], []
        for n in range(A_KV_HEADS):
            st = sts[n] + bias
            m_new = jnp.maximum(ms[n], jnp.max(st, axis=0, keepdims=True))
            ps.append(jnp.exp2(st - m_new).astype(BF16))
            alphas.append(jnp.exp2(ms[n] - m_new))
            new_ms.append(m_new)
        for n in range(A_KV_HEADS):
            p = ps[n]
            if p.shape[0] < vtcs[n].shape[1]:
                p = jnp.concatenate([jnp.zeros((vtcs[n].shape[1] - p.shape[0], p.shape[1]), BF16), p], axis=0)
            pv = jnp.dot(vtcs[n], p, preferred_element_type=F32)
            acc_ref[n] = pv if first else alphas[n] * acc_ref[n] + pv
        return tuple(new_ms)

    head = lambda n: slice(n * LANES, (n + 1) * LANES)
    vrow = lambda n: slice(n * VT_ROWS, (n + 1) * VT_ROWS)
    ms = tuple(jnp.full((1, group * qb), NEG_INIT, F32) for _ in range(A_KV_HEADS))
    ms = attend(ms, logits([km_ref[meta_rows, head(n)] for n in range(A_KV_HEADS)]),
                score_ref[pl.ds(s_len, N_META), :], kpos_m,
                [vtm_ref[0, vrow(n), :] for n in range(A_KV_HEADS)], first=True)

    def attend_pair(i, ms):
        r0 = pair_row(i)
        return attend(ms, logits([kx_ref[pl.ds(r0, ck), head(n)] for n in range(A_KV_HEADS)]),
                      score_ref[pl.ds(r0, ck), :], kpos_x + i * ck,
                      [jnp.concatenate([vtx_ref[KEY_CHUNKS * i + c, vrow(n), :] for c in range(KEY_CHUNKS)], axis=1)
                       for n in range(A_KV_HEADS)])

    lax.fori_loop(0, npair, attend_pair, ms)

    for n in range(A_KV_HEADS):
        a = acc_ref[n]
        ot = a[:A_HEAD_DIM] / a[A_HEAD_DIM:A_HEAD_DIM + 1]
        for h in range(group):
            hd = n * group + h
            o_ref[:, hd * LANES:(hd + 1) * LANES] = ot[:, h * qb:(h + 1) * qb].T.astype(BF16)


def _dsa(qi, wit, q, ki2, k, vt, bsz, s_len, topk):
    qb = LANES
    assert s_len % (KEY_CHUNKS * qb) == 0, s_len
    nq = s_len // qb
    n_xblk = bsz * nq
    m = q.shape[0]
    n_blk = m // qb
    kvd = k.shape[1]
    group = A_HEADS // A_KV_HEADS
    qrow = lambda w: pl.BlockSpec((qb, w), lambda g: (g, 0))
    batch = lambda g: jnp.minimum(g // nq, bsz - 1)
    return pl.pallas_call(
        functools.partial(_dsa_kernel, topk, n_xblk, nq),
        grid=(n_blk,),
        in_specs=[qrow(qi.shape[1]),
                  pl.BlockSpec((IDX_HEADS, qb), lambda g: (0, g)),
                  qrow(q.shape[1]),
                  pl.BlockSpec((s_len, 2 * LANES), lambda g: (batch(g), 0)),
                  pl.BlockSpec((qb, 2 * LANES), lambda g: (n_xblk, 0)),
                  pl.BlockSpec((s_len, kvd), lambda g: (batch(g), 0)),
                  pl.BlockSpec((qb, kvd), lambda g: (n_xblk, 0)),
                  pl.BlockSpec((nq, vt.shape[1], LANES), lambda g: (batch(g), 0, 0)),
                  pl.BlockSpec((1, vt.shape[1], LANES), lambda g: (n_xblk, 0, 0))],
        out_specs=qrow(q.shape[1]),
        out_shape=jax.ShapeDtypeStruct((m, q.shape[1]), BF16),
        scratch_shapes=[pltpu.VMEM((s_len + qb, qb), F32),
                        pltpu.VMEM((A_KV_HEADS, VT_ROWS, group * qb), F32)],
        compiler_params=_cparams(("arbitrary",)),
        name="dsa",
    )(qi, wit, q, ki2, ki2, k, k, vt, vt)


def _rms64_tiles(x, g, pavg):
    x2 = x * x
    hi = x2.astype(BF16)
    lo = (x2 - hi.astype(F32)).astype(BF16)
    ms = jnp.dot(hi, pavg, preferred_element_type=F32) + jnp.dot(lo, pavg, preferred_element_type=F32)
    y = x * lax.rsqrt(ms + EPS)
    return [y[:, t * LANES:(t + 1) * LANES] * g for t in range(2)]


def _split_halves(x, par):
    lane = lax.broadcasted_iota(jnp.int32, x.shape, 1)
    keep = jnp.where(lane < B_HEAD_DIM if par == 0 else lane >= B_HEAD_DIM, x, 0.0)
    moved = pltpu.roll(keep, B_HEAD_DIM, 1)
    return (keep, moved) if par == 0 else (moved, keep)


def _kvproj_kernel(h_ref, g_ref, w_ref, kn_ref, pavg_ref, c64_ref, a64_ref, b64_ref, k2_ref, v2_ref):
    hn = _rms(h_ref[...], g_ref[...]).astype(BF16)
    kv = jnp.dot(hn, w_ref[...], preferred_element_type=F32)
    half = B_HEAD_DIM // ROPE_FRACTION // 2
    kvd = B_KV_HEADS * B_HEAD_DIM
    assert kvd == 2 * LANES
    k_tiles = _rms64_tiles(kv[:, :kvd], kn_ref[...], pavg_ref[...])
    for s in range(kvd // LANES):
        ks = _rope(k_tiles[s], c64_ref[...], a64_ref[...], b64_ref[...], half)
        vs = kv[:, kvd + s * LANES:kvd + (s + 1) * LANES]
        for par in range(2):
            n = 2 * s + par
            for src, dst in ((ks, k2_ref), (vs, v2_ref)):
                low, high = _split_halves(src, par)
                dst[:, (2 * n) * LANES:(2 * n + 1) * LANES] = low.astype(BF16)
                dst[:, (2 * n + 1) * LANES:(2 * n + 2) * LANES] = high.astype(BF16)


def _kvproj(h, g, w, kn, pavg, t64, s_len, tm=512):
    m, d = h.shape
    assert m % tm == 0, (m, tm)
    row = lambda w_: pl.BlockSpec((tm, w_), lambda i: (i, 0))
    vec = lambda w_: pl.BlockSpec((1, w_), lambda i: (0, 0))
    tab = _tab_spec(tm, s_len, m - TAIL_ROWS)
    wide = B_KV_HEADS * 2 * LANES
    return pl.pallas_call(
        _kvproj_kernel,
        grid=(m // tm,),
        in_specs=[row(d), vec(d), _resident(w.shape), vec(LANES), _resident(pavg.shape), tab, tab, tab],
        out_specs=[row(wide), row(wide)],
        out_shape=[jax.ShapeDtypeStruct((m, wide), BF16)] * 2,
        compiler_params=_cparams(("parallel",)),
        name="kvproj",
    )(h, g, w, kn, pavg, *t64)


def _bqproj_kernel(h_ref, g_ref, w_ref, qn_ref, pavg_ref, c64_ref, a64_ref, b64_ref, q_ref):
    hn = _rms(h_ref[...], g_ref[...]).astype(BF16)
    qf = jnp.dot(hn, w_ref[...], preferred_element_type=F32)
    half = B_HEAD_DIM // ROPE_FRACTION // 2
    scale = B_LOGIT_SCALE2
    for s in range(B_HEADS * B_HEAD_DIM // (2 * LANES)):
        tiles = _rms64_tiles(qf[:, 2 * s * LANES:(2 * s + 2) * LANES], qn_ref[...], pavg_ref[...])
        for t in range(2):
            sl = slice((2 * s + t) * LANES, (2 * s + t + 1) * LANES)
            y = _rope(tiles[t], c64_ref[...], a64_ref[...], b64_ref[...], half)
            q_ref[:, sl] = (y * scale).astype(BF16)


def _bqproj(h, g, w, qn, pavg, t64, s_len, tm=512):
    m, d = h.shape
    assert m % tm == 0, (m, tm)
    row = lambda w_: pl.BlockSpec((tm, w_), lambda i: (i, 0))
    vec = lambda w_: pl.BlockSpec((1, w_), lambda i: (0, 0))
    tab = _tab_spec(tm, s_len, m - TAIL_ROWS)
    return pl.pallas_call(
        _bqproj_kernel,
        grid=(m // tm,),
        in_specs=[row(d), vec(d), _resident(w.shape), vec(LANES), _resident(pavg.shape), tab, tab, tab],
        out_specs=row(w.shape[1]),
        out_shape=jax.ShapeDtypeStruct((m, w.shape[1]), BF16),
        compiler_params=_cparams(("parallel",)),
        name="bqproj",
    )(h, g, w, qn, pavg, *t64)


def _swa_kernel(n_xblk, nq, sink_ref, q_ref, kp_ref, kc_ref, vp_ref, vc_ref, o_ref):
    qb = q_ref.shape[0]
    g = pl.program_id(0)
    per_kv = B_HEADS // B_KV_HEADS
    n_tile = per_kv // 2
    r = lax.broadcasted_iota(jnp.int32, (qb, 2 * qb), 0)
    col = lax.broadcasted_iota(jnp.int32, (qb, 2 * qb), 1)
    rel = qb + r - col
    first_x = jnp.where(lax.rem(g, nq) == 0, META_ROW0, 0)
    first_tail = jnp.where(g == n_xblk, qb + META_ROW0, qb)
    first_col = jnp.where(g >= n_xblk, first_tail, first_x)
    mask = (rel >= 0) & (rel < WINDOW) & (col >= first_col)
    for n in range(B_KV_HEADS):
        lo = slice(2 * n * LANES, (2 * n + 1) * LANES)
        hi = slice((2 * n + 1) * LANES, (2 * n + 2) * LANES)
        kcat = jnp.concatenate([kp_ref[:, lo], kc_ref[:, lo], kp_ref[:, hi], kc_ref[:, hi]], axis=0)
        vcat = jnp.concatenate([vp_ref[:, lo], vc_ref[:, lo], vp_ref[:, hi], vc_ref[:, hi]], axis=0)
        qst = jnp.concatenate([q_ref[:, (n * n_tile + s) * LANES:(n * n_tile + s + 1) * LANES] for s in range(n_tile)], axis=0)
        logits = _nt(qst, kcat)
        rows, inv = [], []
        low_half = lax.broadcasted_iota(jnp.int32, (qb, LANES), 1) < B_HEAD_DIM
        for s in range(n_tile):
            halves, dens = [], []
            for par in range(2):
                sink = sink_ref[n * per_kv + 2 * s + par] * math.log2(math.e)
                lg = jnp.where(mask, logits[s * qb:(s + 1) * qb, par * 2 * qb:(par + 1) * 2 * qb], -jnp.inf)
                m = jnp.maximum(jnp.max(lg, axis=-1, keepdims=True), sink)
                e = jnp.exp2(lg - m)
                dens.append(jnp.sum(e, axis=-1, keepdims=True) + jnp.exp2(sink - m))
                halves.append(e.astype(BF16))
            rows.append(jnp.concatenate(halves, axis=1))
            inv.append(jnp.where(low_half, 1.0 / dens[0], 1.0 / dens[1]))
        p2 = jnp.concatenate(rows, axis=0)
        o = jnp.dot(p2, vcat, preferred_element_type=F32)
        for s in range(n_tile):
            t = n * n_tile + s
            o_ref[:, t * LANES:(t + 1) * LANES] = (o[s * qb:(s + 1) * qb] * inv[s]).astype(BF16)


def _swa(sinks, q, k2, v2, bsz, s_len):
    qb = LANES
    nq = s_len // qb
    n_xblk = bsz * nq
    m, width = q.shape
    wide = k2.shape[1]
    cur = lambda w: pl.BlockSpec((qb, w), lambda g: (g, 0))
    prev = lambda w: pl.BlockSpec(
        (qb, w), lambda g: (jnp.where((g >= n_xblk) | (lax.rem(g, nq) == 0), n_xblk, g - 1), 0))
    return pl.pallas_call(
        functools.partial(_swa_kernel, n_xblk, nq),
        grid=(m // qb,),
        in_specs=[pl.BlockSpec(memory_space=pltpu.SMEM), cur(width), prev(wide), cur(wide), prev(wide), cur(wide)],
        out_specs=cur(width),
        out_shape=jax.ShapeDtypeStruct((m, width), BF16),
        compiler_params=_cparams(("parallel",)),
        name="swa",
    )(sinks, q, k2, k2, v2, v2)


def _rope_tables(pos, head_dim):
    half = head_dim // ROPE_FRACTION // 2
    inv = ROPE_THETA ** (-jnp.arange(half, dtype=F32) / half)
    ang = pos.astype(F32)[:, None] * inv[None, :]
    cos, sin = jnp.cos(ang), jnp.sin(ang)
    lp = jnp.arange(LANES) % head_dim
    idx = lp % half
    c = jnp.where(lp < 2 * half, cos[:, idx], 1.0)
    a = jnp.where(lp < half, -sin[:, idx], 0.0)
    b = jnp.where((lp >= half) & (lp < 2 * half), sin[:, idx], 0.0)
    return c, a, b


def _pad_cols(w, mult):
    pad = (-w.shape[-1]) % mult
    return jnp.pad(w, ((0, 0), (0, pad))) if pad else w


def kernel(x, meta_tokens, ffn1_norm, ffn1_w_gate, ffn1_w_up, ffn1_w_down, ffn2_norm, ffn2_w_gate, ffn2_w_up, ffn2_w_down, a_norm, a_w_in, a_q_norm, a_k_norm, a_idx_k_norm, a_w_out, kv_norm, kv_w, kv_k_norm, b_norm, b_w_q, b_q_norm, b_sinks, b_w_out):
    bsz, s_len, d = x.shape
    depth = ffn1_norm.shape[0]
    n_a = a_norm.shape[0]
    topk = min(TOPK_MAX, s_len // 4)
    n_x = bsz * s_len
    assert n_x % TAIL_ROWS == 0 and meta_tokens.shape[0] == N_META

    zeros = lambda n: jnp.zeros((n, d), x.dtype)
    tail_rows = jnp.concatenate([zeros(META_ROW0), meta_tokens.astype(x.dtype), zeros(TAIL_ROWS - LANES)], axis=0)
    h = x.reshape(n_x, d)
    seq_pos = N_META + jnp.arange(s_len, dtype=jnp.int32)
    tail_pos = jnp.clip(jnp.arange(TAIL_ROWS, dtype=jnp.int32) - META_ROW0, 0, N_META - 1)
    tail_pos = jnp.where(jnp.arange(TAIL_ROWS) < LANES, tail_pos, 0)
    pos = jnp.concatenate([seq_pos, tail_pos])
    t128 = _rope_tables(pos, A_HEAD_DIM)
    t64 = _rope_tables(pos, B_HEAD_DIM)
    seg = jnp.arange(2 * LANES) // B_HEAD_DIM
    pavg = jnp.where(seg[:, None] == seg[None, :], 1.0 / B_HEAD_DIM, 0.0).astype(BF16)
    tile2 = lambda g: jnp.tile(g.astype(F32), 2)[None, :]

    ffn_w = []
    for layer in range(depth):
        ffn_w += [((ffn1_w_gate, ffn1_w_up, ffn1_w_down), layer), ((ffn2_w_gate, ffn2_w_up, ffn2_w_down), layer)]
    state = {"w": tuple(w[0].astype(BF16) for w in ffn_w[0][0]), "k": 0}

    def ffn(h, g, **kw):
        k = state["k"]
        nxt = ffn_w[k + 1] if k + 1 < len(ffn_w) else None
        h, w_next = _ffn(h, g[None, :], *state["w"], convert=nxt, **kw)
        state["w"], state["k"] = w_next, k + 1
        return h

    q_dim, kv_dim, qi_dim = A_HEADS * A_HEAD_DIM, A_KV_HEADS * A_HEAD_DIM, IDX_HEADS * IDX_DIM
    k2 = v2 = None
    for layer in range(depth):
        if layer == n_a:
            k2, v2 = _kvproj(h, kv_norm[None, :], kv_w.astype(BF16), tile2(kv_k_norm), pavg, t64, s_len)
        h = ffn(h, ffn1_norm[layer], tail_rows=tail_rows if layer == 0 else None, short_last=True)
        if layer < n_a:
            i = layer
            w = a_w_in[i].astype(BF16)
            o0, o1, o2, o3 = q_dim, q_dim + kv_dim, q_dim + 2 * kv_dim, q_dim + 2 * kv_dim + qi_dim
            wkw = _pad_cols(w[:, o3:], LANES)
            ikn = _pad_cols(a_idx_k_norm[i][None, :], LANES)
            q, k, vt, qi, ki2, wit = _aproj(
                h, a_norm[i][None, :], w[:, :o0], w[:, o0:o1], w[:, o1:o2].T, w[:, o2:o3], wkw,
                a_q_norm[i][None, :], a_k_norm[i][None, :], ikn, t128, t64, s_len)
            o = _dsa(qi, wit, q, ki2, k, vt, bsz, s_len, topk)
            h = _oproj(h, o, a_w_out[i].astype(BF16))
        else:
            i = layer - n_a
            q = _bqproj(h, b_norm[i][None, :], b_w_q[i].astype(BF16), tile2(b_q_norm[i]), pavg, t64, s_len)
            o = _swa(b_sinks[i].astype(F32), q, k2, v2, bsz, s_len)
            h = _oproj(h, o, b_w_out[i].astype(BF16))
        final = layer == depth - 1
        h = ffn(h, ffn2_norm[layer], rows=n_x if final else None, short_last=not final)
    return h.reshape(bsz, s_len, d)
```
